```python
import jax, jax.numpy as jnp
from jax import lax
import numpy as np

D_MODEL = 2048
BATCH = 2
SEQ = 4096
DEPTH = 4
DEC_BATCH = 8
DEC_SEQ = 8
PAST_LEN = 16384
PAGE_SIZE = 128

N_REC = (DEPTH + 1) // 2
N_ATTN = DEPTH // 2
RWKV_HEADS = 16
RWKV_HD = 64
RWKV_W = RWKV_HEADS * RWKV_HD
LORA_DECAY = 64
LORA_A = 64
LORA_GATE = 128
RWKV_SHIFT_W = 3 * RWKV_W + LORA_DECAY + LORA_A + LORA_GATE
GN_EPS = 64e-5
LRU_W = D_MODEL - RWKV_W
LRU_BLOCKS = 16
LRU_BW = LRU_W // LRU_BLOCKS
CONV_W = 4
LRU_C = 8.0
REC_IN = RWKV_SHIFT_W + 2 * LRU_W
N_HEADS = 16
HEAD_DIM = 128
N_KV = 4
GQA_R = N_HEADS // N_KV
CMP_BLOCK = 32
CMP_STRIDE = 16
SLC_BLOCK = 64
TOP_N = 16
WINDOW = 512
Q_BLOCK = 64
ATTN_IN = N_HEADS * HEAD_DIM + 6 * N_KV * HEAD_DIM + 3 * N_HEADS
SCALE = HEAD_DIM ** -0.5
N_EXPERTS = 16
N_GROUPS = 4
EXPERTS_PER_GROUP = N_EXPERTS // N_GROUPS
TOP_K = 2
D_EXPERT = 1024
LN_EPS = 1e-5
DN_ALPHA = (2 * DEPTH) ** 0.25
DN_BETA = (8 * DEPTH) ** -0.25
NEG = -1e30
BIG = 1e4
F32 = jnp.float32

kernel_name = 'hybrid_rwkv7_rglru_nsa_moe_step'


def layer_norm(x, g, b):
    xf = x.astype(F32)
    mu = xf.mean(-1, keepdims=True)
    var = jnp.square(xf - mu).mean(-1, keepdims=True)
    return ((xf - mu) * lax.rsqrt(var + LN_EPS) * g + b).astype(x.dtype)


def masked_softmax(s, mask):
    s = jnp.where(mask, s, NEG)
    m = jnp.max(s, axis=-1, keepdims=True)
    e = jnp.exp(s - m) * mask
    return e / jnp.maximum(jnp.sum(e, axis=-1, keepdims=True), 1e-30)


def alibi_slopes():
    h = jnp.arange(1, N_HEADS + 1, dtype=F32)
    return (2.0 ** (-8.0 * h / N_HEADS)).reshape(N_KV, GQA_R)


def rwkv7_mix(p, sh0, S0, mu, w0, w_up, a0, a_up, g_up, k_k, k_a, r_k, gn_g, gn_b):
    B, T, _ = p.shape
    p_prev = jnp.concatenate([sh0[:, None, :].astype(p.dtype), p[:, :-1]], axis=1)
    pm = p + mu * (p_prev - p)
    r, k, v, wd, ad, gd = jnp.split(pm, [RWKV_W, 2 * RWKV_W, 3 * RWKV_W, 3 * RWKV_W + LORA_DECAY, 3 * RWKV_W + LORA_DECAY + LORA_A], axis=-1)
    w = -jax.nn.softplus(-(w0 + jnp.tanh(wd) @ w_up)) - 0.5
    decay = jnp.exp(-jnp.exp(w.astype(F32)))
    a = jax.nn.sigmoid(a0 + ad @ a_up)
    g = jax.nn.sigmoid(gd) @ g_up
    heads = lambda t: t.astype(F32).reshape(B, T, RWKV_HEADS, RWKV_HD)
    kk = heads(k * k_k)
    kk = kk * lax.rsqrt(jnp.maximum(jnp.sum(kk * kk, axis=-1, keepdims=True), 1e-24))
    k_eff = heads(k * (1 + (a - 1) * k_a))
    r_h, v_h, a_h, w_h = heads(r), heads(v), heads(a), heads(decay)

    def step(S, inp):
        r_t, w_t, k_t, v_t, kk_t, a_t = inp
        Skk = jnp.einsum('bhvk,bhk->bhv', S, kk_t)
        S = S * w_t[:, :, None, :] - Skk[..., None] * (kk_t * a_t)[:, :, None, :] + v_t[..., None] * k_t[:, :, None, :]
        return S, jnp.einsum('bhvk,bhk->bhv', S, r_t)

    seq_in = tuple(jnp.moveaxis(t, 1, 0) for t in (r_h, w_h, k_eff, v_h, kk, a_h))
    S_fin, y = lax.scan(step, S0.astype(F32), seq_in)
    y = jnp.moveaxis(y, 0, 1)
    mean = y.mean(-1, keepdims=True)
    var = jnp.square(y - mean).mean(-1, keepdims=True)
    y = ((y - mean) * lax.rsqrt(var + GN_EPS)).reshape(B, T, RWKV_W) * gn_g + gn_b
    bonus = (jnp.sum(r_h * k_eff * r_k, axis=-1, keepdims=True) * v_h).reshape(B, T, RWKV_W)
    return (y + bonus) * g, p[:, -1], S_fin


def lru_combine(left, right):
    return (left[0] * right[0], right[0] * left[1] + right[1])


def rglru_mix(px, pg, c0, h0, conv_w, conv_b, wa, ba, wx, bx, lam):
    B, T, _ = px.shape
    xe = jnp.concatenate([c0.astype(px.dtype), px], axis=1)
    xc = conv_b + sum(conv_w[j] * xe[:, j:j + T] for j in range(CONV_W))
    xb = xc.reshape(B, T, LRU_BLOCKS, LRU_BW)
    r = jax.nn.sigmoid(jnp.einsum('btnd,nde->btne', xb, wa).reshape(B, T, LRU_W) + ba)
    i = jax.nn.sigmoid(jnp.einsum('btnd,nde->btne', xb, wx).reshape(B, T, LRU_W) + bx)
    log_a = (-LRU_C * jax.nn.softplus(-lam) * r).astype(F32)
    a = jnp.exp(log_a)
    b = jnp.sqrt(-jnp.expm1(2.0 * log_a)) * (i * xc).astype(F32)
    b = b.at[:, 0].add(a[:, 0] * h0.astype(F32))
    _, h = lax.associative_scan(lru_combine, (a, b), axis=1)
    y = h * jax.nn.gelu(pg.astype(F32))
    return y, xe[:, -(CONV_W - 1):], h[:, -1]


def rec_mixer(x, S0, sh0, h0, c0, w_in, mu, w0, w_up, a0, a_up, g_up, k_k, k_a, r_k, gn_g, gn_b,
              conv_w, conv_b, wa, ba, wx, bx, lam, w_out):
    p = x @ w_in
    p_rw, p_lx, p_lg = jnp.split(p, [RWKV_SHIFT_W, RWKV_SHIFT_W + LRU_W], axis=-1)
    y_rw, sh, S = rwkv7_mix(p_rw, sh0, S0, mu, w0, w_up, a0, a_up, g_up, k_k, k_a, r_k, gn_g, gn_b)
    y_lru, cbuf, h = rglru_mix(p_lx, p_lg, c0, h0, conv_w, conv_b, wa, ba, wx, bx, lam)
    y = jnp.concatenate([y_rw, y_lru], axis=-1).astype(x.dtype) @ w_out
    return y, S, sh, h, cbuf


def attn_project(x, w_in, gate_b):
    B, T, _ = x.shape
    p = x @ w_in
    qw = N_HEADS * HEAD_DIM
    kvw = N_KV * HEAD_DIM
    q = p[..., :qw].reshape(B, T, N_HEADS, HEAD_DIM)
    kv = p[..., qw:qw + 6 * kvw].reshape(B, T, 6, N_KV, HEAD_DIM)
    gates = jax.nn.sigmoid((p[..., qw + 6 * kvw:] + gate_b).astype(F32)).reshape(B, T, 3, N_HEADS)
    return q, kv, gates


def attn_combine(o_c, o_s, o_w, gates, w_out, dtype):
    o = gates[:, :, 0, :, None] * o_c + gates[:, :, 1, :, None] * o_s + gates[:, :, 2, :, None] * o_w
    B, T = o.shape[:2]
    return o.reshape(B, T, N_HEADS * HEAD_DIM).astype(dtype) @ w_out


def nsa_compress(k_full, w_cmp):
    B, L, G, D = k_full.shape
    ratio = CMP_BLOCK // CMP_STRIDE
    nh = L // CMP_STRIDE
    nc = nh - ratio + 1
    pieces = k_full.reshape(B, nh, CMP_STRIDE, G, D)
    w = w_cmp.reshape(ratio, CMP_STRIDE, D, D)
    return sum(jnp.einsum('bnlgd,lde->bnge', pieces[:, j:j + nc], w[j]) for j in range(ratio))


def nsa_core(q, q_pos, kc, vc, ks, vs, slopes):
    B, Q = q.shape[:2]
    NC = kc.shape[1]
    NS = ks.shape[1] // SLC_BLOCK
    qg = q.reshape(B, Q, N_KV, GQA_R, HEAD_DIM)
    sl = slopes[None, None, :, :, None]
    c_start = jnp.arange(NC) * CMP_STRIDE
    dist_c = q_pos[:, None] - (c_start + CMP_BLOCK - 1)[None, :]
    s_c = jnp.einsum('bqgrd,bngd->bqgrn', qg, kc).astype(F32) * SCALE - sl * dist_c[None, :, None, None, :]
    p_c = masked_softmax(s_c, (dist_c >= 0)[None, :, None, None, :])
    o_c = jnp.einsum('bqgrn,bngd->bqgrd', p_c, vc)
    s_start = jnp.arange(NS) * SLC_BLOCK
    overlap = ((c_start[:, None] < s_start[None, :] + SLC_BLOCK) & (c_start[:, None] + CMP_BLOCK > s_start[None, :])).astype(F32)
    imp = jnp.einsum('bqgrn,ns->bqgs', p_c, overlap)
    blk = jnp.arange(NS)
    cur = q_pos // SLC_BLOCK
    forced = (blk[None, :] == 0) | (blk[None, :] == cur[:, None]) | (blk[None, :] == cur[:, None] - 1)
    causal_blk = s_start[None, :] <= q_pos[:, None]
    score = jnp.where(causal_blk[None, :, None, :], imp + BIG * forced.astype(F32)[None, :, None, :], NEG)
    n_sel = min(TOP_N, NS)
    top_val, top_idx = lax.top_k(score, n_sel)
    sel_ok = top_val > 0.5 * NEG
    b_i = jnp.arange(B)[:, None, None, None]
    g_i = jnp.arange(N_KV)[None, None, :, None]
    ks_b = ks.reshape(B, NS, SLC_BLOCK, N_KV, HEAD_DIM)
    vs_b = vs.reshape(B, NS, SLC_BLOCK, N_KV, HEAD_DIM)
    k_sel = ks_b[b_i, top_idx, :, g_i, :]
    v_sel = vs_b[b_i, top_idx, :, g_i, :]
    t_pos = top_idx[..., None] * SLC_BLOCK + jnp.arange(SLC_BLOCK)
    dist_s = q_pos[None, :, None, None, None] - t_pos
    ok_s = (dist_s >= 0) & sel_ok[..., None]
    s_s = jnp.einsum('bqgrd,bqgnld->bqgrnl', qg, k_sel).astype(F32) * SCALE - sl[..., None] * dist_s[:, :, :, None]
    p_s = masked_softmax(s_s.reshape(B, Q, N_KV, GQA_R, -1), ok_s.reshape(B, Q, N_KV, 1, -1))
    o_s = jnp.einsum('bqgrm,bqgmd->bqgrd', p_s, v_sel.reshape(B, Q, N_KV, n_sel * SLC_BLOCK, HEAD_DIM))
    return o_c.reshape(B, Q, N_HEADS, HEAD_DIM), o_s.reshape(B, Q, N_HEADS, HEAD_DIM)


def window_attn(q, q_pos, kw, vw, k_pos, slopes):
    B, Q = q.shape[:2]
    qg = q.reshape(B, Q, N_KV, GQA_R, HEAD_DIM)
    dist = q_pos[:, None] - k_pos[None, :]
    ok = (dist >= 0) & (dist < WINDOW) & (k_pos[None, :] >= 0)
    s = jnp.einsum('bqgrd,bkgd->bqgrk', qg, kw).astype(F32) * SCALE - slopes[None, None, :, :, None] * dist[None, :, None, None, :]
    p = masked_softmax(s, ok[None, :, None, None, :])
    return jnp.einsum('bqgrk,bkgd->bqgrd', p, vw).reshape(B, Q, N_HEADS, HEAD_DIM)


def nsa_prompt(q, kv, w_cmp_k, w_cmp_v, slopes, w_buf):
    B, T = q.shape[:2]
    kc = nsa_compress(kv[:, :, 0], w_cmp_k)
    vc = nsa_compress(kv[:, :, 1], w_cmp_v)
    ks, vs = kv[:, :, 2], kv[:, :, 3]
    win = jnp.pad(kv[:, :, 4:], ((0, 0), (WINDOW, 0), (0, 0), (0, 0), (0, 0)))

    def block(i):
        q0 = i * Q_BLOCK
        qb = lax.dynamic_slice_in_dim(q, q0, Q_BLOCK, axis=1)
        q_pos = q0 + jnp.arange(Q_BLOCK)
        o_c, o_s = nsa_core(qb, q_pos, kc, vc, ks, vs, slopes)
        wb = lax.dynamic_slice_in_dim(win, q0, WINDOW + Q_BLOCK, axis=1)
        k_pos = q0 - WINDOW + jnp.arange(WINDOW + Q_BLOCK)
        o_w = window_attn(qb, q_pos, wb[:, :, 0], wb[:, :, 1], k_pos, slopes)
        return o_c, o_s, o_w

    outs = lax.map(block, jnp.arange(T // Q_BLOCK))
    o_c, o_s, o_w = (jnp.moveaxis(o, 0, 1).reshape(B, T, N_HEADS, HEAD_DIM) for o in outs)
    return o_c, o_s, o_w, win[:, -w_buf:]


def nsa_sample(q, kv_new, past_rows, win_buf, w_cmp_k, w_cmp_v, slopes, past_len):
    B, T = q.shape[:2]
    full = jnp.concatenate([past_rows, kv_new[:, :, :4].astype(past_rows.dtype)], axis=1)
    L = past_len + T
    L_pad = -(-L // SLC_BLOCK) * SLC_BLOCK
    full = jnp.pad(full, ((0, 0), (0, L_pad - L), (0, 0), (0, 0), (0, 0)))
    kc = nsa_compress(full[:, :, 0], w_cmp_k)
    vc = nsa_compress(full[:, :, 1], w_cmp_v)
    q_pos = past_len + jnp.arange(T)
    o_c, o_s = nsa_core(q, q_pos, kc, vc, full[:, :, 2], full[:, :, 3], slopes)
    w_buf = win_buf.shape[1]
    win = jnp.concatenate([win_buf, kv_new[:, :, 4:].astype(win_buf.dtype)], axis=1)
    k_pos = past_len - w_buf + jnp.arange(w_buf + T)
    o_w = window_attn(q, q_pos, win[:, :, 0], win[:, :, 1], k_pos, slopes)
    return o_c, o_s, o_w, win[:, -w_buf:]


def moe(x, w_router, r_bias, w_g, w_u, w_d):
    B, T, D = x.shape
    xf = x.reshape(B * T, D)
    aff = jax.nn.sigmoid((xf @ w_router).astype(F32))
    sel = (aff + r_bias).reshape(-1, N_GROUPS, EXPERTS_PER_GROUP)
    grp_score = lax.top_k(sel, TOP_K)[0].sum(-1)
    g_star = jnp.argmax(grp_score, axis=-1)
    in_grp = jnp.take_along_axis(sel, g_star[:, None, None], axis=1)[:, 0]
    _, loc = lax.top_k(in_grp, TOP_K)
    e_idx = g_star[:, None] * EXPERTS_PER_GROUP + loc
    w_sel = jnp.take_along_axis(aff, e_idx, axis=-1)
    w_sel = w_sel / jnp.sum(w_sel, axis=-1, keepdims=True)
    gate = jnp.sum(jax.nn.one_hot(e_idx, N_EXPERTS, dtype=F32) * w_sel[..., None], axis=1)
    h = jax.nn.silu(jnp.einsum('nd,edf->nef', xf, w_g)) * jnp.einsum('nd,edf->nef', xf, w_u)
    y = jnp.einsum('nef,efd->nd', h * gate[..., None].astype(h.dtype), w_d)
    return y.reshape(B, T, D).astype(x.dtype)


def setup_inputs(seed: int = 0) -> dict:
    key = jax.random.key(seed)
    keys = iter(jax.random.split(key, 64))

    def nrm(shape, scale):
        return jax.random.normal(next(keys), shape, F32) * scale

    def unif(shape, lo, hi):
        return jax.random.uniform(next(keys), shape, F32, lo, hi)

    n_pages = PAST_LEN // PAGE_SIZE
    n_used = DEC_BATCH * n_pages
    n_pool = n_used + max(1, n_used // 4)
    w_buf = min(WINDOW, PAST_LEN)
    sD = D_MODEL ** -0.5
    x_prompt = nrm((BATCH, SEQ, D_MODEL), 1.0)
    x_sample = nrm((DEC_BATCH, DEC_SEQ, D_MODEL), 1.0)
    cache_nsa_kv = nrm((n_pool, PAGE_SIZE, N_ATTN, 4, N_KV, HEAD_DIM), 1.0)
    cache_win_kv = nrm((DEC_BATCH, w_buf, N_ATTN, 2, N_KV, HEAD_DIM), 1.0)
    state_rwkv = nrm((DEC_BATCH, N_REC, RWKV_HEADS, RWKV_HD, RWKV_HD), 0.3)
    state_rwkv_shift = nrm((DEC_BATCH, N_REC, RWKV_SHIFT_W), 1.0)
    state_lru_h = nrm((DEC_BATCH, N_REC, LRU_W), 0.5)
    state_lru_conv = nrm((DEC_BATCH, N_REC, CONV_W - 1, LRU_W), 1.0)
    page_table = jax.random.permutation(next(keys), n_pool)[:n_used].reshape(DEC_BATCH, n_pages).astype(jnp.int32)
    ln1_g = 1.0 + nrm((DEPTH, D_MODEL), 0.02)
    ln1_b = nrm((DEPTH, D_MODEL), 0.02)
    ln2_g = 1.0 + nrm((DEPTH, D_MODEL), 0.02)
    ln2_b = nrm((DEPTH, D_MODEL), 0.02)
    rec_w_in = nrm((N_REC, D_MODEL, REC_IN), sD)
    rec_w_in = rec_w_in.at[:, :, 2 * RWKV_W:3 * RWKV_W].multiply(DN_BETA)
    rec_w_in = rec_w_in.at[:, :, RWKV_SHIFT_W:RWKV_SHIFT_W + LRU_W].multiply(DN_BETA)
    rec_mu = unif((N_REC, RWKV_SHIFT_W), 0.0, 1.0)
    rwkv_w0 = unif((N_REC, RWKV_W), -6.0, -1.0)
    rwkv_w_up = nrm((N_REC, LORA_DECAY, RWKV_W), 0.5 * LORA_DECAY ** -0.5)
    rwkv_a0 = nrm((N_REC, RWKV_W), 0.5)
    rwkv_a_up = nrm((N_REC, LORA_A, RWKV_W), LORA_A ** -0.5)
    rwkv_g_up = nrm((N_REC, LORA_GATE, RWKV_W), LORA_GATE ** -0.5)
    rwkv_k_k = 0.85 + nrm((N_REC, RWKV_W), 0.05)
    rwkv_k_a = 1.0 + nrm((N_REC, RWKV_W), 0.05)
    rwkv_r_k = nrm((N_REC, RWKV_HEADS, RWKV_HD), 0.1)
    rwkv_gn_g = 1.0 + nrm((N_REC, RWKV_W), 0.02)
    rwkv_gn_b = nrm((N_REC, RWKV_W), 0.02)
    lru_conv_w = nrm((N_REC, CONV_W, LRU_W), 0.5)
    lru_conv_b = nrm((N_REC, LRU_W), 0.02)
    lru_wa = nrm((N_REC, LRU_BLOCKS, LRU_BW, LRU_BW), LRU_BW ** -0.5)
    lru_ba = nrm((N_REC, LRU_W), 0.02)
    lru_wx = nrm((N_REC, LRU_BLOCKS, LRU_BW, LRU_BW), LRU_BW ** -0.5)
    lru_bx = nrm((N_REC, LRU_W), 0.02)
    a_base = unif((N_REC, LRU_W), 0.9, 0.999) ** (1.0 / LRU_C)
    lru_lambda = jnp.log(a_base) - jnp.log1p(-a_base)
    rec_w_out = nrm((N_REC, D_MODEL, D_MODEL), sD * DN_BETA)
    attn_w_in = nrm((N_ATTN, D_MODEL, ATTN_IN), sD)
    for j in (1, 3, 5):
        lo = N_HEADS * HEAD_DIM + j * N_KV * HEAD_DIM
        attn_w_in = attn_w_in.at[:, :, lo:lo + N_KV * HEAD_DIM].multiply(DN_BETA)
    attn_gate_b = nrm((N_ATTN, 3 * N_HEADS), 0.1)
    w_cmp_k = nrm((N_ATTN, CMP_BLOCK, HEAD_DIM, HEAD_DIM), (CMP_BLOCK * HEAD_DIM) ** -0.5)
    w_cmp_v = nrm((N_ATTN, CMP_BLOCK, HEAD_DIM, HEAD_DIM), (CMP_BLOCK * HEAD_DIM) ** -0.5)
    attn_w_out = nrm((N_ATTN, D_MODEL, D_MODEL), sD * DN_BETA)
    moe_w_router = nrm((D_MODEL, N_EXPERTS), sD)
    moe_router_bias = nrm((N_EXPERTS,), 0.01)
    moe_w_gate = nrm((DEPTH, N_EXPERTS, D_MODEL, D_EXPERT), sD)
    moe_w_up = nrm((DEPTH, N_EXPERTS, D_MODEL, D_EXPERT), sD)
    moe_w_down = nrm((DEPTH, N_EXPERTS, D_EXPERT, D_MODEL), D_EXPERT ** -0.5 * DN_BETA)
    return {'x_prompt': x_prompt, 'x_sample': x_sample, 'cache_nsa_kv': cache_nsa_kv, 'cache_win_kv': cache_win_kv,
            'state_rwkv': state_rwkv, 'state_rwkv_shift': state_rwkv_shift, 'state_lru_h': state_lru_h,
            'state_lru_conv': state_lru_conv, 'page_table': page_table,
            'ln1_g': ln1_g, 'ln1_b': ln1_b, 'ln2_g': ln2_g, 'ln2_b': ln2_b,
            'rec_w_in': rec_w_in, 'rec_mu': rec_mu, 'rwkv_w0': rwkv_w0, 'rwkv_w_up': rwkv_w_up, 'rwkv_a0': rwkv_a0,
            'rwkv_a_up': rwkv_a_up, 'rwkv_g_up': rwkv_g_up, 'rwkv_k_k': rwkv_k_k, 'rwkv_k_a': rwkv_k_a, 'rwkv_r_k': rwkv_r_k,
            'rwkv_gn_g': rwkv_gn_g, 'rwkv_gn_b': rwkv_gn_b, 'lru_conv_w': lru_conv_w, 'lru_conv_b': lru_conv_b,
            'lru_wa': lru_wa, 'lru_ba': lru_ba, 'lru_wx': lru_wx, 'lru_bx': lru_bx, 'lru_lambda': lru_lambda,
            'rec_w_out': rec_w_out, 'attn_w_in': attn_w_in, 'attn_gate_b': attn_gate_b, 'w_cmp_k': w_cmp_k,
            'w_cmp_v': w_cmp_v, 'attn_w_out': attn_w_out, 'moe_w_router': moe_w_router, 'moe_router_bias': moe_router_bias,
            'moe_w_gate': moe_w_gate, 'moe_w_up': moe_w_up, 'moe_w_down': moe_w_down}


def reference(x_prompt, x_sample, cache_nsa_kv, cache_win_kv, state_rwkv, state_rwkv_shift, state_lru_h, state_lru_conv,
              page_table, ln1_g, ln1_b, ln2_g, ln2_b, rec_w_in, rec_mu, rwkv_w0, rwkv_w_up, rwkv_a0, rwkv_a_up, rwkv_g_up,
              rwkv_k_k, rwkv_k_a, rwkv_r_k, rwkv_gn_g, rwkv_gn_b, lru_conv_w, lru_conv_b, lru_wa, lru_ba, lru_wx, lru_bx,
              lru_lambda, rec_w_out, attn_w_in, attn_gate_b, w_cmp_k, w_cmp_v, attn_w_out, moe_w_router, moe_router_bias,
              moe_w_gate, moe_w_up, moe_w_down):
    slopes = alibi_slopes()
    Bp = x_prompt.shape[0]
    Bs = x_sample.shape[0]
    n_pages = page_table.shape[1]
    past_len = n_pages * cache_nsa_kv.shape[1]
    w_buf = cache_win_kv.shape[1]
    xp, xs = x_prompt, x_sample
    nsa_p, nsa_s, win_p, win_s = [], [], [], []
    rS_p, rS_s, rsh_p, rsh_s, lh_p, lh_s, lc_p, lc_s = [], [], [], [], [], [], [], []
    for layer in range(DEPTH):
        li = layer // 2
        if layer % 2 == 0:
            prm = (rec_w_in[li], rec_mu[li], rwkv_w0[li], rwkv_w_up[li], rwkv_a0[li], rwkv_a_up[li], rwkv_g_up[li],
                   rwkv_k_k[li], rwkv_k_a[li], rwkv_r_k[li], rwkv_gn_g[li], rwkv_gn_b[li], lru_conv_w[li], lru_conv_b[li],
                   lru_wa[li], lru_ba[li], lru_wx[li], lru_bx[li], lru_lambda[li], rec_w_out[li])
            yp, S, sh, h, cb = rec_mixer(xp, jnp.zeros((Bp, RWKV_HEADS, RWKV_HD, RWKV_HD), F32),
                                         jnp.zeros((Bp, RWKV_SHIFT_W), F32), jnp.zeros((Bp, LRU_W), F32),
                                         jnp.zeros((Bp, CONV_W - 1, LRU_W), F32), *prm)
            ys, S2, sh2, h2, cb2 = rec_mixer(xs, state_rwkv[:, li], state_rwkv_shift[:, li], state_lru_h[:, li],
                                             state_lru_conv[:, li], *prm)
            rS_p.append(S); rS_s.append(S2); rsh_p.append(sh); rsh_s.append(sh2)
            lh_p.append(h); lh_s.append(h2); lc_p.append(cb); lc_s.append(cb2)
        else:
            q, kv, g = attn_project(xp, attn_w_in[li], attn_gate_b[li])
            o_c, o_s, o_w, wb = nsa_prompt(q, kv, w_cmp_k[li], w_cmp_v[li], slopes, w_buf)
            yp = attn_combine(o_c, o_s, o_w, g, attn_w_out[li], xp.dtype)
            nsa_p.append(kv[:, :, :4]); win_p.append(wb)
            q2, kv2, g2 = attn_project(xs, attn_w_in[li], attn_gate_b[li])
            past_rows = cache_nsa_kv[page_table, :, li].reshape(Bs, past_len, 4, N_KV, HEAD_DIM)
            o_c2, o_s2, o_w2, wb2 = nsa_sample(q2, kv2, past_rows, cache_win_kv[:, :, li], w_cmp_k[li], w_cmp_v[li],
                                               slopes, past_len)
            ys = attn_combine(o_c2, o_s2, o_w2, g2, attn_w_out[li], xs.dtype)
            nsa_s.append(kv2[:, :, :4]); win_s.append(wb2)
        xp = layer_norm(DN_ALPHA * xp + yp, ln1_g[layer], ln1_b[layer])
        xs = layer_norm(DN_ALPHA * xs + ys, ln1_g[layer], ln1_b[layer])
        xp = layer_norm(DN_ALPHA * xp + moe(xp, moe_w_router, moe_router_bias, moe_w_gate[layer], moe_w_up[layer], moe_w_down[layer]),
                        ln2_g[layer], ln2_b[layer])
        xs = layer_norm(DN_ALPHA * xs + moe(xs, moe_w_router, moe_router_bias, moe_w_gate[layer], moe_w_up[layer], moe_w_down[layer]),
                        ln2_g[layer], ln2_b[layer])
    return (xp, xs, jnp.stack(nsa_p, axis=2), jnp.stack(nsa_s, axis=2), jnp.stack(win_p, axis=2), jnp.stack(win_s, axis=2),
            jnp.stack(rS_p, axis=1), jnp.stack(rS_s, axis=1), jnp.stack(rsh_p, axis=1), jnp.stack(rsh_s, axis=1),
            jnp.stack(lh_p, axis=1), jnp.stack(lh_s, axis=1), jnp.stack(lc_p, axis=1), jnp.stack(lc_s, axis=1))
```

```python
import functools
import math

import jax
import jax.numpy as jnp
from jax import lax
from jax.experimental import pallas as pl
from jax.experimental.pallas import tpu as pltpu

F32 = jnp.float32
BF16 = jnp.bfloat16

D_MODEL = 2048
DEPTH = 4
RWKV_HEADS = 16
RWKV_HD = 64
RWKV_W = RWKV_HEADS * RWKV_HD
LORA_DECAY = 64
LORA_A = 64
LORA_GATE = 128
RWKV_SHIFT_W = 3 * RWKV_W + LORA_DECAY + LORA_A + LORA_GATE
GN_EPS = 64e-5
LRU_W = D_MODEL - RWKV_W
LRU_BLOCKS = 16
CONV_W = 4
LRU_C = 8.0
N_HEADS = 16
HEAD_DIM = 128
N_KV = 4
GQA_R = N_HEADS // N_KV
CMP_BLOCK = 32
CMP_STRIDE = 16
SLC_BLOCK = 64
TOP_N = 16
WINDOW = 512
SCALE = HEAD_DIM ** -0.5
N_EXPERTS = 16
N_GROUPS = 4
EXPERTS_PER_GROUP = N_EXPERTS // N_GROUPS
TOP_K = 2
D_EXPERT = 1024
LN_EPS = 1e-5
DN_ALPHA = (2 * DEPTH) ** 0.25
NEG = -1e30
BIG = 1e4

LANES = 128
SUBLANES = 8
VMEM_LIMIT_BYTES = 48 * 1024 * 1024

RWKV_PAD_W = 3584
PAIRS = RWKV_HEADS // 2


def _cparams(*sem):
    return pltpu.CompilerParams(dimension_semantics=sem, vmem_limit_bytes=VMEM_LIMIT_BYTES)


def _split2(x):
    hi = x.astype(BF16)
    lo = (x - hi.astype(F32)).astype(BF16)
    return hi, lo


def _dot(a, b):
    return jnp.dot(a, b, preferred_element_type=F32)


def _dot2(x, w_bf):
    hi, lo = _split2(x)
    return _dot(hi, w_bf) + _dot(lo, w_bf)


def _sigmoid(x):
    return 1.0 / (1.0 + jnp.exp(-x))


def _softplus(x):
    return jnp.maximum(x, 0.0) + jnp.log(1.0 + jnp.exp(-jnp.abs(x)))


def _mm_kernel(x_ref, w_ref, o_ref):
    o_ref[...] = _dot(x_ref[...].astype(BF16), w_ref[...].astype(BF16))


def _pick(n, cands):
    for c in cands:
        if n % c == 0:
            return c
    return n


def mm(x, w):
    m, k = x.shape
    n = w.shape[1]
    tm = _pick(m, (512, 256, 128, 64, 32, 16, 8))
    tn = _pick(n, (1024, 768, 512, 256, 128))
    return pl.pallas_call(
        _mm_kernel,
        grid=(n // tn, m // tm),
        in_specs=[pl.BlockSpec((tm, k), lambda j, i: (i, 0)),
                  pl.BlockSpec((k, tn), lambda j, i: (0, j))],
        out_specs=pl.BlockSpec((tm, tn), lambda j, i: (i, j)),
        out_shape=jax.ShapeDtypeStruct((m, n), F32),
        compiler_params=_cparams("parallel", "parallel"),
        name="mm",
    )(x, w)


def _mm2_kernel(x1_ref, x2_ref, w1_ref, w2_ref, o_ref):
    o_ref[...] = (_dot(x1_ref[...].astype(BF16), w1_ref[...].astype(BF16))
                  + _dot(x2_ref[...].astype(BF16), w2_ref[...].astype(BF16)))


def mm_cat2(x1, x2, w):
    m, k1 = x1.shape
    k2 = x2.shape[1]
    assert k1 == k2 and w.shape[0] == k1 + k2
    n = w.shape[1]
    tm = _pick(m, (512, 256, 128, 64, 32, 16, 8))
    tn = _pick(n, (1024, 512, 256, 128))
    return pl.pallas_call(
        _mm2_kernel,
        grid=(n // tn, m // tm),
        in_specs=[pl.BlockSpec((tm, k1), lambda j, i: (i, 0)),
                  pl.BlockSpec((tm, k2), lambda j, i: (i, 0)),
                  pl.BlockSpec((k1, tn), lambda j, i: (0, j)),
                  pl.BlockSpec((k2, tn), lambda j, i: (1, j))],
        out_specs=pl.BlockSpec((tm, tn), lambda j, i: (i, j)),
        out_shape=jax.ShapeDtypeStruct((m, n), F32),
        compiler_params=_cparams("parallel", "parallel"),
        name="mm_cat2",
    )(x1, x2, w, w)


def _ln_res_kernel(x_ref, y_ref, g_ref, b_ref, o_ref):
    z = DN_ALPHA * x_ref[...] + y_ref[...]
    mu = jnp.mean(z, axis=-1, keepdims=True)
    zc = z - mu
    var = jnp.mean(zc * zc, axis=-1, keepdims=True)
    o_ref[...] = zc * lax.rsqrt(var + LN_EPS) * g_ref[...] + b_ref[...]


def ln_res(x, y, g, b):
    m, d = x.shape
    tm = _pick(m, (256, 192, 128, 64, 32, 16, 8))
    row = pl.BlockSpec((tm, d), lambda i: (i, 0))
    vec = pl.BlockSpec((1, d), lambda i: (0, 0))
    return pl.pallas_call(
        _ln_res_kernel,
        grid=(m // tm,),
        in_specs=[row, row, vec, vec],
        out_specs=row,
        out_shape=jax.ShapeDtypeStruct((m, d), F32),
        compiler_params=_cparams("parallel"),
        name="ln_res",
    )(x, y, g.reshape(1, d), b.reshape(1, d))


def _shifted(carry_ref, x, j):
    ext = jnp.concatenate([carry_ref[...], x], axis=0)
    return pltpu.roll(ext, j, axis=0)[SUBLANES:, :]


def _rwkv_prep_kernel(p_ref, sh0_ref, mu_ref, w0_ref, a0_ref, wlora_ref, alora_ref, gup_ref,
                      kk_ref, ka_ref, rk_ref, bd_ref,
                      r_o, w_o, k_o, kn_o, b_o, v_o, wr_o, br_o, kr_o, g_o, bonus_o, carry):
    ti = pl.program_id(1)

    @pl.when(ti == 0)
    def _():
        carry[...] = jnp.broadcast_to(sh0_ref[0], carry.shape)

    p = p_ref[...]
    tm = p.shape[0]
    p_prev = _shifted(carry, p, 1)
    carry[...] = p[tm - SUBLANES:, :]
    pm = p + mu_ref[...] * (p_prev - p)
    W = RWKV_W
    r = pm[:, 0:W]
    k = pm[:, W:2 * W]
    v = pm[:, 2 * W:3 * W]
    lo = pm[:, 3 * W:3 * W + LORA_DECAY + LORA_A]
    gd = pm[:, 3 * W + LORA_DECAY + LORA_A:RWKV_SHIFT_W]
    lane = lax.broadcasted_iota(jnp.int32, lo.shape, 1)
    z = jnp.where(lane < LORA_DECAY, jnp.tanh(lo), lo).astype(BF16)
    w_l = _dot(z, wlora_ref[...].astype(BF16))
    a_l = _dot(z, alora_ref[...].astype(BF16))
    w = -_softplus(-(w0_ref[...] + w_l)) - 0.5
    decay = jnp.exp(-jnp.exp(w))
    a = _sigmoid(a0_ref[...] + a_l)
    g = _dot(_sigmoid(gd).astype(BF16), gup_ref[...].astype(BF16))
    bd = bd_ref[...]
    kk = k * kk_ref[...]
    ss = _dot2(kk * kk, bd)
    kn = kk * lax.rsqrt(jnp.maximum(ss, 1e-24))
    k_eff = k * (1.0 + (a - 1.0) * ka_ref[...])
    bvec = kn * a
    r_o[...] = r
    w_o[...] = decay
    k_o[...] = k_eff
    kn_o[...] = kn
    b_o[...] = bvec
    v_o[...] = v
    wr_o[...] = decay * r
    br_o[...] = _dot2(bvec * r, bd)
    kr_o[...] = _dot2(k_eff * r, bd)
    g_o[...] = g
    bonus_o[...] = _dot2(r * k_eff * rk_ref[...], bd) * v


def rwkv_prep(p_rw, sh0, B, T, prm):
    tm = _pick(T, (256, 128, 64, 32, 16, 8))
    nt = T // tm
    W = RWKV_W
    row_in = pl.BlockSpec((tm, RWKV_PAD_W), lambda b, t: (b * nt + t, 0))
    row_out = pl.BlockSpec((tm, W), lambda b, t: (b * nt + t, 0))
    full = lambda a: pl.BlockSpec(a.shape, lambda b, t: (0,) * a.ndim)
    sh0p = sh0.reshape(B, 1, RWKV_PAD_W)
    consts = [prm["mu"], prm["w0"], prm["a0"], prm["wlora"], prm["alora"], prm["gup"],
              prm["k_k"], prm["k_a"], prm["r_k"], prm["bd"]]
    outs = pl.pallas_call(
        _rwkv_prep_kernel,
        grid=(B, nt),
        in_specs=[row_in, pl.BlockSpec((1, 1, RWKV_PAD_W), lambda b, t: (b, 0, 0))] + [full(c) for c in consts],
        out_specs=[row_out] * 11,
        out_shape=[jax.ShapeDtypeStruct((B * T, W), F32)] * 11,
        scratch_shapes=[pltpu.VMEM((SUBLANES, RWKV_PAD_W), F32)],
        compiler_params=_cparams("arbitrary", "arbitrary"),
        name="rwkv_prep",
    )(p_rw, sh0p, *consts)
    return outs


PAIR_GROUP = 4
RWKV_SCAN_BATCH = 2


def _rwkv_scan_kernel(r_ref, w_ref, k_ref, kn_ref, b_ref, v_ref, wr_ref, br_ref, kr_ref, s0_ref, oseg_ref,
                      y_ref, sfin_ref, s_scr, *, nb, tc):
    ti = pl.program_id(1)

    @pl.when(ti == 0)
    def _():
        s_scr[...] = s0_ref[...]

    row = lax.broadcasted_iota(jnp.int32, (RWKV_HD, LANES), 0)
    lane = lax.broadcasted_iota(jnp.int32, (RWKV_HD, LANES), 1)
    diag = (lane & (RWKV_HD - 1)) == row
    oseg = oseg_ref[...]
    n_pairs = nb * PAIRS
    R = RWKV_HD

    def step8(t8, carry):
        base = pl.multiple_of(t8 * SUBLANES, SUBLANES)
        for g0 in range(0, n_pairs, PAIR_GROUP):
            ids = [(i // PAIRS, i % PAIRS) for i in range(g0, g0 + PAIR_GROUP)]
            n = len(ids)
            tile = lambda ref, b, p: ref[b, pl.ds(base, SUBLANES), p * LANES:(p + 1) * LANES]
            tiles = {nm: [tile(ref, b, p) for (b, p) in ids]
                     for nm, ref in (("kn", kn_ref), ("wr", wr_ref), ("v", v_ref), ("w", w_ref), ("b", b_ref),
                                     ("k", k_ref), ("br", br_ref), ("kr", kr_ref))}
            s_cur = [s_scr[b * PAIRS + p] for (b, p) in ids]
            y_rows = [[] for _ in ids]
            for j in range(SUBLANES):
                row = lambda nm, i: tiles[nm][i][j:j + 1, :]
                blocks = ([s_cur[i] * row("kn", i) for i in range(n)]
                          + [s_cur[i] * row("wr", i) for i in range(n)]
                          + [jnp.where(diag, row("v", i), 0.0) for i in range(n)])
                red = _dot2(jnp.concatenate(blocks, axis=0), oseg)
                for i in range(n):
                    skk = red[i * R:(i + 1) * R]
                    u = red[(n + i) * R:(n + i + 1) * R]
                    vb = red[(2 * n + i) * R:(2 * n + i + 1) * R]
                    s_cur[i] = s_cur[i] * row("w", i) - skk * row("b", i) + vb * row("k", i)
                    y_bc = u - skk * row("br", i) + vb * row("kr", i)
                    y_rows[i].append(jnp.sum(jnp.where(diag, y_bc, 0.0), axis=0, keepdims=True))
            for i, (b, p) in enumerate(ids):
                s_scr[b * PAIRS + p] = s_cur[i]
                y_ref[b, pl.ds(base, SUBLANES), p * LANES:(p + 1) * LANES] = jnp.concatenate(y_rows[i], axis=0)
        return carry

    lax.fori_loop(0, tc // SUBLANES, step8, 0)

    @pl.when(ti == pl.num_programs(1) - 1)
    def _():
        sfin_ref[...] = s_scr[...]


def rwkv_scan(seqs, s0, B, T):
    tc = _pick(T, (128, 64, 32, 16, 8))
    nb = RWKV_SCAN_BATCH
    assert B % nb == 0
    views = [s.reshape(B, T, RWKV_W) for s in seqs]
    s0p = s0.reshape(B, PAIRS, 2, RWKV_HD, RWKV_HD).transpose(0, 1, 3, 2, 4).reshape(B * PAIRS, RWKV_HD, LANES)
    li = jnp.arange(LANES)
    oseg = (li[:, None] // RWKV_HD == li[None, :] // RWKV_HD).astype(BF16)
    seq_spec = pl.BlockSpec((nb, tc, RWKV_W), lambda b, t: (b, t, 0))
    st_spec = pl.BlockSpec((nb * PAIRS, RWKV_HD, LANES), lambda b, t: (b, 0, 0))
    y, sfin = pl.pallas_call(
        functools.partial(_rwkv_scan_kernel, nb=nb, tc=tc),
        grid=(B // nb, T // tc),
        in_specs=[seq_spec] * 9 + [st_spec, pl.BlockSpec((LANES, LANES), lambda b, t: (0, 0))],
        out_specs=[seq_spec, st_spec],
        out_shape=[jax.ShapeDtypeStruct((B, T, RWKV_W), F32),
                   jax.ShapeDtypeStruct((B * PAIRS, RWKV_HD, LANES), F32)],
        scratch_shapes=[pltpu.VMEM((nb * PAIRS, RWKV_HD, LANES), F32)],
        compiler_params=_cparams("arbitrary", "arbitrary"),
        name="rwkv_scan",
    )(*views, s0p, oseg)
    sfin = sfin.reshape(B, PAIRS, RWKV_HD, 2, RWKV_HD).transpose(0, 1, 3, 2, 4).reshape(B, RWKV_HEADS, RWKV_HD, RWKV_HD)
    return y.reshape(B * T, RWKV_W), sfin


def _rwkv_post_kernel(y_ref, bonus_ref, g_ref, gng_ref, gnb_ref, bd_ref, o_ref):
    y = y_ref[...]
    bd = bd_ref[...]
    inv = 1.0 / RWKV_HD
    mean = _dot2(y, bd) * inv
    yc = y - mean
    var = _dot2(yc * yc, bd) * inv
    yn = yc * lax.rsqrt(var + GN_EPS) * gng_ref[...] + gnb_ref[...]
    o_ref[...] = (yn + bonus_ref[...]) * g_ref[...]


def rwkv_post(y, bonus, g, gn_g, gn_b, bd):
    m, W = y.shape
    tm = _pick(m, (256, 128, 64, 32, 16, 8))
    row = pl.BlockSpec((tm, W), lambda i: (i, 0))
    vec = pl.BlockSpec((1, W), lambda i: (0, 0))
    return pl.pallas_call(
        _rwkv_post_kernel,
        grid=(m // tm,),
        in_specs=[row, row, row, vec, vec, pl.BlockSpec((W, W), lambda i: (0, 0))],
        out_specs=row,
        out_shape=jax.ShapeDtypeStruct((m, W), F32),
        compiler_params=_cparams("parallel"),
        name="rwkv_post",
    )(y, bonus, g, gn_g.reshape(1, W), gn_b.reshape(1, W), bd)


def _gelu_tanh(x):
    return 0.5 * x * (1.0 + jnp.tanh(math.sqrt(2.0 / math.pi) * (x + 0.044715 * (x * x * x))))


def _neg_expm1(x):
    series = -x * (1.0 + x * (1.0 / 2.0) * (1.0 + x * (1.0 / 3.0) * (1.0 + x * (1.0 / 4.0) * (1.0 + x * (1.0 / 5.0)
             * (1.0 + x * (1.0 / 6.0))))))
    return jnp.where(x > -0.05, series, 1.0 - jnp.exp(x))


def _lru_kernel(px_ref, pg_ref, c0_ref, h0_ref, cw_ref, cb_ref, wa_ref, ba_ref, wx_ref, bx_ref, lam_ref,
                y_ref, hlast_ref, carry, hcar):
    ti = pl.program_id(1)

    @pl.when(ti == 0)
    def _():
        carry[...] = c0_ref[0]
        hcar[...] = h0_ref[0]

    x = px_ref[...]
    tm = x.shape[0]
    cw = cw_ref[...]
    xc = cb_ref[...] + cw[CONV_W - 1:CONV_W, :] * x
    for j in range(1, CONV_W):
        xc = xc + cw[CONV_W - 1 - j:CONV_W - j, :] * _shifted(carry, x, j)
    carry[...] = x[tm - SUBLANES:, :]
    xcb = xc.astype(BF16)
    r = _sigmoid(_dot(xcb, wa_ref[...].astype(BF16)) + ba_ref[...])
    i = _sigmoid(_dot(xcb, wx_ref[...].astype(BF16)) + bx_ref[...])
    log_a = (-LRU_C * _softplus(-lam_ref[...])) * r
    a = jnp.exp(log_a)
    b = jnp.sqrt(_neg_expm1(2.0 * log_a)) * (i * xc)
    rows = lax.broadcasted_iota(jnp.int32, a.shape, 0)
    d = 1
    while d < tm:
        a_sh = pltpu.roll(a, d, axis=0)
        b_sh = pltpu.roll(b, d, axis=0)
        keep = rows >= d
        b = jnp.where(keep, a * b_sh + b, b)
        a = jnp.where(keep, a * a_sh, a)
        d *= 2
    h = a * hcar[...] + b
    hcar[...] = h[tm - 1:tm, :]
    hlast_ref[0] = h[tm - 1:tm, :]
    y_ref[...] = h * _gelu_tanh(pg_ref[...])


def lru_mix(p_l, c0, h0, B, T, prm):
    W = LRU_W
    tm = _pick(T, (256, 128, 64, 32, 16, 8))
    nt = T // tm
    c0p = jnp.concatenate([jnp.zeros((B, SUBLANES - (CONV_W - 1), W), F32), c0], axis=1)
    h0p = h0.reshape(B, 1, W)
    full = lambda a: pl.BlockSpec(a.shape, lambda b, t: (0,) * a.ndim)
    consts = [prm["conv_w"], prm["conv_b"], prm["wa"], prm["ba"], prm["wx"], prm["bx"], prm["lam"]]
    y, hlast = pl.pallas_call(
        _lru_kernel,
        grid=(B, nt),
        in_specs=[pl.BlockSpec((tm, W), lambda b, t: (b * nt + t, 0)),
                  pl.BlockSpec((tm, W), lambda b, t: (b * nt + t, 1)),
                  pl.BlockSpec((1, SUBLANES, W), lambda b, t: (b, 0, 0)),
                  pl.BlockSpec((1, 1, W), lambda b, t: (b, 0, 0))] + [full(c) for c in consts],
        out_specs=[pl.BlockSpec((tm, W), lambda b, t: (b * nt + t, 0)),
                   pl.BlockSpec((1, 1, W), lambda b, t: (b, 0, 0))],
        out_shape=[jax.ShapeDtypeStruct((B * T, W), F32), jax.ShapeDtypeStruct((B, 1, W), F32)],
        scratch_shapes=[pltpu.VMEM((SUBLANES, W), F32), pltpu.VMEM((1, W), F32)],
        compiler_params=_cparams("arbitrary", "arbitrary"),
        name="lru_mix",
    )(p_l, p_l, c0p, h0p, *consts)
    return y, hlast.reshape(B, W)


def _block_diag(w):
    n, d, e = w.shape
    eye = jnp.eye(n, dtype=w.dtype)
    return (eye[:, None, :, None] * w[:, :, None, :]).reshape(n * d, n * e)


def rec_params(li, rec_w_in, rec_mu, rwkv_w0, rwkv_w_up, rwkv_a0, rwkv_a_up, rwkv_g_up, rwkv_k_k, rwkv_k_a,
               rwkv_r_k, rwkv_gn_g, rwkv_gn_b, lru_conv_w, lru_conv_b, lru_wa, lru_ba, lru_wx, lru_bx, lru_lambda,
               rec_w_out):
    W = RWKV_W
    pad = RWKV_PAD_W - RWKV_SHIFT_W
    hi = jnp.arange(W) // RWKV_HD
    zl = jnp.zeros((LORA_DECAY, W), F32)
    return dict(
        w_rw=jnp.pad(rec_w_in[li][:, :RWKV_SHIFT_W], ((0, 0), (0, pad))),
        w_l=rec_w_in[li][:, RWKV_SHIFT_W:],
        mu=jnp.pad(rec_mu[li], (0, pad)).reshape(1, RWKV_PAD_W),
        w0=rwkv_w0[li].reshape(1, W), a0=rwkv_a0[li].reshape(1, W),
        wlora=jnp.concatenate([rwkv_w_up[li], zl], axis=0),
        alora=jnp.concatenate([zl, rwkv_a_up[li]], axis=0),
        gup=rwkv_g_up[li],
        k_k=rwkv_k_k[li].reshape(1, W), k_a=rwkv_k_a[li].reshape(1, W), r_k=rwkv_r_k[li].reshape(1, W),
        bd=(hi[:, None] == hi[None, :]).astype(BF16),
        gn_g=rwkv_gn_g[li], gn_b=rwkv_gn_b[li],
        conv_w=lru_conv_w[li], conv_b=lru_conv_b[li].reshape(1, LRU_W),
        wa=_block_diag(lru_wa[li]), ba=lru_ba[li].reshape(1, LRU_W),
        wx=_block_diag(lru_wx[li]), bx=lru_bx[li].reshape(1, LRU_W),
        lam=lru_lambda[li].reshape(1, LRU_W),
        w_out=rec_w_out[li],
    )


def rec_mixer(x2d, B, T, S0, sh0, h0, c0, prm):
    p_rw = mm(x2d, prm["w_rw"])
    p_l = mm(x2d, prm["w_l"])
    sh0p = jnp.pad(sh0, ((0, 0), (0, RWKV_PAD_W - RWKV_SHIFT_W)))
    r, w, k, kn, bv, v, wr, br, kr, g, bonus = rwkv_prep(p_rw, sh0p, B, T, prm)
    y_raw, s_fin = rwkv_scan([r, w, k, kn, bv, v, wr, br, kr], S0, B, T)
    y_rw = rwkv_post(y_raw, bonus, g, prm["gn_g"], prm["gn_b"], prm["bd"])
    y_lru, h_last = lru_mix(p_l, c0, h0, B, T, prm)
    y = mm_cat2(y_rw, y_lru, prm["w_out"])
    sh = p_rw.reshape(B, T, RWKV_PAD_W)[:, -1, :RWKV_SHIFT_W]
    px = p_l.reshape(B, T, 2 * LRU_W)[:, :, :LRU_W]
    cbuf = jnp.concatenate([c0, px], axis=1)[:, -(CONV_W - 1):] if T < CONV_W - 1 else px[:, -(CONV_W - 1):]
    return y, s_fin, sh, h_last, cbuf


def _dot_nt(a, b):
    return lax.dot_general(a, b, (((1,), (1,)), ((), ())), preferred_element_type=F32)


def _compress_kernel(slab_ref, w_ref, o_ref, *, n_pieces):
    acc = None
    for l in range(CMP_STRIDE):
        x = slab_ref[pl.ds(l, n_pieces, stride=CMP_STRIDE), :].astype(BF16)
        d = _dot(x, w_ref[l].astype(BF16))
        acc = d if acc is None else acc + d
    out = acc[:, :HEAD_DIM] + pltpu.roll(acc[:, HEAD_DIM:], n_pieces - 1, axis=0)
    o_ref[0:n_pieces, :] = out
    pad = o_ref.shape[0] - n_pieces
    if pad:
        o_ref[n_pieces:, :] = jnp.zeros((pad, HEAD_DIM), F32)


def nsa_compress(kv2d, col0, B, t_kv, wcat, ncp):
    n_pieces = t_kv // CMP_STRIDE
    return pl.pallas_call(
        functools.partial(_compress_kernel, n_pieces=n_pieces),
        grid=(B, 2, N_KV),
        in_specs=[pl.BlockSpec((t_kv, HEAD_DIM), lambda b, c, g: (b, col0 + c * N_KV + g)),
                  pl.BlockSpec((None, CMP_STRIDE, HEAD_DIM, 2 * HEAD_DIM), lambda b, c, g: (c, 0, 0, 0))],
        out_specs=pl.BlockSpec((None, None, None, ncp, HEAD_DIM), lambda b, c, g: (b, c, g, 0, 0)),
        out_shape=jax.ShapeDtypeStruct((B, 2, N_KV, ncp, HEAD_DIM), F32),
        compiler_params=_cparams("parallel", "parallel", "parallel"),
        name="nsa_compress",
    )(kv2d, wcat)


def _online_softmax_step(q_bf, k_bf, v_bf, bias, mask, m, l, acc):
    s = _dot_nt(q_bf, k_bf) * SCALE - bias
    s = jnp.where(mask, s, NEG)
    m_new = jnp.maximum(m, jnp.max(s, axis=-1, keepdims=True))
    alpha = jnp.exp(m - m_new)
    p = jnp.where(mask, jnp.exp(s - m_new), 0.0)
    l = alpha * l + jnp.sum(p, axis=-1, keepdims=True)
    acc = alpha * acc + _dot(p.astype(BF16), v_bf)
    return m_new, l, acc


def _nsa_kernel(q_ref, kc_ref, vc_ref, ks_ref, vs_ref, kw_ref, vw_ref, gate_ref, gb_ref, ov_ref, o_ref, *,
                tq, q_off, nc_real, ns_real, tk, n_kv_static, w_off, w_tiles_static):
    g = pl.program_id(1)
    qi = pl.program_id(2)
    q0 = qi * tq
    ncp = kc_ref.shape[0]
    nsp = ov_ref.shape[1]
    qpos_col = q_off + q0 + lax.broadcasted_iota(jnp.int32, (tq, 1), 0)
    q_all = q_ref[...]
    q_bf = [q_all[:, r * HEAD_DIM:(r + 1) * HEAD_DIM].astype(BF16) for r in range(GQA_R)]
    head_f = (g * GQA_R).astype(F32)
    slopes = [jnp.exp2(-0.5 * (jnp.full((1, 1), 1.0 + r, F32) + head_f)) for r in range(GQA_R)]

    n_idx = lax.broadcasted_iota(jnp.int32, (tq, ncp), 1)
    dist_c = qpos_col - (n_idx * CMP_STRIDE + (CMP_BLOCK - 1))
    ok_c = (dist_c >= 0) & (n_idx < nc_real)
    dist_cf = dist_c.astype(F32)
    kc_bf = kc_ref[...].astype(BF16)
    vc_bf = vc_ref[...].astype(BF16)
    o_c = []
    p_sum = None
    for r in range(GQA_R):
        s = _dot_nt(q_bf[r], kc_bf) * SCALE - slopes[r] * dist_cf
        s = jnp.where(ok_c, s, NEG)
        m = jnp.max(s, axis=-1, keepdims=True)
        e = jnp.where(ok_c, jnp.exp(s - m), 0.0)
        p = e / jnp.maximum(jnp.sum(e, axis=-1, keepdims=True), 1e-30)
        o_c.append(_dot(p.astype(BF16), vc_bf))
        p_sum = p if p_sum is None else p_sum + p
    imp = _dot2(p_sum, ov_ref[...])
    s_idx = lax.broadcasted_iota(jnp.int32, (tq, nsp), 1)
    cur = lax.shift_right_logical(qpos_col, int(math.log2(SLC_BLOCK)))
    forced = (s_idx == 0) | (s_idx == cur) | (s_idx == cur - 1)
    causal_blk = (s_idx * SLC_BLOCK <= qpos_col) & (s_idx < ns_real)
    score = jnp.where(causal_blk, imp + jnp.where(forced, BIG, 0.0), NEG)
    rank = jnp.zeros((tq, nsp), jnp.int32)
    for j in range(ns_real):
        col = score[:, j:j + 1]
        ahead = (col > score) | ((col == score) & (s_idx > j))
        rank = rank + jnp.where(ahead, 1, 0)
    sel = (rank < min(TOP_N, ns_real)) & (score > 0.5 * NEG)
    sel_bf = jnp.where(sel, 1.0, 0.0).astype(BF16)

    def init():
        return tuple((jnp.full((tq, 1), NEG, F32), jnp.zeros((tq, 1), F32), jnp.zeros((tq, HEAD_DIM), F32))
                     for _ in range(GQA_R))

    blk_per_tile = tk // SLC_BLOCK
    ex_row = lax.broadcasted_iota(jnp.int32, (nsp, tk), 0)
    ex_col = lax.shift_right_logical(lax.broadcasted_iota(jnp.int32, (nsp, tk), 1), int(math.log2(SLC_BLOCK)))
    key_i = lax.broadcasted_iota(jnp.int32, (tq, tk), 1)

    def slc_body(kt, carry):
        base = pl.multiple_of(kt * tk, tk)
        expand = jnp.where(ex_row == ex_col + kt * blk_per_tile, 1.0, 0.0).astype(BF16)
        selexp = _dot(sel_bf, expand)
        dist = qpos_col - (key_i + base)
        mask = (dist >= 0) & (selexp > 0.5)
        distf = dist.astype(F32)
        k_bf = ks_ref[pl.ds(base, tk), :].astype(BF16)
        v_bf = vs_ref[pl.ds(base, tk), :].astype(BF16)
        return tuple(_online_softmax_step(q_bf[r], k_bf, v_bf, slopes[r] * distf, mask, *carry[r])
                     for r in range(GQA_R))

    if n_kv_static is None:
        n_kv = (q0 + tq + tk - 1) // tk
    else:
        n_kv = n_kv_static
    st_s = lax.fori_loop(0, n_kv, slc_body, init())

    wt = LANES
    wkey_i = lax.broadcasted_iota(jnp.int32, (tq, wt), 1)

    def win_body(t, carry):
        base = pl.multiple_of(t * wt, wt)
        dist = qpos_col - (w_off + base + wkey_i)
        mask = (dist >= 0) & (dist < WINDOW)
        distf = dist.astype(F32)
        k_bf = kw_ref[pl.ds(base, wt), :].astype(BF16)
        v_bf = vw_ref[pl.ds(base, wt), :].astype(BF16)
        return tuple(_online_softmax_step(q_bf[r], k_bf, v_bf, slopes[r] * distf, mask, *carry[r])
                     for r in range(GQA_R))

    if w_tiles_static is None:
        w_hi = (q0 + tq + wt - 1) // wt
        w_lo = jnp.maximum(q0 // wt - WINDOW // wt, 0)
    else:
        w_lo, w_hi = 0, w_tiles_static
    st_w = lax.fori_loop(w_lo, w_hi, win_body, init())

    gates = _sigmoid(gate_ref[...] + gb_ref[...])
    lane = lax.broadcasted_iota(jnp.int32, gates.shape, 1)
    for r in range(GQA_R):
        head = g * GQA_R + r
        gcol = lambda br: jnp.sum(jnp.where(lane == br * N_HEADS + head, gates, 0.0), axis=-1, keepdims=True)
        o_s = st_s[r][2] / jnp.maximum(st_s[r][1], 1e-30)
        o_w = st_w[r][2] / jnp.maximum(st_w[r][1], 1e-30)
        o_ref[:, r * HEAD_DIM:(r + 1) * HEAD_DIM] = gcol(0) * o_c[r] + gcol(1) * o_s + gcol(2) * o_w


def nsa_attention(q2d, kcvc, kv2d, kv_col0, win2d, win_col0, gates, gate_b, overlap, *, B, t_q, t_kv, t_win, tq,
                  q_off, nc_real, ns_real, tk, n_kv_static, w_off, w_tiles_static):
    nq = t_q // tq
    ncp = kcvc.shape[3]
    nsp = overlap.shape[1]
    qw = GQA_R * HEAD_DIM
    slab = lambda arr_t, col: pl.BlockSpec((arr_t, HEAD_DIM), col)
    kern = functools.partial(_nsa_kernel, tq=tq, q_off=q_off, nc_real=nc_real, ns_real=ns_real, tk=tk,
                             n_kv_static=n_kv_static, w_off=w_off, w_tiles_static=w_tiles_static)
    return pl.pallas_call(
        kern,
        grid=(B, N_KV, nq),
        in_specs=[pl.BlockSpec((tq, qw), lambda b, g, i: (b * nq + i, g)),
                  pl.BlockSpec((None, None, None, ncp, HEAD_DIM), lambda b, g, i: (b, 0, g, 0, 0)),
                  pl.BlockSpec((None, None, None, ncp, HEAD_DIM), lambda b, g, i: (b, 1, g, 0, 0)),
                  slab(t_kv, lambda b, g, i: (b, kv_col0 + 2 * N_KV + g)),
                  slab(t_kv, lambda b, g, i: (b, kv_col0 + 3 * N_KV + g)),
                  slab(t_win, lambda b, g, i: (b, win_col0 + g)),
                  slab(t_win, lambda b, g, i: (b, win_col0 + N_KV + g)),
                  pl.BlockSpec((tq, LANES), lambda b, g, i: (b * nq + i, 0)),
                  pl.BlockSpec((1, LANES), lambda b, g, i: (0, 0)),
                  pl.BlockSpec((ncp, nsp), lambda b, g, i: (0, 0))],
        out_specs=pl.BlockSpec((tq, qw), lambda b, g, i: (b * nq + i, g)),
        out_shape=jax.ShapeDtypeStruct((B * t_q, N_HEADS * HEAD_DIM), F32),
        compiler_params=_cparams("parallel", "parallel", "arbitrary"),
        name="nsa_attention",
    )(q2d, kcvc, kcvc, kv2d, kv2d, win2d, win2d, gates, gate_b, overlap)


def _overlap_matrix(ncp, nsp, nc_real, ns_real):
    n = jnp.arange(ncp)[:, None]
    s = jnp.arange(nsp)[None, :]
    c0 = n * CMP_STRIDE
    s0 = s * SLC_BLOCK
    ov = (c0 < s0 + SLC_BLOCK) & (c0 + CMP_BLOCK > s0) & (n < nc_real) & (s < ns_real)
    return ov.astype(BF16)


def _gather_pages_kernel(pt_ref, page_ref, tail_ref, o_ref):
    j = pl.program_id(1)
    n_pages = pl.num_programs(1) - 1

    @pl.when(j < n_pages)
    def _():
        o_ref[...] = page_ref[...]

    @pl.when(j == n_pages)
    def _():
        o_ref[...] = tail_ref[...]


def gather_pages(cache5, page_table, tail, li):
    B, n_pages = page_table.shape
    page, row_w = cache5.shape[1], cache5.shape[4]
    grid_spec = pltpu.PrefetchScalarGridSpec(
        num_scalar_prefetch=1,
        grid=(B, n_pages + 1),
        in_specs=[pl.BlockSpec((None, page, None, None, row_w),
                               lambda b, j, pt: (pt[b, jnp.minimum(j, n_pages - 1)], 0, li, 0, 0)),
                  pl.BlockSpec((None, page, row_w), lambda b, j, pt: (b, 0, 0))],
        out_specs=pl.BlockSpec((None, page, row_w), lambda b, j, pt: (b, j, 0)),
    )
    return pl.pallas_call(
        _gather_pages_kernel,
        grid_spec=grid_spec,
        out_shape=jax.ShapeDtypeStruct((B, (n_pages + 1) * page, row_w), F32),
        compiler_params=_cparams("parallel", "arbitrary"),
        name="gather_pages",
    )(page_table, cache5, tail)


def attn_params(li, attn_w_in, attn_gate_b, w_cmp_k, w_cmp_v, attn_w_out):
    qkv_w = N_HEADS * HEAD_DIM + 6 * N_KV * HEAD_DIM
    ng = 3 * N_HEADS
    half = lambda w: jnp.concatenate([w[:CMP_STRIDE], w[CMP_STRIDE:]], axis=-1)
    return dict(
        w_main=attn_w_in[li][:, :qkv_w],
        w_gate=jnp.pad(attn_w_in[li][:, qkv_w:], ((0, 0), (0, LANES - ng))),
        gate_b=jnp.pad(attn_gate_b[li], (0, LANES - ng)).reshape(1, LANES),
        wcat=jnp.stack([half(w_cmp_k[li]), half(w_cmp_v[li])]),
        w_out=attn_w_out[li],
    )


def attn_prompt(x2d, B, T, prm):
    p = mm(x2d, prm["w_main"])
    gates = mm(x2d, prm["w_gate"])
    q_blocks = N_HEADS
    n_pieces = T // CMP_STRIDE
    nc_real = n_pieces - CMP_BLOCK // CMP_STRIDE + 1
    ns_real = T // SLC_BLOCK
    ncp = -(-n_pieces // LANES) * LANES
    nsp = -(-ns_real // LANES) * LANES
    kcvc = nsa_compress(p, q_blocks, B, T, prm["wcat"], ncp)
    ov = _overlap_matrix(ncp, nsp, nc_real, ns_real)
    tq = min(128, T)
    o = nsa_attention(p, kcvc, p, q_blocks, p, q_blocks + 4 * N_KV, gates, prm["gate_b"], ov,
                      B=B, t_q=T, t_kv=T, t_win=T, tq=tq, q_off=0, nc_real=nc_real, ns_real=ns_real,
                      tk=min(512, T), n_kv_static=None, w_off=0, w_tiles_static=None)
    y = mm(o, prm["w_out"])
    kv = p.reshape(B, T, -1)[:, :, N_HEADS * HEAD_DIM:].reshape(B, T, 6, N_KV, HEAD_DIM)
    w_buf = min(WINDOW, T)
    win = jnp.pad(kv[:, :, 4:], ((0, 0), (WINDOW, 0), (0, 0), (0, 0), (0, 0)))[:, -w_buf:]
    return y, kv[:, :, :4], win


def attn_sample(x2d, B, T, prm, cache_nsa_kv, cache_win, page_table, li):
    n_pool, page = cache_nsa_kv.shape[:2]
    n_attn = cache_nsa_kv.shape[2]
    n_pages = page_table.shape[1]
    past_len = n_pages * page
    row_w = 4 * N_KV * HEAD_DIM
    p = mm(x2d, prm["w_main"])
    gates = mm(x2d, prm["w_gate"])
    kv_new = p[:, N_HEADS * HEAD_DIM:].reshape(B, T, 6, N_KV, HEAD_DIM)
    tail = jnp.pad(kv_new[:, :, :4].reshape(B, T, row_w), ((0, 0), (0, page - T), (0, 0)))
    full = gather_pages(cache_nsa_kv.reshape(n_pool, page, n_attn, 1, row_w), page_table, tail, li)
    t_kv = (n_pages + 1) * page
    full2d = full.reshape(B * t_kv, row_w)
    L = past_len + T
    l_pad = -(-L // SLC_BLOCK) * SLC_BLOCK
    nc_real = l_pad // CMP_STRIDE - CMP_BLOCK // CMP_STRIDE + 1
    ns_real = l_pad // SLC_BLOCK
    n_pieces = t_kv // CMP_STRIDE
    ncp = -(-n_pieces // LANES) * LANES
    nsp = -(-ns_real // LANES) * LANES
    kcvc = nsa_compress(full2d, 0, B, t_kv, prm["wcat"], ncp)
    ov = _overlap_matrix(ncp, nsp, nc_real, ns_real)
    w_buf = cache_win.shape[1]
    win = jnp.concatenate([cache_win, kv_new[:, :, 4:]], axis=1)
    t_win = -(-(w_buf + T) // LANES) * LANES
    win2d = jnp.pad(win.reshape(B, w_buf + T, 2 * N_KV * HEAD_DIM), ((0, 0), (0, t_win - (w_buf + T)), (0, 0)))
    win2d = win2d.reshape(B * t_win, 2 * N_KV * HEAD_DIM)
    tk = 3 * page
    assert t_kv % tk == 0
    o = nsa_attention(p, kcvc, full2d, 0, win2d, 0, gates, prm["gate_b"], ov,
                      B=B, t_q=T, t_kv=t_kv, t_win=t_win, tq=T, q_off=past_len, nc_real=nc_real, ns_real=ns_real,
                      tk=tk, n_kv_static=t_kv // tk, w_off=past_len - w_buf, w_tiles_static=t_win // LANES)
    y = mm(o, prm["w_out"])
    return y, kv_new[:, :, :4], win[:, -w_buf:]


MOE_TM = 256


def _router_kernel(x_ref, w_ref, o_ref):
    xh, xl = _split2(x_ref[...])
    wh, wl = _split2(w_ref[...])
    o_ref[...] = _dot(xh, wh) + _dot(xl, wh) + _dot(xh, wl)


def router_logits(x, w_pad):
    m, d = x.shape
    tm = _pick(m, (512, 256, 192, 128, 64, 32, 16, 8))
    return pl.pallas_call(
        _router_kernel,
        grid=(m // tm,),
        in_specs=[pl.BlockSpec((tm, d), lambda i: (i, 0)), pl.BlockSpec((d, LANES), lambda i: (0, 0))],
        out_specs=pl.BlockSpec((tm, LANES), lambda i: (i, 0)),
        out_shape=jax.ShapeDtypeStruct((m, LANES), F32),
        compiler_params=_cparams("parallel"),
        name="router_logits",
    )(x, w_pad)


def _expert_kernel(te_ref, x_ref, rw_ref, wg_ref, wu_ref, wd_ref, o_ref):
    xb = x_ref[...].astype(BF16)
    hg = _dot(xb, wg_ref[...])
    hu = _dot(xb, wu_ref[...])
    h = (hg * _sigmoid(hg)) * hu * rw_ref[...]
    o_ref[...] = _dot(h.astype(BF16), wd_ref[...])


def expert_ffn(x_sorted, row_w, tile_expert, wg, wu, wd):
    a_pad, d = x_sorted.shape
    f = wg.shape[2]
    n_tiles = a_pad // MOE_TM
    grid_spec = pltpu.PrefetchScalarGridSpec(
        num_scalar_prefetch=1,
        grid=(n_tiles,),
        in_specs=[pl.BlockSpec((MOE_TM, d), lambda t, te: (t, 0)),
                  pl.BlockSpec((MOE_TM, 1), lambda t, te: (t, 0)),
                  pl.BlockSpec((None, d, f), lambda t, te: (te[t], 0, 0)),
                  pl.BlockSpec((None, d, f), lambda t, te: (te[t], 0, 0)),
                  pl.BlockSpec((None, f, d), lambda t, te: (te[t], 0, 0))],
        out_specs=pl.BlockSpec((MOE_TM, d), lambda t, te: (t, 0)),
    )
    return pl.pallas_call(
        _expert_kernel,
        grid_spec=grid_spec,
        out_shape=jax.ShapeDtypeStruct((a_pad, d), F32),
        compiler_params=_cparams("arbitrary"),
        name="expert_ffn",
    )(tile_expert, x_sorted, row_w, wg, wu, wd)


def moe(x, router_w, r_bias, w_g, w_u, w_d):
    n, d = x.shape
    logits = router_logits(x, router_w)[:, :N_EXPERTS]
    aff = jax.nn.sigmoid(logits)
    sel = (aff + r_bias).reshape(n, N_GROUPS, EXPERTS_PER_GROUP)
    grp_score = lax.top_k(sel, TOP_K)[0].sum(-1)
    g_star = jnp.argmax(grp_score, axis=-1)
    in_grp = jnp.take_along_axis(sel, g_star[:, None, None], axis=1)[:, 0]
    _, loc = lax.top_k(in_grp, TOP_K)
    e_idx = g_star[:, None] * EXPERTS_PER_GROUP + loc
    w_sel = jnp.take_along_axis(aff, e_idx, axis=-1)
    w_sel = w_sel / jnp.sum(w_sel, axis=-1, keepdims=True)
    n_asg = n * TOP_K
    e_flat = e_idx.reshape(n_asg).astype(jnp.int32)
    order = jnp.argsort(e_flat, stable=True)
    counts = jnp.zeros((N_EXPERTS,), jnp.int32).at[e_flat].add(1)
    gsz = (counts + MOE_TM - 1) // MOE_TM * MOE_TM
    c_end = jnp.cumsum(counts)
    p_end = jnp.cumsum(gsz)
    e_sorted = e_flat[order]
    dest = (p_end - gsz)[e_sorted] + (jnp.arange(n_asg, dtype=jnp.int32) - (c_end - counts)[e_sorted])
    a_pad = -(-(n_asg + N_EXPERTS * (MOE_TM - 1)) // MOE_TM) * MOE_TM
    src_tok = jnp.zeros((a_pad,), jnp.int32).at[dest].set(order // TOP_K)
    row_w = jnp.zeros((a_pad,), F32).at[dest].set(w_sel.reshape(n_asg)[order])
    pos = jnp.zeros((n_asg,), jnp.int32).at[order].set(dest)
    tile_start = jnp.arange(a_pad // MOE_TM, dtype=jnp.int32) * MOE_TM
    tile_expert = jnp.minimum(jnp.searchsorted(p_end, tile_start, side="right"), N_EXPERTS - 1).astype(jnp.int32)
    ys = expert_ffn(x[src_tok], row_w.reshape(a_pad, 1), tile_expert,
                    w_g.astype(BF16), w_u.astype(BF16), w_d.astype(BF16))
    pos = pos.reshape(n, TOP_K)
    return ys[pos[:, 0]] + ys[pos[:, 1]]


def kernel(x_prompt, x_sample, cache_nsa_kv, cache_win_kv, state_rwkv, state_rwkv_shift, state_lru_h, state_lru_conv,
           page_table, ln1_g, ln1_b, ln2_g, ln2_b, rec_w_in, rec_mu, rwkv_w0, rwkv_w_up, rwkv_a0, rwkv_a_up, rwkv_g_up,
           rwkv_k_k, rwkv_k_a, rwkv_r_k, rwkv_gn_g, rwkv_gn_b, lru_conv_w, lru_conv_b, lru_wa, lru_ba, lru_wx, lru_bx,
           lru_lambda, rec_w_out, attn_w_in, attn_gate_b, w_cmp_k, w_cmp_v, attn_w_out, moe_w_router, moe_router_bias,
           moe_w_gate, moe_w_up, moe_w_down):
    Bp, Tp, D = x_prompt.shape
    Bs, Ts, _ = x_sample.shape
    n_p, n_s = Bp * Tp, Bs * Ts
    rec_args = (rec_w_in, rec_mu, rwkv_w0, rwkv_w_up, rwkv_a0, rwkv_a_up, rwkv_g_up, rwkv_k_k, rwkv_k_a,
                rwkv_r_k, rwkv_gn_g, rwkv_gn_b, lru_conv_w, lru_conv_b, lru_wa, lru_ba, lru_wx, lru_bx, lru_lambda,
                rec_w_out)
    x = jnp.concatenate([x_prompt.reshape(n_p, D), x_sample.reshape(n_s, D)], axis=0)
    router_w = jnp.pad(moe_w_router, ((0, 0), (0, LANES - N_EXPERTS)))
    nsa_p, nsa_s, win_p, win_s = [], [], [], []
    rs_p, rs_s, rsh_p, rsh_s, lh_p, lh_s, lc_p, lc_s = [], [], [], [], [], [], [], []
    for layer in range(DEPTH):
        li = layer // 2
        xp, xs = x[:n_p], x[n_p:]
        if layer % 2 == 0:
            prm = rec_params(li, *rec_args)
            yp, S, sh, h, cb = rec_mixer(xp, Bp, Tp, jnp.zeros((Bp, RWKV_HEADS, RWKV_HD, RWKV_HD), F32),
                                         jnp.zeros((Bp, RWKV_SHIFT_W), F32), jnp.zeros((Bp, LRU_W), F32),
                                         jnp.zeros((Bp, CONV_W - 1, LRU_W), F32), prm)
            ys, S2, sh2, h2, cb2 = rec_mixer(xs, Bs, Ts, state_rwkv[:, li], state_rwkv_shift[:, li],
                                             state_lru_h[:, li], state_lru_conv[:, li], prm)
            rs_p.append(S); rs_s.append(S2); rsh_p.append(sh); rsh_s.append(sh2)
            lh_p.append(h); lh_s.append(h2); lc_p.append(cb); lc_s.append(cb2)
        else:
            prm = attn_params(li, attn_w_in, attn_gate_b, w_cmp_k, w_cmp_v, attn_w_out)
            yp, rows, wb = attn_prompt(xp, Bp, Tp, prm)
            ys, rows2, wb2 = attn_sample(xs, Bs, Ts, prm, cache_nsa_kv, cache_win_kv[:, :, li], page_table, li)
            nsa_p.append(rows); nsa_s.append(rows2); win_p.append(wb); win_s.append(wb2)
        x = ln_res(x, jnp.concatenate([yp, ys], axis=0), ln1_g[layer], ln1_b[layer])
        y_moe = moe(x, router_w, moe_router_bias, moe_w_gate[layer], moe_w_up[layer], moe_w_down[layer])
        x = ln_res(x, y_moe, ln2_g[layer], ln2_b[layer])
    return (x[:n_p].reshape(Bp, Tp, D), x[n_p:].reshape(Bs, Ts, D),
            jnp.stack(nsa_p, axis=2), jnp.stack(nsa_s, axis=2), jnp.stack(win_p, axis=2), jnp.stack(win_s, axis=2),
            jnp.stack(rs_p, axis=1), jnp.stack(rs_s, axis=1), jnp.stack(rsh_p, axis=1), jnp.stack(rsh_s, axis=1),
            jnp.stack(lh_p, axis=1), jnp.stack(lh_s, axis=1), jnp.stack(lc_p, axis=1), jnp.stack(lc_s, axis=1))
```

```python
import functools
import math

import jax
import jax.numpy as jnp
from jax import lax
from jax.experimental import pallas as pl
from jax.experimental.pallas import tpu as pltpu

F32 = jnp.float32
BF16 = jnp.bfloat16

D_MODEL = 2048
DEPTH = 4
RWKV_HEADS = 16
RWKV_HD = 64
RWKV_W = RWKV_HEADS * RWKV_HD
LORA_DECAY = 64
LORA_A = 64
LORA_GATE = 128
RWKV_SHIFT_W = 3 * RWKV_W + LORA_DECAY + LORA_A + LORA_GATE
GN_EPS = 64e-5
LRU_W = D_MODEL - RWKV_W
LRU_BLOCKS = 16
CONV_W = 4
LRU_C = 8.0
N_HEADS = 16
HEAD_DIM = 128
N_KV = 4
GQA_R = N_HEADS // N_KV
CMP_BLOCK = 32
CMP_STRIDE = 16
SLC_BLOCK = 64
TOP_N = 16
WINDOW = 512
SCALE = HEAD_DIM ** -0.5
N_EXPERTS = 16
N_GROUPS = 4
EXPERTS_PER_GROUP = N_EXPERTS // N_GROUPS
TOP_K = 2
D_EXPERT = 1024
LN_EPS = 1e-5
DN_ALPHA = (2 * DEPTH) ** 0.25
NEG = -1e30
BIG = 1e4

LANES = 128
SUBLANES = 8
VMEM_LIMIT_BYTES = 48 * 1024 * 1024

RWKV_PAD_W = 3584
PAIRS = RWKV_HEADS // 2


def _cparams(*sem):
    return pltpu.CompilerParams(dimension_semantics=sem, vmem_limit_bytes=VMEM_LIMIT_BYTES)


def _split2(x):
    hi = x.astype(BF16)
    lo = (x - hi.astype(F32)).astype(BF16)
    return hi, lo


def _dot(a, b):
    return jnp.dot(a, b, preferred_element_type=F32)


def _dot2(x, w_bf):
    hi, lo = _split2(x)
    return _dot(hi, w_bf) + _dot(lo, w_bf)


def _sigmoid(x):
    return 1.0 / (1.0 + jnp.exp(-x))


def _softplus(x):
    return jnp.maximum(x, 0.0) + jnp.log(1.0 + jnp.exp(-jnp.abs(x)))


def _mm_kernel(x_ref, w_ref, o_ref):
    o_ref[...] = _dot(x_ref[...].astype(BF16), w_ref[...].astype(BF16))


def _pick(n, cands):
    for c in cands:
        if n % c == 0:
            return c
    return n


def mm(x, w):
    m, k = x.shape
    n = w.shape[1]
    tm = _pick(m, (512, 256, 128, 64, 32, 16, 8))
    tn = _pick(n, (1024, 768, 512, 256, 128))
    return pl.pallas_call(
        _mm_kernel,
        grid=(n // tn, m // tm),
        in_specs=[pl.BlockSpec((tm, k), lambda j, i: (i, 0)),
                  pl.BlockSpec((k, tn), lambda j, i: (0, j))],
        out_specs=pl.BlockSpec((tm, tn), lambda j, i: (i, j)),
        out_shape=jax.ShapeDtypeStruct((m, n), F32),
        compiler_params=_cparams("parallel", "parallel"),
        name="mm",
    )(x, w)


def _mm2_kernel(x1_ref, x2_ref, w1_ref, w2_ref, o_ref):
    o_ref[...] = (_dot(x1_ref[...].astype(BF16), w1_ref[...].astype(BF16))
                  + _dot(x2_ref[...].astype(BF16), w2_ref[...].astype(BF16)))


def mm_cat2(x1, x2, w):
    m, k1 = x1.shape
    k2 = x2.shape[1]
    assert k1 == k2 and w.shape[0] == k1 + k2
    n = w.shape[1]
    tm = _pick(m, (512, 256, 128, 64, 32, 16, 8))
    tn = _pick(n, (1024, 512, 256, 128))
    return pl.pallas_call(
        _mm2_kernel,
        grid=(n // tn, m // tm),
        in_specs=[pl.BlockSpec((tm, k1), lambda j, i: (i, 0)),
                  pl.BlockSpec((tm, k2), lambda j, i: (i, 0)),
                  pl.BlockSpec((k1, tn), lambda j, i: (0, j)),
                  pl.BlockSpec((k2, tn), lambda j, i: (1, j))],
        out_specs=pl.BlockSpec((tm, tn), lambda j, i: (i, j)),
        out_shape=jax.ShapeDtypeStruct((m, n), F32),
        compiler_params=_cparams("parallel", "parallel"),
        name="mm_cat2",
    )(x1, x2, w, w)


def _ln_res_kernel(x_ref, y_ref, g_ref, b_ref, o_ref):
    z = DN_ALPHA * x_ref[...] + y_ref[...]
    mu = jnp.mean(z, axis=-1, keepdims=True)
    zc = z - mu
    var = jnp.mean(zc * zc, axis=-1, keepdims=True)
    o_ref[...] = zc * lax.rsqrt(var + LN_EPS) * g_ref[...] + b_ref[...]


def ln_res(x, y, g, b):
    m, d = x.shape
    tm = _pick(m, (256, 192, 128, 64, 32, 16, 8))
    row = pl.BlockSpec((tm, d), lambda i: (i, 0))
    vec = pl.BlockSpec((1, d), lambda i: (0, 0))
    return pl.pallas_call(
        _ln_res_kernel,
        grid=(m // tm,),
        in_specs=[row, row, vec, vec],
        out_specs=row,
        out_shape=jax.ShapeDtypeStruct((m, d), F32),
        compiler_params=_cparams("parallel"),
        name="ln_res",
    )(x, y, g.reshape(1, d), b.reshape(1, d))


def _shifted(carry_ref, x, j):
    ext = jnp.concatenate([carry_ref[...], x], axis=0)
    return pltpu.roll(ext, j, axis=0)[SUBLANES:, :]


def _rwkv_prep_kernel(p_ref, sh0_ref, mu_ref, w0_ref, a0_ref, wlora_ref, alora_ref, gup_ref,
                      kk_ref, ka_ref, rk_ref, bd_ref,
                      r_o, w_o, k_o, kn_o, b_o, v_o, wr_o, br_o, kr_o, g_o, bonus_o, carry):
    ti = pl.program_id(1)

    @pl.when(ti == 0)
    def _():
        carry[...] = jnp.broadcast_to(sh0_ref[0], carry.shape)

    p = p_ref[...]
    tm = p.shape[0]
    p_prev = _shifted(carry, p, 1)
    carry[...] = p[tm - SUBLANES:, :]
    pm = p + mu_ref[...] * (p_prev - p)
    W = RWKV_W
    r = pm[:, 0:W]
    k = pm[:, W:2 * W]
    v = pm[:, 2 * W:3 * W]
    lo = pm[:, 3 * W:3 * W + LORA_DECAY + LORA_A]
    gd = pm[:, 3 * W + LORA_DECAY + LORA_A:RWKV_SHIFT_W]
    lane = lax.broadcasted_iota(jnp.int32, lo.shape, 1)
    z = jnp.where(lane < LORA_DECAY, jnp.tanh(lo), lo).astype(BF16)
    w_l = _dot(z, wlora_ref[...].astype(BF16))
    a_l = _dot(z, alora_ref[...].astype(BF16))
    w = -_softplus(-(w0_ref[...] + w_l)) - 0.5
    decay = jnp.exp(-jnp.exp(w))
    a = _sigmoid(a0_ref[...] + a_l)
    g = _dot(_sigmoid(gd).astype(BF16), gup_ref[...].astype(BF16))
    bd = bd_ref[...]
    kk = k * kk_ref[...]
    ss = _dot2(kk * kk, bd)
    kn = kk * lax.rsqrt(jnp.maximum(ss, 1e-24))
    k_eff = k * (1.0 + (a - 1.0) * ka_ref[...])
    bvec = kn * a
    r_o[...] = r
    w_o[...] = decay
    k_o[...] = k_eff
    kn_o[...] = kn
    b_o[...] = bvec
    v_o[...] = v
    wr_o[...] = decay * r
    br_o[...] = _dot2(bvec * r, bd)
    kr_o[...] = _dot2(k_eff * r, bd)
    g_o[...] = g
    bonus_o[...] = _dot2(r * k_eff * rk_ref[...], bd) * v


def rwkv_prep(p_rw, sh0, B, T, prm):
    tm = _pick(T, (256, 128, 64, 32, 16, 8))
    nt = T // tm
    W = RWKV_W
    row_in = pl.BlockSpec((tm, RWKV_PAD_W), lambda b, t: (b * nt + t, 0))
    row_out = pl.BlockSpec((tm, W), lambda b, t: (b * nt + t, 0))
    full = lambda a: pl.BlockSpec(a.shape, lambda b, t: (0,) * a.ndim)
    sh0p = sh0.reshape(B, 1, RWKV_PAD_W)
    consts = [prm["mu"], prm["w0"], prm["a0"], prm["wlora"], prm["alora"], prm["gup"],
              prm["k_k"], prm["k_a"], prm["r_k"], prm["bd"]]
    outs = pl.pallas_call(
        _rwkv_prep_kernel,
        grid=(B, nt),
        in_specs=[row_in, pl.BlockSpec((1, 1, RWKV_PAD_W), lambda b, t: (b, 0, 0))] + [full(c) for c in consts],
        out_specs=[row_out] * 11,
        out_shape=[jax.ShapeDtypeStruct((B * T, W), F32)] * 11,
        scratch_shapes=[pltpu.VMEM((SUBLANES, RWKV_PAD_W), F32)],
        compiler_params=_cparams("arbitrary", "arbitrary"),
        name="rwkv_prep",
    )(p_rw, sh0p, *consts)
    return outs


PAIR_GROUP = 4
RWKV_SCAN_BATCH = 2


def _rwkv_scan_kernel(r_ref, w_ref, k_ref, kn_ref, b_ref, v_ref, wr_ref, br_ref, kr_ref, s0_ref, oseg_ref,
                      y_ref, sfin_ref, s_scr, *, nb, tc):
    ti = pl.program_id(1)

    @pl.when(ti == 0)
    def _():
        s_scr[...] = s0_ref[...]

    row = lax.broadcasted_iota(jnp.int32, (RWKV_HD, LANES), 0)
    lane = lax.broadcasted_iota(jnp.int32, (RWKV_HD, LANES), 1)
    diag = (lane & (RWKV_HD - 1)) == row
    oseg = oseg_ref[...]
    n_pairs = nb * PAIRS
    R = RWKV_HD

    def step8(t8, carry):
        base = pl.multiple_of(t8 * SUBLANES, SUBLANES)
        for g0 in range(0, n_pairs, PAIR_GROUP):
            ids = [(i // PAIRS, i % PAIRS) for i in range(g0, g0 + PAIR_GROUP)]
            n = len(ids)
            tile = lambda ref, b, p: ref[b, pl.ds(base, SUBLANES), p * LANES:(p + 1) * LANES]
            tiles = {nm: [tile(ref, b, p) for (b, p) in ids]
                     for nm, ref in (("kn", kn_ref), ("wr", wr_ref), ("v", v_ref), ("w", w_ref), ("b", b_ref),
                                     ("k", k_ref), ("br", br_ref), ("kr", kr_ref))}
            s_cur = [s_scr[b * PAIRS + p] for (b, p) in ids]
            y_rows = [[] for _ in ids]
            for j in range(SUBLANES):
                row = lambda nm, i: tiles[nm][i][j:j + 1, :]
                blocks = ([s_cur[i] * row("kn", i) for i in range(n)]
                          + [s_cur[i] * row("wr", i) for i in range(n)]
                          + [jnp.where(diag, row("v", i), 0.0) for i in range(n)])
                red = _dot2(jnp.concatenate(blocks, axis=0), oseg)
                for i in range(n):
                    skk = red[i * R:(i + 1) * R]
                    u = red[(n + i) * R:(n + i + 1) * R]
                    vb = red[(2 * n + i) * R:(2 * n + i + 1) * R]
                    s_cur[i] = s_cur[i] * row("w", i) - skk * row("b", i) + vb * row("k", i)
                    y_bc = u - skk * row("br", i) + vb * row("kr", i)
                    y_rows[i].append(jnp.sum(jnp.where(diag, y_bc, 0.0), axis=0, keepdims=True))
            for i, (b, p) in enumerate(ids):
                s_scr[b * PAIRS + p] = s_cur[i]
                y_ref[b, pl.ds(base, SUBLANES), p * LANES:(p + 1) * LANES] = jnp.concatenate(y_rows[i], axis=0)
        return carry

    lax.fori_loop(0, tc // SUBLANES, step8, 0)

    @pl.when(ti == pl.num_programs(1) - 1)
    def _():
        sfin_ref[...] = s_scr[...]


def rwkv_scan(seqs, s0, B, T):
    tc = _pick(T, (128, 64, 32, 16, 8))
    nb = RWKV_SCAN_BATCH
    assert B % nb == 0
    views = [s.reshape(B, T, RWKV_W) for s in seqs]
    s0p = s0.reshape(B, PAIRS, 2, RWKV_HD, RWKV_HD).transpose(0, 1, 3, 2, 4).reshape(B * PAIRS, RWKV_HD, LANES)
    li = jnp.arange(LANES)
    oseg = (li[:, None] // RWKV_HD == li[None, :] // RWKV_HD).astype(BF16)
    seq_spec = pl.BlockSpec((nb, tc, RWKV_W), lambda b, t: (b, t, 0))
    st_spec = pl.BlockSpec((nb * PAIRS, RWKV_HD, LANES), lambda b, t: (b, 0, 0))
    y, sfin = pl.pallas_call(
        functools.partial(_rwkv_scan_kernel, nb=nb, tc=tc),
        grid=(B // nb, T // tc),
        in_specs=[seq_spec] * 9 + [st_spec, pl.BlockSpec((LANES, LANES), lambda b, t: (0, 0))],
        out_specs=[seq_spec, st_spec],
        out_shape=[jax.ShapeDtypeStruct((B, T, RWKV_W), F32),
                   jax.ShapeDtypeStruct((B * PAIRS, RWKV_HD, LANES), F32)],
        scratch_shapes=[pltpu.VMEM((nb * PAIRS, RWKV_HD, LANES), F32)],
        compiler_params=_cparams("arbitrary", "arbitrary"),
        name="rwkv_scan",
    )(*views, s0p, oseg)
    sfin = sfin.reshape(B, PAIRS, RWKV_HD, 2, RWKV_HD).transpose(0, 1, 3, 2, 4).reshape(B, RWKV_HEADS, RWKV_HD, RWKV_HD)
    return y.reshape(B * T, RWKV_W), sfin


def _rwkv_post_kernel(y_ref, bonus_ref, g_ref, gng_ref, gnb_ref, bd_ref, o_ref):
    y = y_ref[...]
    bd = bd_ref[...]
    inv = 1.0 / RWKV_HD
    mean = _dot2(y, bd) * inv
    yc = y - mean
    var = _dot2(yc * yc, bd) * inv
    yn = yc * lax.rsqrt(var + GN_EPS) * gng_ref[...] + gnb_ref[...]
    o_ref[...] = (yn + bonus_ref[...]) * g_ref[...]


def rwkv_post(y, bonus, g, gn_g, gn_b, bd):
    m, W = y.shape
    tm = _pick(m, (256, 128, 64, 32, 16, 8))
    row = pl.BlockSpec((tm, W), lambda i: (i, 0))
    vec = pl.BlockSpec((1, W), lambda i: (0, 0))
    return pl.pallas_call(
        _rwkv_post_kernel,
        grid=(m // tm,),
        in_specs=[row, row, row, vec, vec, pl.BlockSpec((W, W), lambda i: (0, 0))],
        out_specs=row,
        out_shape=jax.ShapeDtypeStruct((m, W), F32),
        compiler_params=_cparams("parallel"),
        name="rwkv_post",
    )(y, bonus, g, gn_g.reshape(1, W), gn_b.reshape(1, W), bd)


def _gelu_tanh(x):
    return 0.5 * x * (1.0 + jnp.tanh(math.sqrt(2.0 / math.pi) * (x + 0.044715 * (x * x * x))))


def _neg_expm1(x):
    series = -x * (1.0 + x * (1.0 / 2.0) * (1.0 + x * (1.0 / 3.0) * (1.0 + x * (1.0 / 4.0) * (1.0 + x * (1.0 / 5.0)
             * (1.0 + x * (1.0 / 6.0))))))
    return jnp.where(x > -0.05, series, 1.0 - jnp.exp(x))


def _lru_kernel(px_ref, pg_ref, c0_ref, h0_ref, cw_ref, cb_ref, wa_ref, ba_ref, wx_ref, bx_ref, lam_ref,
                y_ref, hlast_ref, carry, hcar):
    ti = pl.program_id(1)

    @pl.when(ti == 0)
    def _():
        carry[...] = c0_ref[0]
        hcar[...] = h0_ref[0]

    x = px_ref[...]
    tm = x.shape[0]
    cw = cw_ref[...]
    xc = cb_ref[...] + cw[CONV_W - 1:CONV_W, :] * x
    for j in range(1, CONV_W):
        xc = xc + cw[CONV_W - 1 - j:CONV_W - j, :] * _shifted(carry, x, j)
    carry[...] = x[tm - SUBLANES:, :]
    xcb = xc.astype(BF16)
    r = _sigmoid(_dot(xcb, wa_ref[...].astype(BF16)) + ba_ref[...])
    i = _sigmoid(_dot(xcb, wx_ref[...].astype(BF16)) + bx_ref[...])
    log_a = (-LRU_C * _softplus(-lam_ref[...])) * r
    a = jnp.exp(log_a)
    b = jnp.sqrt(_neg_expm1(2.0 * log_a)) * (i * xc)
    rows = lax.broadcasted_iota(jnp.int32, a.shape, 0)
    d = 1
    while d < tm:
        a_sh = pltpu.roll(a, d, axis=0)
        b_sh = pltpu.roll(b, d, axis=0)
        keep = rows >= d
        b = jnp.where(keep, a * b_sh + b, b)
        a = jnp.where(keep, a * a_sh, a)
        d *= 2
    h = a * hcar[...] + b
    hcar[...] = h[tm - 1:tm, :]
    hlast_ref[0] = h[tm - 1:tm, :]
    y_ref[...] = h * _gelu_tanh(pg_ref[...])


def lru_mix(p_l, c0, h0, B, T, prm):
    W = LRU_W
    tm = _pick(T, (256, 128, 64, 32, 16, 8))
    nt = T // tm
    c0p = jnp.concatenate([jnp.zeros((B, SUBLANES - (CONV_W - 1), W), F32), c0], axis=1)
    h0p = h0.reshape(B, 1, W)
    full = lambda a: pl.BlockSpec(a.shape, lambda b, t: (0,) * a.ndim)
    consts = [prm["conv_w"], prm["conv_b"], prm["wa"], prm["ba"], prm["wx"], prm["bx"], prm["lam"]]
    y, hlast = pl.pallas_call(
        _lru_kernel,
        grid=(B, nt),
        in_specs=[pl.BlockSpec((tm, W), lambda b, t: (b * nt + t, 0)),
                  pl.BlockSpec((tm, W), lambda b, t: (b * nt + t, 1)),
                  pl.BlockSpec((1, SUBLANES, W), lambda b, t: (b, 0, 0)),
                  pl.BlockSpec((1, 1, W), lambda b, t: (b, 0, 0))] + [full(c) for c in consts],
        out_specs=[pl.BlockSpec((tm, W), lambda b, t: (b * nt + t, 0)),
                   pl.BlockSpec((1, 1, W), lambda b, t: (b, 0, 0))],
        out_shape=[jax.ShapeDtypeStruct((B * T, W), F32), jax.ShapeDtypeStruct((B, 1, W), F32)],
        scratch_shapes=[pltpu.VMEM((SUBLANES, W), F32), pltpu.VMEM((1, W), F32)],
        compiler_params=_cparams("arbitrary", "arbitrary"),
        name="lru_mix",
    )(p_l, p_l, c0p, h0p, *consts)
    return y, hlast.reshape(B, W)


def _block_diag(w):
    n, d, e = w.shape
    eye = jnp.eye(n, dtype=w.dtype)
    return (eye[:, None, :, None] * w[:, :, None, :]).reshape(n * d, n * e)


def rec_params(li, rec_w_in, rec_mu, rwkv_w0, rwkv_w_up, rwkv_a0, rwkv_a_up, rwkv_g_up, rwkv_k_k, rwkv_k_a,
               rwkv_r_k, rwkv_gn_g, rwkv_gn_b, lru_conv_w, lru_conv_b, lru_wa, lru_ba, lru_wx, lru_bx, lru_lambda,
               rec_w_out):
    W = RWKV_W
    pad = RWKV_PAD_W - RWKV_SHIFT_W
    hi = jnp.arange(W) // RWKV_HD
    zl = jnp.zeros((LORA_DECAY, W), F32)
    return dict(
        w_rw=jnp.pad(rec_w_in[li][:, :RWKV_SHIFT_W], ((0, 0), (0, pad))),
        w_l=rec_w_in[li][:, RWKV_SHIFT_W:],
        mu=jnp.pad(rec_mu[li], (0, pad)).reshape(1, RWKV_PAD_W),
        w0=rwkv_w0[li].reshape(1, W), a0=rwkv_a0[li].reshape(1, W),
        wlora=jnp.concatenate([rwkv_w_up[li], zl], axis=0),
        alora=jnp.concatenate([zl, rwkv_a_up[li]], axis=0),
        gup=rwkv_g_up[li],
        k_k=rwkv_k_k[li].reshape(1, W), k_a=rwkv_k_a[li].reshape(1, W), r_k=rwkv_r_k[li].reshape(1, W),
        bd=(hi[:, None] == hi[None, :]).astype(BF16),
        gn_g=rwkv_gn_g[li], gn_b=rwkv_gn_b[li],
        conv_w=lru_conv_w[li], conv_b=lru_conv_b[li].reshape(1, LRU_W),
        wa=_block_diag(lru_wa[li]), ba=lru_ba[li].reshape(1, LRU_W),
        wx=_block_diag(lru_wx[li]), bx=lru_bx[li].reshape(1, LRU_W),
        lam=lru_lambda[li].reshape(1, LRU_W),
        w_out=rec_w_out[li],
    )


def rec_mixer(x2d, B, T, S0, sh0, h0, c0, prm):
    p_rw = mm(x2d, prm["w_rw"])
    p_l = mm(x2d, prm["w_l"])
    sh0p = jnp.pad(sh0, ((0, 0), (0, RWKV_PAD_W - RWKV_SHIFT_W)))
    r, w, k, kn, bv, v, wr, br, kr, g, bonus = rwkv_prep(p_rw, sh0p, B, T, prm)
    y_raw, s_fin = rwkv_scan([r, w, k, kn, bv, v, wr, br, kr], S0, B, T)
    y_rw = rwkv_post(y_raw, bonus, g, prm["gn_g"], prm["gn_b"], prm["bd"])
    y_lru, h_last = lru_mix(p_l, c0, h0, B, T, prm)
    y = mm_cat2(y_rw, y_lru, prm["w_out"])
    sh = p_rw.reshape(B, T, RWKV_PAD_W)[:, -1, :RWKV_SHIFT_W]
    px = p_l.reshape(B, T, 2 * LRU_W)[:, :, :LRU_W]
    cbuf = jnp.concatenate([c0, px], axis=1)[:, -(CONV_W - 1):] if T < CONV_W - 1 else px[:, -(CONV_W - 1):]
    return y, s_fin, sh, h_last, cbuf


def _dot_nt(a, b):
    return lax.dot_general(a, b, (((1,), (1,)), ((), ())), preferred_element_type=F32)


def _compress_kernel(slab_ref, w_ref, o_ref, *, n_pieces):
    acc = None
    for l in range(CMP_STRIDE):
        x = slab_ref[pl.ds(l, n_pieces, stride=CMP_STRIDE), :].astype(BF16)
        d = _dot(x, w_ref[l].astype(BF16))
        acc = d if acc is None else acc + d
    out = acc[:, :HEAD_DIM] + pltpu.roll(acc[:, HEAD_DIM:], n_pieces - 1, axis=0)
    o_ref[0:n_pieces, :] = out
    pad = o_ref.shape[0] - n_pieces
    if pad:
        o_ref[n_pieces:, :] = jnp.zeros((pad, HEAD_DIM), F32)


def nsa_compress(kv2d, col0, B, t_kv, wcat, ncp):
    n_pieces = t_kv // CMP_STRIDE
    return pl.pallas_call(
        functools.partial(_compress_kernel, n_pieces=n_pieces),
        grid=(B, 2, N_KV),
        in_specs=[pl.BlockSpec((t_kv, HEAD_DIM), lambda b, c, g: (b, col0 + c * N_KV + g)),
                  pl.BlockSpec((None, CMP_STRIDE, HEAD_DIM, 2 * HEAD_DIM), lambda b, c, g: (c, 0, 0, 0))],
        out_specs=pl.BlockSpec((None, None, None, ncp, HEAD_DIM), lambda b, c, g: (b, c, g, 0, 0)),
        out_shape=jax.ShapeDtypeStruct((B, 2, N_KV, ncp, HEAD_DIM), F32),
        compiler_params=_cparams("parallel", "parallel", "parallel"),
        name="nsa_compress",
    )(kv2d, wcat)


def _online_softmax_step(q_bf, k_bf, v_bf, bias, mask, m, l, acc):
    s = _dot_nt(q_bf, k_bf) * SCALE - bias
    s = jnp.where(mask, s, NEG)
    m_new = jnp.maximum(m, jnp.max(s, axis=-1, keepdims=True))
    alpha = jnp.exp(m - m_new)
    p = jnp.where(mask, jnp.exp(s - m_new), 0.0)
    l = alpha * l + jnp.sum(p, axis=-1, keepdims=True)
    acc = alpha * acc + _dot(p.astype(BF16), v_bf)
    return m_new, l, acc


def _nsa_kernel(q_ref, kc_ref, vc_ref, ks_ref, vs_ref, kw_ref, vw_ref, gate_ref, gb_ref, ov_ref, o_ref, *,
                tq, q_off, nc_real, ns_real, tk, n_kv_static, w_off, w_len, w_follows_q):
    g = pl.program_id(1)
    qi = pl.program_id(2)
    q0 = qi * tq
    ncp = kc_ref.shape[0]
    nsp = ov_ref.shape[1]
    qpos_col = q_off + q0 + lax.broadcasted_iota(jnp.int32, (tq, 1), 0)
    q_all = q_ref[...]
    q_bf = [q_all[:, r * HEAD_DIM:(r + 1) * HEAD_DIM].astype(BF16) for r in range(GQA_R)]
    head_f = (g * GQA_R).astype(F32)
    slopes = [jnp.exp2(-0.5 * (jnp.full((1, 1), 1.0 + r, F32) + head_f)) for r in range(GQA_R)]

    n_idx = lax.broadcasted_iota(jnp.int32, (tq, ncp), 1)
    dist_c = qpos_col - (n_idx * CMP_STRIDE + (CMP_BLOCK - 1))
    ok_c = (dist_c >= 0) & (n_idx < nc_real)
    dist_cf = dist_c.astype(F32)
    kc_bf = kc_ref[...].astype(BF16)
    vc_bf = vc_ref[...].astype(BF16)
    o_c = []
    p_sum = None
    for r in range(GQA_R):
        s = _dot_nt(q_bf[r], kc_bf) * SCALE - slopes[r] * dist_cf
        s = jnp.where(ok_c, s, NEG)
        m = jnp.max(s, axis=-1, keepdims=True)
        e = jnp.where(ok_c, jnp.exp(s - m), 0.0)
        p = e / jnp.maximum(jnp.sum(e, axis=-1, keepdims=True), 1e-30)
        o_c.append(_dot(p.astype(BF16), vc_bf))
        p_sum = p if p_sum is None else p_sum + p
    imp = _dot2(p_sum, ov_ref[...])
    s_idx = lax.broadcasted_iota(jnp.int32, (tq, nsp), 1)
    cur = lax.shift_right_logical(qpos_col, int(math.log2(SLC_BLOCK)))
    forced = (s_idx == 0) | (s_idx == cur) | (s_idx == cur - 1)
    causal_blk = (s_idx * SLC_BLOCK <= qpos_col) & (s_idx < ns_real)
    score = jnp.where(causal_blk, imp + jnp.where(forced, BIG, 0.0), NEG)
    rank = jnp.zeros((tq, nsp), jnp.int32)
    for j in range(ns_real):
        col = score[:, j:j + 1]
        ahead = (col > score) | ((col == score) & (s_idx > j))
        rank = rank + jnp.where(ahead, 1, 0)
    sel = (rank < min(TOP_N, ns_real)) & (score > 0.5 * NEG)
    sel_bf = jnp.where(sel, 1.0, 0.0).astype(BF16)

    def init():
        return tuple((jnp.full((tq, 1), NEG, F32), jnp.zeros((tq, 1), F32), jnp.zeros((tq, HEAD_DIM), F32))
                     for _ in range(GQA_R))

    blk_per_tile = tk // SLC_BLOCK
    ex_row = lax.broadcasted_iota(jnp.int32, (nsp, tk), 0)
    ex_col = lax.shift_right_logical(lax.broadcasted_iota(jnp.int32, (nsp, tk), 1), int(math.log2(SLC_BLOCK)))
    key_i = lax.broadcasted_iota(jnp.int32, (tq, tk), 1)

    def slc_body(kt, carry):
        base = pl.multiple_of(kt * tk, tk)
        expand = jnp.where(ex_row == ex_col + kt * blk_per_tile, 1.0, 0.0).astype(BF16)
        selexp = _dot(sel_bf, expand)
        dist = qpos_col - (key_i + base)
        mask = (dist >= 0) & (selexp > 0.5)
        distf = dist.astype(F32)
        k_bf = ks_ref[pl.ds(base, tk), :].astype(BF16)
        v_bf = vs_ref[pl.ds(base, tk), :].astype(BF16)
        return tuple(_online_softmax_step(q_bf[r], k_bf, v_bf, slopes[r] * distf, mask, *carry[r])
                     for r in range(GQA_R))

    if n_kv_static is None:
        n_kv = (q0 + tq + tk - 1) // tk
    else:
        n_kv = n_kv_static
    st_s = lax.fori_loop(0, n_kv, slc_body, init())

    if w_follows_q:
        w_base = pl.multiple_of(jnp.maximum(q0 + tq - w_len, 0), SUBLANES)
    else:
        w_base = 0
    dist_w = qpos_col - (w_off + w_base + lax.broadcasted_iota(jnp.int32, (tq, w_len), 1))
    ok_w = jnp.where(dist_w >= 0, dist_w, WINDOW) < WINDOW
    dist_wf = dist_w.astype(F32)
    kw_bf = kw_ref[pl.ds(w_base, w_len), :].astype(BF16)
    vw_bf = vw_ref[pl.ds(w_base, w_len), :].astype(BF16)
    st_w = tuple(_online_softmax_step(q_bf[r], kw_bf, vw_bf, slopes[r] * dist_wf, ok_w, *init()[r])
                 for r in range(GQA_R))

    gates = _sigmoid(gate_ref[...] + gb_ref[...])
    lane = lax.broadcasted_iota(jnp.int32, gates.shape, 1)
    for r in range(GQA_R):
        head = g * GQA_R + r
        gcol = lambda br: jnp.sum(jnp.where(lane == br * N_HEADS + head, gates, 0.0), axis=-1, keepdims=True)
        o_s = st_s[r][2] / jnp.maximum(st_s[r][1], 1e-30)
        o_w = st_w[r][2] / jnp.maximum(st_w[r][1], 1e-30)
        o_ref[:, r * HEAD_DIM:(r + 1) * HEAD_DIM] = gcol(0) * o_c[r] + gcol(1) * o_s + gcol(2) * o_w


def nsa_attention(q2d, kcvc, kv2d, kv_col0, win2d, win_col0, gates, gate_b, overlap, *, B, t_q, t_kv, t_win, tq,
                  q_off, nc_real, ns_real, tk, n_kv_static, w_off, w_len, w_follows_q):
    nq = t_q // tq
    ncp = kcvc.shape[3]
    nsp = overlap.shape[1]
    qw = GQA_R * HEAD_DIM
    slab = lambda arr_t, col: pl.BlockSpec((arr_t, HEAD_DIM), col)
    kern = functools.partial(_nsa_kernel, tq=tq, q_off=q_off, nc_real=nc_real, ns_real=ns_real, tk=tk,
                             n_kv_static=n_kv_static, w_off=w_off, w_len=w_len, w_follows_q=w_follows_q)
    return pl.pallas_call(
        kern,
        grid=(B, N_KV, nq),
        in_specs=[pl.BlockSpec((tq, qw), lambda b, g, i: (b * nq + i, g)),
                  pl.BlockSpec((None, None, None, ncp, HEAD_DIM), lambda b, g, i: (b, 0, g, 0, 0)),
                  pl.BlockSpec((None, None, None, ncp, HEAD_DIM), lambda b, g, i: (b, 1, g, 0, 0)),
                  slab(t_kv, lambda b, g, i: (b, kv_col0 + 2 * N_KV + g)),
                  slab(t_kv, lambda b, g, i: (b, kv_col0 + 3 * N_KV + g)),
                  slab(t_win, lambda b, g, i: (b, win_col0 + g)),
                  slab(t_win, lambda b, g, i: (b, win_col0 + N_KV + g)),
                  pl.BlockSpec((tq, LANES), lambda b, g, i: (b * nq + i, 0)),
                  pl.BlockSpec((1, LANES), lambda b, g, i: (0, 0)),
                  pl.BlockSpec((ncp, nsp), lambda b, g, i: (0, 0))],
        out_specs=pl.BlockSpec((tq, qw), lambda b, g, i: (b * nq + i, g)),
        out_shape=jax.ShapeDtypeStruct((B * t_q, N_HEADS * HEAD_DIM), F32),
        compiler_params=_cparams("parallel", "parallel", "arbitrary"),
        name="nsa_attention",
    )(q2d, kcvc, kcvc, kv2d, kv2d, win2d, win2d, gates, gate_b, overlap)


def _overlap_matrix(ncp, nsp, nc_real, ns_real):
    n = jnp.arange(ncp)[:, None]
    s = jnp.arange(nsp)[None, :]
    c0 = n * CMP_STRIDE
    s0 = s * SLC_BLOCK
    ov = (c0 < s0 + SLC_BLOCK) & (c0 + CMP_BLOCK > s0) & (n < nc_real) & (s < ns_real)
    return ov.astype(BF16)


def _gather_pages_kernel(pt_ref, page_ref, tail_ref, o_ref):
    j = pl.program_id(1)
    n_pages = pl.num_programs(1) - 1

    @pl.when(j < n_pages)
    def _():
        o_ref[...] = page_ref[...]

    @pl.when(j == n_pages)
    def _():
        o_ref[...] = tail_ref[...]


def gather_pages(cache5, page_table, tail, li):
    B, n_pages = page_table.shape
    page, row_w = cache5.shape[1], cache5.shape[4]
    grid_spec = pltpu.PrefetchScalarGridSpec(
        num_scalar_prefetch=1,
        grid=(B, n_pages + 1),
        in_specs=[pl.BlockSpec((None, page, None, None, row_w),
                               lambda b, j, pt: (pt[b, jnp.minimum(j, n_pages - 1)], 0, li, 0, 0)),
                  pl.BlockSpec((None, page, row_w), lambda b, j, pt: (b, 0, 0))],
        out_specs=pl.BlockSpec((None, page, row_w), lambda b, j, pt: (b, j, 0)),
    )
    return pl.pallas_call(
        _gather_pages_kernel,
        grid_spec=grid_spec,
        out_shape=jax.ShapeDtypeStruct((B, (n_pages + 1) * page, row_w), F32),
        compiler_params=_cparams("parallel", "arbitrary"),
        name="gather_pages",
    )(page_table, cache5, tail)


def attn_params(li, attn_w_in, attn_gate_b, w_cmp_k, w_cmp_v, attn_w_out):
    qkv_w = N_HEADS * HEAD_DIM + 6 * N_KV * HEAD_DIM
    ng = 3 * N_HEADS
    half = lambda w: jnp.concatenate([w[:CMP_STRIDE], w[CMP_STRIDE:]], axis=-1)
    return dict(
        w_main=attn_w_in[li][:, :qkv_w],
        w_gate=jnp.pad(attn_w_in[li][:, qkv_w:], ((0, 0), (0, LANES - ng))),
        gate_b=jnp.pad(attn_gate_b[li], (0, LANES - ng)).reshape(1, LANES),
        wcat=jnp.stack([half(w_cmp_k[li]), half(w_cmp_v[li])]),
        w_out=attn_w_out[li],
    )


def attn_prompt(x2d, B, T, prm):
    p = mm(x2d, prm["w_main"])
    gates = mm(x2d, prm["w_gate"])
    q_blocks = N_HEADS
    n_pieces = T // CMP_STRIDE
    nc_real = n_pieces - CMP_BLOCK // CMP_STRIDE + 1
    ns_real = T // SLC_BLOCK
    ncp = -(-n_pieces // LANES) * LANES
    nsp = -(-ns_real // LANES) * LANES
    kcvc = nsa_compress(p, q_blocks, B, T, prm["wcat"], ncp)
    ov = _overlap_matrix(ncp, nsp, nc_real, ns_real)
    tq = min(128, T)
    o = nsa_attention(p, kcvc, p, q_blocks, p, q_blocks + 4 * N_KV, gates, prm["gate_b"], ov,
                      B=B, t_q=T, t_kv=T, t_win=T, tq=tq, q_off=0, nc_real=nc_real, ns_real=ns_real,
                      tk=min(512, T), n_kv_static=None, w_off=0, w_len=min(WINDOW + tq, T), w_follows_q=True)
    y = mm(o, prm["w_out"])
    kv = p.reshape(B, T, -1)[:, :, N_HEADS * HEAD_DIM:].reshape(B, T, 6, N_KV, HEAD_DIM)
    w_buf = min(WINDOW, T)
    win = jnp.pad(kv[:, :, 4:], ((0, 0), (WINDOW, 0), (0, 0), (0, 0), (0, 0)))[:, -w_buf:]
    return y, kv[:, :, :4], win


def attn_sample(x2d, B, T, prm, cache_nsa_kv, cache_win, page_table, li):
    n_pool, page = cache_nsa_kv.shape[:2]
    n_attn = cache_nsa_kv.shape[2]
    n_pages = page_table.shape[1]
    past_len = n_pages * page
    row_w = 4 * N_KV * HEAD_DIM
    p = mm(x2d, prm["w_main"])
    gates = mm(x2d, prm["w_gate"])
    kv_new = p[:, N_HEADS * HEAD_DIM:].reshape(B, T, 6, N_KV, HEAD_DIM)
    tail = jnp.pad(kv_new[:, :, :4].reshape(B, T, row_w), ((0, 0), (0, page - T), (0, 0)))
    full = gather_pages(cache_nsa_kv.reshape(n_pool, page, n_attn, 1, row_w), page_table, tail, li)
    t_kv = (n_pages + 1) * page
    full2d = full.reshape(B * t_kv, row_w)
    L = past_len + T
    l_pad = -(-L // SLC_BLOCK) * SLC_BLOCK
    nc_real = l_pad // CMP_STRIDE - CMP_BLOCK // CMP_STRIDE + 1
    ns_real = l_pad // SLC_BLOCK
    n_pieces = t_kv // CMP_STRIDE
    ncp = -(-n_pieces // LANES) * LANES
    nsp = -(-ns_real // LANES) * LANES
    kcvc = nsa_compress(full2d, 0, B, t_kv, prm["wcat"], ncp)
    ov = _overlap_matrix(ncp, nsp, nc_real, ns_real)
    w_buf = cache_win.shape[1]
    win = jnp.concatenate([cache_win, kv_new[:, :, 4:]], axis=1)
    t_win = -(-(w_buf + T) // LANES) * LANES
    win2d = jnp.pad(win.reshape(B, w_buf + T, 2 * N_KV * HEAD_DIM), ((0, 0), (0, t_win - (w_buf + T)), (0, 0)))
    win2d = win2d.reshape(B * t_win, 2 * N_KV * HEAD_DIM)
    n_kv = 3
    assert t_kv % (n_kv * LANES) == 0
    o = nsa_attention(p, kcvc, full2d, 0, win2d, 0, gates, prm["gate_b"], ov,
                      B=B, t_q=T, t_kv=t_kv, t_win=t_win, tq=T, q_off=past_len, nc_real=nc_real, ns_real=ns_real,
                      tk=t_kv // n_kv, n_kv_static=n_kv, w_off=past_len - w_buf, w_len=t_win, w_follows_q=False)
    y = mm(o, prm["w_out"])
    return y, kv_new[:, :, :4], win[:, -w_buf:]


MOE_TM = 256


def _router_kernel(x_ref, w_ref, o_ref):
    xh, xl = _split2(x_ref[...])
    wh, wl = _split2(w_ref[...])
    o_ref[...] = _dot(xh, wh) + _dot(xl, wh) + _dot(xh, wl)


def router_logits(x, w_pad):
    m, d = x.shape
    tm = _pick(m, (512, 256, 192, 128, 64, 32, 16, 8))
    return pl.pallas_call(
        _router_kernel,
        grid=(m // tm,),
        in_specs=[pl.BlockSpec((tm, d), lambda i: (i, 0)), pl.BlockSpec((d, LANES), lambda i: (0, 0))],
        out_specs=pl.BlockSpec((tm, LANES), lambda i: (i, 0)),
        out_shape=jax.ShapeDtypeStruct((m, LANES), F32),
        compiler_params=_cparams("parallel"),
        name="router_logits",
    )(x, w_pad)


def _expert_kernel(te_ref, x_ref, rw_ref, wg_ref, wu_ref, wd_ref, o_ref):
    xb = x_ref[...].astype(BF16)
    hg = _dot(xb, wg_ref[...])
    hu = _dot(xb, wu_ref[...])
    h = (hg * _sigmoid(hg)) * hu * rw_ref[...]
    o_ref[...] = _dot(h.astype(BF16), wd_ref[...])


def expert_ffn(x_sorted, row_w, tile_expert, wg, wu, wd):
    a_pad, d = x_sorted.shape
    f = wg.shape[2]
    n_tiles = a_pad // MOE_TM
    grid_spec = pltpu.PrefetchScalarGridSpec(
        num_scalar_prefetch=1,
        grid=(n_tiles,),
        in_specs=[pl.BlockSpec((MOE_TM, d), lambda t, te: (t, 0)),
                  pl.BlockSpec((MOE_TM, 1), lambda t, te: (t, 0)),
                  pl.BlockSpec((None, d, f), lambda t, te: (te[t], 0, 0)),
                  pl.BlockSpec((None, d, f), lambda t, te: (te[t], 0, 0)),
                  pl.BlockSpec((None, f, d), lambda t, te: (te[t], 0, 0))],
        out_specs=pl.BlockSpec((MOE_TM, d), lambda t, te: (t, 0)),
    )
    return pl.pallas_call(
        _expert_kernel,
        grid_spec=grid_spec,
        out_shape=jax.ShapeDtypeStruct((a_pad, d), F32),
        compiler_params=_cparams("arbitrary"),
        name="expert_ffn",
    )(tile_expert, x_sorted, row_w, wg, wu, wd)


def moe(x, router_w, r_bias, w_g, w_u, w_d):
    n, d = x.shape
    logits = router_logits(x, router_w)[:, :N_EXPERTS]
    aff = jax.nn.sigmoid(logits)
    sel = (aff + r_bias).reshape(n, N_GROUPS, EXPERTS_PER_GROUP)
    grp_score = lax.top_k(sel, TOP_K)[0].sum(-1)
    g_star = jnp.argmax(grp_score, axis=-1)
    in_grp = jnp.take_along_axis(sel, g_star[:, None, None], axis=1)[:, 0]
    _, loc = lax.top_k(in_grp, TOP_K)
    e_idx = g_star[:, None] * EXPERTS_PER_GROUP + loc
    w_sel = jnp.take_along_axis(aff, e_idx, axis=-1)
    w_sel = w_sel / jnp.sum(w_sel, axis=-1, keepdims=True)
    n_asg = n * TOP_K
    e_flat = e_idx.reshape(n_asg).astype(jnp.int32)
    onehot = (e_flat[:, None] == jnp.arange(N_EXPERTS, dtype=jnp.int32)[None, :]).astype(jnp.int32)
    running = jnp.cumsum(onehot, axis=0)
    rank = jnp.sum(running * onehot, axis=1) - 1
    counts = running[-1]
    gsz = (counts + MOE_TM - 1) // MOE_TM * MOE_TM
    p_end = jnp.cumsum(gsz)
    p_start = p_end - gsz
    dest = jnp.sum(onehot * p_start[None, :], axis=1) + rank
    a_pad = -(-(n_asg + N_EXPERTS * (MOE_TM - 1)) // MOE_TM) * MOE_TM
    src_tok = jnp.zeros((a_pad,), jnp.int32).at[dest].set(jnp.arange(n_asg, dtype=jnp.int32) // TOP_K)
    row_w = jnp.zeros((a_pad,), F32).at[dest].set(w_sel.reshape(n_asg))
    pos = dest
    tile_start = jnp.arange(a_pad // MOE_TM, dtype=jnp.int32) * MOE_TM
    tile_expert = jnp.minimum(jnp.searchsorted(p_end, tile_start, side="right"), N_EXPERTS - 1).astype(jnp.int32)
    ys = expert_ffn(x[src_tok], row_w.reshape(a_pad, 1), tile_expert,
                    w_g.astype(BF16), w_u.astype(BF16), w_d.astype(BF16))
    pos = pos.reshape(n, TOP_K)
    return ys[pos[:, 0]] + ys[pos[:, 1]]


def kernel(x_prompt, x_sample, cache_nsa_kv, cache_win_kv, state_rwkv, state_rwkv_shift, state_lru_h, state_lru_conv,
           page_table, ln1_g, ln1_b, ln2_g, ln2_b, rec_w_in, rec_mu, rwkv_w0, rwkv_w_up, rwkv_a0, rwkv_a_up, rwkv_g_up,
           rwkv_k_k, rwkv_k_a, rwkv_r_k, rwkv_gn_g, rwkv_gn_b, lru_conv_w, lru_conv_b, lru_wa, lru_ba, lru_wx, lru_bx,
           lru_lambda, rec_w_out, attn_w_in, attn_gate_b, w_cmp_k, w_cmp_v, attn_w_out, moe_w_router, moe_router_bias,
           moe_w_gate, moe_w_up, moe_w_down):
    Bp, Tp, D = x_prompt.shape
    Bs, Ts, _ = x_sample.shape
    n_p, n_s = Bp * Tp, Bs * Ts
    rec_args = (rec_w_in, rec_mu, rwkv_w0, rwkv_w_up, rwkv_a0, rwkv_a_up, rwkv_g_up, rwkv_k_k, rwkv_k_a,
                rwkv_r_k, rwkv_gn_g, rwkv_gn_b, lru_conv_w, lru_conv_b, lru_wa, lru_ba, lru_wx, lru_bx, lru_lambda,
                rec_w_out)
    x = jnp.concatenate([x_prompt.reshape(n_p, D), x_sample.reshape(n_s, D)], axis=0)
    router_w = jnp.pad(moe_w_router, ((0, 0), (0, LANES - N_EXPERTS)))
    nsa_p, nsa_s, win_p, win_s = [], [], [], []
    rs_p, rs_s, rsh_p, rsh_s, lh_p, lh_s, lc_p, lc_s = [], [], [], [], [], [], [], []
    for layer in range(DEPTH):
        li = layer // 2
        xp, xs = x[:n_p], x[n_p:]
        if layer % 2 == 0:
            prm = rec_params(li, *rec_args)
            yp, S, sh, h, cb = rec_mixer(xp, Bp, Tp, jnp.zeros((Bp, RWKV_HEADS, RWKV_HD, RWKV_HD), F32),
                                         jnp.zeros((Bp, RWKV_SHIFT_W), F32), jnp.zeros((Bp, LRU_W), F32),
                                         jnp.zeros((Bp, CONV_W - 1, LRU_W), F32), prm)
            ys, S2, sh2, h2, cb2 = rec_mixer(xs, Bs, Ts, state_rwkv[:, li], state_rwkv_shift[:, li],
                                             state_lru_h[:, li], state_lru_conv[:, li], prm)
            rs_p.append(S); rs_s.append(S2); rsh_p.append(sh); rsh_s.append(sh2)
            lh_p.append(h); lh_s.append(h2); lc_p.append(cb); lc_s.append(cb2)
        else:
            prm = attn_params(li, attn_w_in, attn_gate_b, w_cmp_k, w_cmp_v, attn_w_out)
            yp, rows, wb = attn_prompt(xp, Bp, Tp, prm)
            ys, rows2, wb2 = attn_sample(xs, Bs, Ts, prm, cache_nsa_kv, cache_win_kv[:, :, li], page_table, li)
            nsa_p.append(rows); nsa_s.append(rows2); win_p.append(wb); win_s.append(wb2)
        x = ln_res(x, jnp.concatenate([yp, ys], axis=0), ln1_g[layer], ln1_b[layer])
        y_moe = moe(x, router_w, moe_router_bias, moe_w_gate[layer], moe_w_up[layer], moe_w_down[layer])
        x = ln_res(x, y_moe, ln2_g[layer], ln2_b[layer])
    return (x[:n_p].reshape(Bp, Tp, D), x[n_p:].reshape(Bs, Ts, D),
            jnp.stack(nsa_p, axis=2), jnp.stack(nsa_s, axis=2), jnp.stack(win_p, axis=2), jnp.stack(win_s, axis=2),
            jnp.stack(rs_p, axis=1), jnp.stack(rs_s, axis=1), jnp.stack(rsh_p, axis=1), jnp.stack(rsh_s, axis=1),
            jnp.stack(lh_p, axis=1), jnp.stack(lh_s, axis=1), jnp.stack(lc_p, axis=1), jnp.stack(lc_s, axis=1))
```

```python
import functools
import math

import jax
import jax.numpy as jnp
from jax import lax
from jax.experimental import pallas as pl
from jax.experimental.pallas import tpu as pltpu

F32 = jnp.float32
BF16 = jnp.bfloat16

D_MODEL = 2048
DEPTH = 4
RWKV_HEADS = 16
RWKV_HD = 64
RWKV_W = RWKV_HEADS * RWKV_HD
LORA_DECAY = 64
LORA_A = 64
LORA_GATE = 128
RWKV_SHIFT_W = 3 * RWKV_W + LORA_DECAY + LORA_A + LORA_GATE
GN_EPS = 64e-5
LRU_W = D_MODEL - RWKV_W
LRU_BLOCKS = 16
CONV_W = 4
LRU_C = 8.0
N_HEADS = 16
HEAD_DIM = 128
N_KV = 4
GQA_R = N_HEADS // N_KV
CMP_BLOCK = 32
CMP_STRIDE = 16
SLC_BLOCK = 64
TOP_N = 16
WINDOW = 512
SCALE = HEAD_DIM ** -0.5
N_EXPERTS = 16
N_GROUPS = 4
EXPERTS_PER_GROUP = N_EXPERTS // N_GROUPS
TOP_K = 2
D_EXPERT = 1024
LN_EPS = 1e-5
DN_ALPHA = (2 * DEPTH) ** 0.25
NEG = -1e30
BIG = 1e4

LANES = 128
SUBLANES = 8
VMEM_LIMIT_BYTES = 48 * 1024 * 1024

RWKV_PAD_W = 3584
PAIRS = RWKV_HEADS // 2


def _cparams(*sem):
    return pltpu.CompilerParams(dimension_semantics=sem, vmem_limit_bytes=VMEM_LIMIT_BYTES)


def _split2(x):
    hi = x.astype(BF16)
    lo = (x - hi.astype(F32)).astype(BF16)
    return hi, lo


def _dot(a, b):
    return jnp.dot(a, b, preferred_element_type=F32)


def _dot2(x, w_bf):
    hi, lo = _split2(x)
    return _dot(hi, w_bf) + _dot(lo, w_bf)


def _sigmoid(x):
    return 1.0 / (1.0 + jnp.exp(-x))


def _softplus(x):
    return jnp.maximum(x, 0.0) + jnp.log(1.0 + jnp.exp(-jnp.abs(x)))


def _mm_kernel(x_ref, w_ref, o_ref):
    o_ref[...] = _dot(x_ref[...].astype(BF16), w_ref[...].astype(BF16))


def _pick(n, cands):
    for c in cands:
        if n % c == 0:
            return c
    return n


def mm(x, w):
    m, k = x.shape
    n = w.shape[1]
    tm = _pick(m, (512, 256, 128, 64, 32, 16, 8))
    tn = _pick(n, (1024, 768, 512, 256, 128))
    return pl.pallas_call(
        _mm_kernel,
        grid=(n // tn, m // tm),
        in_specs=[pl.BlockSpec((tm, k), lambda j, i: (i, 0)),
                  pl.BlockSpec((k, tn), lambda j, i: (0, j))],
        out_specs=pl.BlockSpec((tm, tn), lambda j, i: (i, j)),
        out_shape=jax.ShapeDtypeStruct((m, n), F32),
        compiler_params=_cparams("parallel", "parallel"),
        name="mm",
    )(x, w)


def _mm2_kernel(x1_ref, x2_ref, w1_ref, w2_ref, o_ref):
    o_ref[...] = (_dot(x1_ref[...].astype(BF16), w1_ref[...].astype(BF16))
                  + _dot(x2_ref[...].astype(BF16), w2_ref[...].astype(BF16)))


def mm_cat2(x1, x2, w):
    m, k1 = x1.shape
    k2 = x2.shape[1]
    assert k1 == k2 and w.shape[0] == k1 + k2
    n = w.shape[1]
    tm = _pick(m, (512, 256, 128, 64, 32, 16, 8))
    tn = _pick(n, (1024, 512, 256, 128))
    return pl.pallas_call(
        _mm2_kernel,
        grid=(n // tn, m // tm),
        in_specs=[pl.BlockSpec((tm, k1), lambda j, i: (i, 0)),
                  pl.BlockSpec((tm, k2), lambda j, i: (i, 0)),
                  pl.BlockSpec((k1, tn), lambda j, i: (0, j)),
                  pl.BlockSpec((k2, tn), lambda j, i: (1, j))],
        out_specs=pl.BlockSpec((tm, tn), lambda j, i: (i, j)),
        out_shape=jax.ShapeDtypeStruct((m, n), F32),
        compiler_params=_cparams("parallel", "parallel"),
        name="mm_cat2",
    )(x1, x2, w, w)


def _ln_res_kernel(x_ref, y_ref, g_ref, b_ref, o_ref):
    z = DN_ALPHA * x_ref[...] + y_ref[...]
    mu = jnp.mean(z, axis=-1, keepdims=True)
    zc = z - mu
    var = jnp.mean(zc * zc, axis=-1, keepdims=True)
    o_ref[...] = zc * lax.rsqrt(var + LN_EPS) * g_ref[...] + b_ref[...]


def ln_res(x, y, g, b):
    m, d = x.shape
    tm = _pick(m, (256, 192, 128, 64, 32, 16, 8))
    row = pl.BlockSpec((tm, d), lambda i: (i, 0))
    vec = pl.BlockSpec((1, d), lambda i: (0, 0))
    return pl.pallas_call(
        _ln_res_kernel,
        grid=(m // tm,),
        in_specs=[row, row, vec, vec],
        out_specs=row,
        out_shape=jax.ShapeDtypeStruct((m, d), F32),
        compiler_params=_cparams("parallel"),
        name="ln_res",
    )(x, y, g.reshape(1, d), b.reshape(1, d))


def _shifted(carry_ref, x, j):
    ext = jnp.concatenate([carry_ref[...], x], axis=0)
    return pltpu.roll(ext, j, axis=0)[SUBLANES:, :]


def _rwkv_prep_kernel(p_ref, sh0_ref, mu_ref, w0_ref, a0_ref, wlora_ref, alora_ref, gup_ref,
                      kk_ref, ka_ref, rk_ref, bd_ref,
                      r_o, w_o, k_o, kn_o, b_o, v_o, wr_o, br_o, kr_o, g_o, bonus_o, carry):
    ti = pl.program_id(1)

    @pl.when(ti == 0)
    def _():
        carry[...] = jnp.broadcast_to(sh0_ref[0], carry.shape)

    p = p_ref[...]
    tm = p.shape[0]
    p_prev = _shifted(carry, p, 1)
    carry[...] = p[tm - SUBLANES:, :]
    pm = p + mu_ref[...] * (p_prev - p)
    W = RWKV_W
    r = pm[:, 0:W]
    k = pm[:, W:2 * W]
    v = pm[:, 2 * W:3 * W]
    lo = pm[:, 3 * W:3 * W + LORA_DECAY + LORA_A]
    gd = pm[:, 3 * W + LORA_DECAY + LORA_A:RWKV_SHIFT_W]
    lane = lax.broadcasted_iota(jnp.int32, lo.shape, 1)
    z = jnp.where(lane < LORA_DECAY, jnp.tanh(lo), lo).astype(BF16)
    w_l = _dot(z, wlora_ref[...].astype(BF16))
    a_l = _dot(z, alora_ref[...].astype(BF16))
    w = -_softplus(-(w0_ref[...] + w_l)) - 0.5
    decay = jnp.exp(-jnp.exp(w))
    a = _sigmoid(a0_ref[...] + a_l)
    g = _dot(_sigmoid(gd).astype(BF16), gup_ref[...].astype(BF16))
    bd = bd_ref[...]
    kk = k * kk_ref[...]
    ss = _dot2(kk * kk, bd)
    kn = kk * lax.rsqrt(jnp.maximum(ss, 1e-24))
    k_eff = k * (1.0 + (a - 1.0) * ka_ref[...])
    bvec = kn * a
    r_o[...] = r
    w_o[...] = decay
    k_o[...] = k_eff
    kn_o[...] = kn
    b_o[...] = bvec
    v_o[...] = v
    wr_o[...] = decay * r
    br_o[...] = _dot2(bvec * r, bd)
    kr_o[...] = _dot2(k_eff * r, bd)
    g_o[...] = g
    bonus_o[...] = _dot2(r * k_eff * rk_ref[...], bd) * v


def rwkv_prep(p_rw, sh0, B, T, prm):
    tm = _pick(T, (256, 128, 64, 32, 16, 8))
    nt = T // tm
    W = RWKV_W
    row_in = pl.BlockSpec((tm, RWKV_PAD_W), lambda b, t: (b * nt + t, 0))
    row_out = pl.BlockSpec((tm, W), lambda b, t: (b * nt + t, 0))
    full = lambda a: pl.BlockSpec(a.shape, lambda b, t: (0,) * a.ndim)
    sh0p = sh0.reshape(B, 1, RWKV_PAD_W)
    consts = [prm["mu"], prm["w0"], prm["a0"], prm["wlora"], prm["alora"], prm["gup"],
              prm["k_k"], prm["k_a"], prm["r_k"], prm["bd"]]
    outs = pl.pallas_call(
        _rwkv_prep_kernel,
        grid=(B, nt),
        in_specs=[row_in, pl.BlockSpec((1, 1, RWKV_PAD_W), lambda b, t: (b, 0, 0))] + [full(c) for c in consts],
        out_specs=[row_out] * 11,
        out_shape=[jax.ShapeDtypeStruct((B * T, W), F32)] * 11,
        scratch_shapes=[pltpu.VMEM((SUBLANES, RWKV_PAD_W), F32)],
        compiler_params=_cparams("arbitrary", "arbitrary"),
        name="rwkv_prep",
    )(p_rw, sh0p, *consts)
    return outs


PAIR_GROUP = 4
RWKV_SCAN_BATCH = 2


def _rwkv_scan_kernel(r_ref, w_ref, k_ref, kn_ref, b_ref, v_ref, wr_ref, br_ref, kr_ref, s0_ref, oseg_ref,
                      y_ref, sfin_ref, s_scr, *, nb, tc):
    ti = pl.program_id(1)

    @pl.when(ti == 0)
    def _():
        s_scr[...] = s0_ref[...]

    row = lax.broadcasted_iota(jnp.int32, (RWKV_HD, LANES), 0)
    lane = lax.broadcasted_iota(jnp.int32, (RWKV_HD, LANES), 1)
    diag = (lane & (RWKV_HD - 1)) == row
    oseg = oseg_ref[...]
    n_pairs = nb * PAIRS
    R = RWKV_HD

    def step8(t8, carry):
        base = pl.multiple_of(t8 * SUBLANES, SUBLANES)
        for g0 in range(0, n_pairs, PAIR_GROUP):
            ids = [(i // PAIRS, i % PAIRS) for i in range(g0, g0 + PAIR_GROUP)]
            n = len(ids)
            tile = lambda ref, b, p: ref[b, pl.ds(base, SUBLANES), p * LANES:(p + 1) * LANES]
            tiles = {nm: [tile(ref, b, p) for (b, p) in ids]
                     for nm, ref in (("kn", kn_ref), ("wr", wr_ref), ("v", v_ref), ("w", w_ref), ("b", b_ref),
                                     ("k", k_ref), ("br", br_ref), ("kr", kr_ref))}
            s_cur = [s_scr[b * PAIRS + p] for (b, p) in ids]
            y_rows = [[] for _ in ids]
            for j in range(SUBLANES):
                row = lambda nm, i: tiles[nm][i][j:j + 1, :]
                blocks = ([s_cur[i] * row("kn", i) for i in range(n)]
                          + [s_cur[i] * row("wr", i) for i in range(n)]
                          + [jnp.where(diag, row("v", i), 0.0) for i in range(n)])
                red = _dot2(jnp.concatenate(blocks, axis=0), oseg)
                for i in range(n):
                    skk = red[i * R:(i + 1) * R]
                    u = red[(n + i) * R:(n + i + 1) * R]
                    vb = red[(2 * n + i) * R:(2 * n + i + 1) * R]
                    s_cur[i] = s_cur[i] * row("w", i) - skk * row("b", i) + vb * row("k", i)
                    y_bc = u - skk * row("br", i) + vb * row("kr", i)
                    y_rows[i].append(jnp.sum(jnp.where(diag, y_bc, 0.0), axis=0, keepdims=True))
            for i, (b, p) in enumerate(ids):
                s_scr[b * PAIRS + p] = s_cur[i]
                y_ref[b, pl.ds(base, SUBLANES), p * LANES:(p + 1) * LANES] = jnp.concatenate(y_rows[i], axis=0)
        return carry

    lax.fori_loop(0, tc // SUBLANES, step8, 0)

    @pl.when(ti == pl.num_programs(1) - 1)
    def _():
        sfin_ref[...] = s_scr[...]


def rwkv_scan(seqs, s0, B, T):
    tc = _pick(T, (128, 64, 32, 16, 8))
    nb = RWKV_SCAN_BATCH
    assert B % nb == 0
    views = [s.reshape(B, T, RWKV_W) for s in seqs]
    s0p = s0.reshape(B, PAIRS, 2, RWKV_HD, RWKV_HD).transpose(0, 1, 3, 2, 4).reshape(B * PAIRS, RWKV_HD, LANES)
    li = jnp.arange(LANES)
    oseg = (li[:, None] // RWKV_HD == li[None, :] // RWKV_HD).astype(BF16)
    seq_spec = pl.BlockSpec((nb, tc, RWKV_W), lambda b, t: (b, t, 0))
    st_spec = pl.BlockSpec((nb * PAIRS, RWKV_HD, LANES), lambda b, t: (b, 0, 0))
    y, sfin = pl.pallas_call(
        functools.partial(_rwkv_scan_kernel, nb=nb, tc=tc),
        grid=(B // nb, T // tc),
        in_specs=[seq_spec] * 9 + [st_spec, pl.BlockSpec((LANES, LANES), lambda b, t: (0, 0))],
        out_specs=[seq_spec, st_spec],
        out_shape=[jax.ShapeDtypeStruct((B, T, RWKV_W), F32),
                   jax.ShapeDtypeStruct((B * PAIRS, RWKV_HD, LANES), F32)],
        scratch_shapes=[pltpu.VMEM((nb * PAIRS, RWKV_HD, LANES), F32)],
        compiler_params=_cparams("arbitrary", "arbitrary"),
        name="rwkv_scan",
    )(*views, s0p, oseg)
    sfin = sfin.reshape(B, PAIRS, RWKV_HD, 2, RWKV_HD).transpose(0, 1, 3, 2, 4).reshape(B, RWKV_HEADS, RWKV_HD, RWKV_HD)
    return y.reshape(B * T, RWKV_W), sfin


def _rwkv_post_kernel(y_ref, bonus_ref, g_ref, gng_ref, gnb_ref, bd_ref, o_ref):
    y = y_ref[...]
    bd = bd_ref[...]
    inv = 1.0 / RWKV_HD
    mean = _dot2(y, bd) * inv
    yc = y - mean
    var = _dot2(yc * yc, bd) * inv
    yn = yc * lax.rsqrt(var + GN_EPS) * gng_ref[...] + gnb_ref[...]
    o_ref[...] = (yn + bonus_ref[...]) * g_ref[...]


def rwkv_post(y, bonus, g, gn_g, gn_b, bd):
    m, W = y.shape
    tm = _pick(m, (256, 128, 64, 32, 16, 8))
    row = pl.BlockSpec((tm, W), lambda i: (i, 0))
    vec = pl.BlockSpec((1, W), lambda i: (0, 0))
    return pl.pallas_call(
        _rwkv_post_kernel,
        grid=(m // tm,),
        in_specs=[row, row, row, vec, vec, pl.BlockSpec((W, W), lambda i: (0, 0))],
        out_specs=row,
        out_shape=jax.ShapeDtypeStruct((m, W), F32),
        compiler_params=_cparams("parallel"),
        name="rwkv_post",
    )(y, bonus, g, gn_g.reshape(1, W), gn_b.reshape(1, W), bd)


def _gelu_tanh(x):
    return 0.5 * x * (1.0 + jnp.tanh(math.sqrt(2.0 / math.pi) * (x + 0.044715 * (x * x * x))))


def _neg_expm1(x):
    series = -x * (1.0 + x * (1.0 / 2.0) * (1.0 + x * (1.0 / 3.0) * (1.0 + x * (1.0 / 4.0) * (1.0 + x * (1.0 / 5.0)
             * (1.0 + x * (1.0 / 6.0))))))
    return jnp.where(x > -0.05, series, 1.0 - jnp.exp(x))


def _lru_kernel(px_ref, pg_ref, c0_ref, h0_ref, cw_ref, cb_ref, wa_ref, ba_ref, wx_ref, bx_ref, lam_ref,
                y_ref, hlast_ref, carry, hcar):
    ti = pl.program_id(1)

    @pl.when(ti == 0)
    def _():
        carry[...] = c0_ref[0]
        hcar[...] = h0_ref[0]

    x = px_ref[...]
    tm = x.shape[0]
    cw = cw_ref[...]
    xc = cb_ref[...] + cw[CONV_W - 1:CONV_W, :] * x
    for j in range(1, CONV_W):
        xc = xc + cw[CONV_W - 1 - j:CONV_W - j, :] * _shifted(carry, x, j)
    carry[...] = x[tm - SUBLANES:, :]
    xcb = xc.astype(BF16)
    r = _sigmoid(_dot(xcb, wa_ref[...].astype(BF16)) + ba_ref[...])
    i = _sigmoid(_dot(xcb, wx_ref[...].astype(BF16)) + bx_ref[...])
    log_a = (-LRU_C * _softplus(-lam_ref[...])) * r
    a = jnp.exp(log_a)
    b = jnp.sqrt(_neg_expm1(2.0 * log_a)) * (i * xc)
    rows = lax.broadcasted_iota(jnp.int32, a.shape, 0)
    d = 1
    while d < tm:
        a_sh = pltpu.roll(a, d, axis=0)
        b_sh = pltpu.roll(b, d, axis=0)
        keep = rows >= d
        b = jnp.where(keep, a * b_sh + b, b)
        a = jnp.where(keep, a * a_sh, a)
        d *= 2
    h = a * hcar[...] + b
    hcar[...] = h[tm - 1:tm, :]
    hlast_ref[0] = h[tm - 1:tm, :]
    y_ref[...] = h * _gelu_tanh(pg_ref[...])


def lru_mix(p_l, c0, h0, B, T, prm):
    W = LRU_W
    tm = _pick(T, (256, 128, 64, 32, 16, 8))
    nt = T // tm
    c0p = jnp.concatenate([jnp.zeros((B, SUBLANES - (CONV_W - 1), W), F32), c0], axis=1)
    h0p = h0.reshape(B, 1, W)
    full = lambda a: pl.BlockSpec(a.shape, lambda b, t: (0,) * a.ndim)
    consts = [prm["conv_w"], prm["conv_b"], prm["wa"], prm["ba"], prm["wx"], prm["bx"], prm["lam"]]
    y, hlast = pl.pallas_call(
        _lru_kernel,
        grid=(B, nt),
        in_specs=[pl.BlockSpec((tm, W), lambda b, t: (b * nt + t, 0)),
                  pl.BlockSpec((tm, W), lambda b, t: (b * nt + t, 1)),
                  pl.BlockSpec((1, SUBLANES, W), lambda b, t: (b, 0, 0)),
                  pl.BlockSpec((1, 1, W), lambda b, t: (b, 0, 0))] + [full(c) for c in consts],
        out_specs=[pl.BlockSpec((tm, W), lambda b, t: (b * nt + t, 0)),
                   pl.BlockSpec((1, 1, W), lambda b, t: (b, 0, 0))],
        out_shape=[jax.ShapeDtypeStruct((B * T, W), F32), jax.ShapeDtypeStruct((B, 1, W), F32)],
        scratch_shapes=[pltpu.VMEM((SUBLANES, W), F32), pltpu.VMEM((1, W), F32)],
        compiler_params=_cparams("arbitrary", "arbitrary"),
        name="lru_mix",
    )(p_l, p_l, c0p, h0p, *consts)
    return y, hlast.reshape(B, W)


def _block_diag(w):
    n, d, e = w.shape
    eye = jnp.eye(n, dtype=w.dtype)
    return (eye[:, None, :, None] * w[:, :, None, :]).reshape(n * d, n * e)


def rec_params(li, rec_w_in, rec_mu, rwkv_w0, rwkv_w_up, rwkv_a0, rwkv_a_up, rwkv_g_up, rwkv_k_k, rwkv_k_a,
               rwkv_r_k, rwkv_gn_g, rwkv_gn_b, lru_conv_w, lru_conv_b, lru_wa, lru_ba, lru_wx, lru_bx, lru_lambda,
               rec_w_out):
    W = RWKV_W
    pad = RWKV_PAD_W - RWKV_SHIFT_W
    hi = jnp.arange(W) // RWKV_HD
    zl = jnp.zeros((LORA_DECAY, W), F32)
    return dict(
        w_rw=jnp.pad(rec_w_in[li][:, :RWKV_SHIFT_W], ((0, 0), (0, pad))),
        w_l=rec_w_in[li][:, RWKV_SHIFT_W:],
        mu=jnp.pad(rec_mu[li], (0, pad)).reshape(1, RWKV_PAD_W),
        w0=rwkv_w0[li].reshape(1, W), a0=rwkv_a0[li].reshape(1, W),
        wlora=jnp.concatenate([rwkv_w_up[li], zl], axis=0),
        alora=jnp.concatenate([zl, rwkv_a_up[li]], axis=0),
        gup=rwkv_g_up[li],
        k_k=rwkv_k_k[li].reshape(1, W), k_a=rwkv_k_a[li].reshape(1, W), r_k=rwkv_r_k[li].reshape(1, W),
        bd=(hi[:, None] == hi[None, :]).astype(BF16),
        gn_g=rwkv_gn_g[li], gn_b=rwkv_gn_b[li],
        conv_w=lru_conv_w[li], conv_b=lru_conv_b[li].reshape(1, LRU_W),
        wa=_block_diag(lru_wa[li]), ba=lru_ba[li].reshape(1, LRU_W),
        wx=_block_diag(lru_wx[li]), bx=lru_bx[li].reshape(1, LRU_W),
        lam=lru_lambda[li].reshape(1, LRU_W),
        w_out=rec_w_out[li],
    )


def rec_mixer(x2d, B, T, S0, sh0, h0, c0, prm):
    p_rw = mm(x2d, prm["w_rw"])
    p_l = mm(x2d, prm["w_l"])
    sh0p = jnp.pad(sh0, ((0, 0), (0, RWKV_PAD_W - RWKV_SHIFT_W)))
    r, w, k, kn, bv, v, wr, br, kr, g, bonus = rwkv_prep(p_rw, sh0p, B, T, prm)
    y_raw, s_fin = rwkv_scan([r, w, k, kn, bv, v, wr, br, kr], S0, B, T)
    y_rw = rwkv_post(y_raw, bonus, g, prm["gn_g"], prm["gn_b"], prm["bd"])
    y_lru, h_last = lru_mix(p_l, c0, h0, B, T, prm)
    y = mm_cat2(y_rw, y_lru, prm["w_out"])
    sh = p_rw.reshape(B, T, RWKV_PAD_W)[:, -1, :RWKV_SHIFT_W]
    px = p_l.reshape(B, T, 2 * LRU_W)[:, :, :LRU_W]
    cbuf = jnp.concatenate([c0, px], axis=1)[:, -(CONV_W - 1):] if T < CONV_W - 1 else px[:, -(CONV_W - 1):]
    return y, s_fin, sh, h_last, cbuf


def _dot_nt(a, b):
    return lax.dot_general(a, b, (((1,), (1,)), ((), ())), preferred_element_type=F32)


def _compress_kernel(slab_ref, w_ref, o_ref, *, n_pieces):
    acc = None
    for l in range(CMP_STRIDE):
        x = slab_ref[pl.ds(l, n_pieces, stride=CMP_STRIDE), :].astype(BF16)
        d = _dot(x, w_ref[l].astype(BF16))
        acc = d if acc is None else acc + d
    out = acc[:, :HEAD_DIM] + pltpu.roll(acc[:, HEAD_DIM:], n_pieces - 1, axis=0)
    o_ref[0:n_pieces, :] = out
    pad = o_ref.shape[0] - n_pieces
    if pad:
        o_ref[n_pieces:, :] = jnp.zeros((pad, HEAD_DIM), F32)


def nsa_compress(kv2d, col0, B, t_kv, wcat, ncp):
    n_pieces = t_kv // CMP_STRIDE
    return pl.pallas_call(
        functools.partial(_compress_kernel, n_pieces=n_pieces),
        grid=(B, 2, N_KV),
        in_specs=[pl.BlockSpec((t_kv, HEAD_DIM), lambda b, c, g: (b, col0 + c * N_KV + g)),
                  pl.BlockSpec((None, CMP_STRIDE, HEAD_DIM, 2 * HEAD_DIM), lambda b, c, g: (c, 0, 0, 0))],
        out_specs=pl.BlockSpec((None, None, None, ncp, HEAD_DIM), lambda b, c, g: (b, c, g, 0, 0)),
        out_shape=jax.ShapeDtypeStruct((B, 2, N_KV, ncp, HEAD_DIM), F32),
        compiler_params=_cparams("parallel", "parallel", "parallel"),
        name="nsa_compress",
    )(kv2d, wcat)


def _online_softmax_step(q_bf, k_bf, v_bf, bias, mask, m, l, acc):
    s = _dot_nt(q_bf, k_bf) * SCALE - bias
    s = jnp.where(mask, s, NEG)
    m_new = jnp.maximum(m, jnp.max(s, axis=-1, keepdims=True))
    alpha = jnp.exp(m - m_new)
    p = jnp.where(mask, jnp.exp(s - m_new), 0.0)
    l = alpha * l + jnp.sum(p, axis=-1, keepdims=True)
    acc = alpha * acc + _dot(p.astype(BF16), v_bf)
    return m_new, l, acc


def _nsa_kernel(q_ref, kc_ref, vc_ref, ks_ref, vs_ref, kw_ref, vw_ref, gate_ref, gb_ref, ov_ref, o_ref, *,
                tq, q_off, nc_real, ns_real, tk, n_kv_static, w_off, w_len, w_follows_q):
    g = pl.program_id(1)
    qi = pl.program_id(2)
    q0 = qi * tq
    ncp = kc_ref.shape[0]
    nsp = ov_ref.shape[1]
    qpos_col = q_off + q0 + lax.broadcasted_iota(jnp.int32, (tq, 1), 0)
    q_all = q_ref[...]
    q_bf = [q_all[:, r * HEAD_DIM:(r + 1) * HEAD_DIM].astype(BF16) for r in range(GQA_R)]
    head_f = (g * GQA_R).astype(F32)
    slopes = [jnp.exp2(-0.5 * (jnp.full((1, 1), 1.0 + r, F32) + head_f)) for r in range(GQA_R)]

    n_idx = lax.broadcasted_iota(jnp.int32, (tq, ncp), 1)
    dist_c = qpos_col - (n_idx * CMP_STRIDE + (CMP_BLOCK - 1))
    ok_c = (dist_c >= 0) & (n_idx < nc_real)
    dist_cf = dist_c.astype(F32)
    kc_bf = kc_ref[...].astype(BF16)
    vc_bf = vc_ref[...].astype(BF16)
    o_c = []
    p_sum = None
    for r in range(GQA_R):
        s = _dot_nt(q_bf[r], kc_bf) * SCALE - slopes[r] * dist_cf
        s = jnp.where(ok_c, s, NEG)
        m = jnp.max(s, axis=-1, keepdims=True)
        e = jnp.where(ok_c, jnp.exp(s - m), 0.0)
        p = e / jnp.maximum(jnp.sum(e, axis=-1, keepdims=True), 1e-30)
        o_c.append(_dot(p.astype(BF16), vc_bf))
        p_sum = p if p_sum is None else p_sum + p
    imp = _dot2(p_sum, ov_ref[...])
    s_idx = lax.broadcasted_iota(jnp.int32, (tq, nsp), 1)
    cur = lax.shift_right_logical(qpos_col, int(math.log2(SLC_BLOCK)))
    forced = (s_idx == 0) | (s_idx == cur) | (s_idx == cur - 1)
    causal_blk = (s_idx * SLC_BLOCK <= qpos_col) & (s_idx < ns_real)
    score = jnp.where(causal_blk, imp + jnp.where(forced, BIG, 0.0), NEG)
    rank = jnp.zeros((tq, nsp), jnp.int32)
    for j in range(ns_real):
        col = score[:, j:j + 1]
        ahead = (col > score) | ((col == score) & (s_idx > j))
        rank = rank + jnp.where(ahead, 1, 0)
    sel = (rank < min(TOP_N, ns_real)) & (score > 0.5 * NEG)
    sel_bf = jnp.where(sel, 1.0, 0.0).astype(BF16)

    def init():
        return tuple((jnp.full((tq, 1), NEG, F32), jnp.zeros((tq, 1), F32), jnp.zeros((tq, HEAD_DIM), F32))
                     for _ in range(GQA_R))

    blk_per_tile = tk // SLC_BLOCK
    ex_row = lax.broadcasted_iota(jnp.int32, (nsp, tk), 0)
    ex_col = lax.shift_right_logical(lax.broadcasted_iota(jnp.int32, (nsp, tk), 1), int(math.log2(SLC_BLOCK)))
    key_i = lax.broadcasted_iota(jnp.int32, (tq, tk), 1)

    def slc_body(kt, carry):
        base = pl.multiple_of(kt * tk, tk)
        expand = jnp.where(ex_row == ex_col + kt * blk_per_tile, 1.0, 0.0).astype(BF16)
        selexp = _dot(sel_bf, expand)
        dist = qpos_col - (key_i + base)
        mask = (dist >= 0) & (selexp > 0.5)
        distf = dist.astype(F32)
        k_bf = ks_ref[pl.ds(base, tk), :].astype(BF16)
        v_bf = vs_ref[pl.ds(base, tk), :].astype(BF16)
        return tuple(_online_softmax_step(q_bf[r], k_bf, v_bf, slopes[r] * distf, mask, *carry[r])
                     for r in range(GQA_R))

    if n_kv_static is None:
        n_kv = (q0 + tq + tk - 1) // tk
    else:
        n_kv = n_kv_static
    st_s = lax.fori_loop(0, n_kv, slc_body, init())

    if w_follows_q:
        w_base = pl.multiple_of(jnp.maximum(q0 + tq - w_len, 0), SUBLANES)
    else:
        w_base = 0
    dist_w = qpos_col - (w_off + w_base + lax.broadcasted_iota(jnp.int32, (tq, w_len), 1))
    ok_w = jnp.where(dist_w >= 0, dist_w, WINDOW) < WINDOW
    dist_wf = dist_w.astype(F32)
    kw_bf = kw_ref[pl.ds(w_base, w_len), :].astype(BF16)
    vw_bf = vw_ref[pl.ds(w_base, w_len), :].astype(BF16)
    st_w = tuple(_online_softmax_step(q_bf[r], kw_bf, vw_bf, slopes[r] * dist_wf, ok_w, *init()[r])
                 for r in range(GQA_R))

    gates = _sigmoid(gate_ref[...] + gb_ref[...])
    lane = lax.broadcasted_iota(jnp.int32, gates.shape, 1)
    for r in range(GQA_R):
        head = g * GQA_R + r
        gcol = lambda br: jnp.sum(jnp.where(lane == br * N_HEADS + head, gates, 0.0), axis=-1, keepdims=True)
        o_s = st_s[r][2] / jnp.maximum(st_s[r][1], 1e-30)
        o_w = st_w[r][2] / jnp.maximum(st_w[r][1], 1e-30)
        o_ref[:, r * HEAD_DIM:(r + 1) * HEAD_DIM] = gcol(0) * o_c[r] + gcol(1) * o_s + gcol(2) * o_w


def nsa_attention(q2d, kcvc, kv2d, kv_col0, win2d, win_col0, gates, gate_b, overlap, *, B, t_q, t_kv, t_win, tq,
                  q_off, nc_real, ns_real, tk, n_kv_static, w_off, w_len, w_follows_q):
    nq = t_q // tq
    ncp = kcvc.shape[3]
    nsp = overlap.shape[1]
    qw = GQA_R * HEAD_DIM
    slab = lambda arr_t, col: pl.BlockSpec((arr_t, HEAD_DIM), col)
    kern = functools.partial(_nsa_kernel, tq=tq, q_off=q_off, nc_real=nc_real, ns_real=ns_real, tk=tk,
                             n_kv_static=n_kv_static, w_off=w_off, w_len=w_len, w_follows_q=w_follows_q)
    return pl.pallas_call(
        kern,
        grid=(B, N_KV, nq),
        in_specs=[pl.BlockSpec((tq, qw), lambda b, g, i: (b * nq + i, g)),
                  pl.BlockSpec((None, None, None, ncp, HEAD_DIM), lambda b, g, i: (b, 0, g, 0, 0)),
                  pl.BlockSpec((None, None, None, ncp, HEAD_DIM), lambda b, g, i: (b, 1, g, 0, 0)),
                  slab(t_kv, lambda b, g, i: (b, kv_col0 + 2 * N_KV + g)),
                  slab(t_kv, lambda b, g, i: (b, kv_col0 + 3 * N_KV + g)),
                  slab(t_win, lambda b, g, i: (b, win_col0 + g)),
                  slab(t_win, lambda b, g, i: (b, win_col0 + N_KV + g)),
                  pl.BlockSpec((tq, LANES), lambda b, g, i: (b * nq + i, 0)),
                  pl.BlockSpec((1, LANES), lambda b, g, i: (0, 0)),
                  pl.BlockSpec((ncp, nsp), lambda b, g, i: (0, 0))],
        out_specs=pl.BlockSpec((tq, qw), lambda b, g, i: (b * nq + i, g)),
        out_shape=jax.ShapeDtypeStruct((B * t_q, N_HEADS * HEAD_DIM), F32),
        compiler_params=_cparams("parallel", "parallel", "arbitrary"),
        name="nsa_attention",
    )(q2d, kcvc, kcvc, kv2d, kv2d, win2d, win2d, gates, gate_b, overlap)


def _overlap_matrix(ncp, nsp, nc_real, ns_real):
    n = jnp.arange(ncp)[:, None]
    s = jnp.arange(nsp)[None, :]
    c0 = n * CMP_STRIDE
    s0 = s * SLC_BLOCK
    ov = (c0 < s0 + SLC_BLOCK) & (c0 + CMP_BLOCK > s0) & (n < nc_real) & (s < ns_real)
    return ov.astype(BF16)


def _gather_pages_kernel(pt_ref, page_ref, tail_ref, o_ref):
    j = pl.program_id(1)
    n_pages = pl.num_programs(1) - 1

    @pl.when(j < n_pages)
    def _():
        o_ref[...] = page_ref[...]

    @pl.when(j == n_pages)
    def _():
        o_ref[...] = tail_ref[...]


def gather_pages(cache5, page_table, tail, li):
    B, n_pages = page_table.shape
    page, row_w = cache5.shape[1], cache5.shape[4]
    grid_spec = pltpu.PrefetchScalarGridSpec(
        num_scalar_prefetch=1,
        grid=(B, n_pages + 1),
        in_specs=[pl.BlockSpec((None, page, None, None, row_w),
                               lambda b, j, pt: (pt[b, jnp.minimum(j, n_pages - 1)], 0, li, 0, 0)),
                  pl.BlockSpec((None, page, row_w), lambda b, j, pt: (b, 0, 0))],
        out_specs=pl.BlockSpec((None, page, row_w), lambda b, j, pt: (b, j, 0)),
    )
    return pl.pallas_call(
        _gather_pages_kernel,
        grid_spec=grid_spec,
        out_shape=jax.ShapeDtypeStruct((B, (n_pages + 1) * page, row_w), F32),
        compiler_params=_cparams("parallel", "arbitrary"),
        name="gather_pages",
    )(page_table, cache5, tail)


def attn_params(li, attn_w_in, attn_gate_b, w_cmp_k, w_cmp_v, attn_w_out):
    qkv_w = N_HEADS * HEAD_DIM + 6 * N_KV * HEAD_DIM
    ng = 3 * N_HEADS
    half = lambda w: jnp.concatenate([w[:CMP_STRIDE], w[CMP_STRIDE:]], axis=-1)
    return dict(
        w_main=attn_w_in[li][:, :qkv_w],
        w_gate=jnp.pad(attn_w_in[li][:, qkv_w:], ((0, 0), (0, LANES - ng))),
        gate_b=jnp.pad(attn_gate_b[li], (0, LANES - ng)).reshape(1, LANES),
        wcat=jnp.stack([half(w_cmp_k[li]), half(w_cmp_v[li])]),
        w_out=attn_w_out[li],
    )


def attn_prompt(x2d, B, T, prm):
    p = mm(x2d, prm["w_main"])
    gates = mm(x2d, prm["w_gate"])
    q_blocks = N_HEADS
    n_pieces = T // CMP_STRIDE
    nc_real = n_pieces - CMP_BLOCK // CMP_STRIDE + 1
    ns_real = T // SLC_BLOCK
    ncp = -(-n_pieces // LANES) * LANES
    nsp = -(-ns_real // LANES) * LANES
    kcvc = nsa_compress(p, q_blocks, B, T, prm["wcat"], ncp)
    ov = _overlap_matrix(ncp, nsp, nc_real, ns_real)
    tq = min(128, T)
    o = nsa_attention(p, kcvc, p, q_blocks, p, q_blocks + 4 * N_KV, gates, prm["gate_b"], ov,
                      B=B, t_q=T, t_kv=T, t_win=T, tq=tq, q_off=0, nc_real=nc_real, ns_real=ns_real,
                      tk=min(512, T), n_kv_static=None, w_off=0, w_len=min(WINDOW + tq, T), w_follows_q=True)
    y = mm(o, prm["w_out"])
    kv = p.reshape(B, T, -1)[:, :, N_HEADS * HEAD_DIM:].reshape(B, T, 6, N_KV, HEAD_DIM)
    w_buf = min(WINDOW, T)
    win = jnp.pad(kv[:, :, 4:], ((0, 0), (WINDOW, 0), (0, 0), (0, 0), (0, 0)))[:, -w_buf:]
    return y, kv[:, :, :4], win


def attn_sample(x2d, B, T, prm, cache_nsa_kv, cache_win, page_table, li):
    n_pool, page = cache_nsa_kv.shape[:2]
    n_attn = cache_nsa_kv.shape[2]
    n_pages = page_table.shape[1]
    past_len = n_pages * page
    row_w = 4 * N_KV * HEAD_DIM
    p = mm(x2d, prm["w_main"])
    gates = mm(x2d, prm["w_gate"])
    kv_new = p[:, N_HEADS * HEAD_DIM:].reshape(B, T, 6, N_KV, HEAD_DIM)
    tail = jnp.pad(kv_new[:, :, :4].reshape(B, T, row_w), ((0, 0), (0, page - T), (0, 0)))
    full = gather_pages(cache_nsa_kv.reshape(n_pool, page, n_attn, 1, row_w), page_table, tail, li)
    t_kv = (n_pages + 1) * page
    full2d = full.reshape(B * t_kv, row_w)
    L = past_len + T
    l_pad = -(-L // SLC_BLOCK) * SLC_BLOCK
    nc_real = l_pad // CMP_STRIDE - CMP_BLOCK // CMP_STRIDE + 1
    ns_real = l_pad // SLC_BLOCK
    n_pieces = t_kv // CMP_STRIDE
    ncp = -(-n_pieces // LANES) * LANES
    nsp = -(-ns_real // LANES) * LANES
    kcvc = nsa_compress(full2d, 0, B, t_kv, prm["wcat"], ncp)
    ov = _overlap_matrix(ncp, nsp, nc_real, ns_real)
    w_buf = cache_win.shape[1]
    win = jnp.concatenate([cache_win, kv_new[:, :, 4:]], axis=1)
    t_win = -(-(w_buf + T) // LANES) * LANES
    win2d = jnp.pad(win.reshape(B, w_buf + T, 2 * N_KV * HEAD_DIM), ((0, 0), (0, t_win - (w_buf + T)), (0, 0)))
    win2d = win2d.reshape(B * t_win, 2 * N_KV * HEAD_DIM)
    n_kv = 3
    assert t_kv % (n_kv * LANES) == 0
    o = nsa_attention(p, kcvc, full2d, 0, win2d, 0, gates, prm["gate_b"], ov,
                      B=B, t_q=T, t_kv=t_kv, t_win=t_win, tq=T, q_off=past_len, nc_real=nc_real, ns_real=ns_real,
                      tk=t_kv // n_kv, n_kv_static=n_kv, w_off=past_len - w_buf, w_len=t_win, w_follows_q=False)
    y = mm(o, prm["w_out"])
    return y, kv_new[:, :, :4], win[:, -w_buf:]


MOE_TM = 256


ROUTER_TN = 256


def _first_argmax(vals):
    best, idx = vals[0], jnp.zeros(vals[0].shape, jnp.int32)
    for i in range(1, len(vals)):
        better = vals[i] > best
        best = jnp.where(better, vals[i], best)
        idx = jnp.where(better, i, idx)
    return best, idx


def _pick_by(idx, vals):
    out = vals[-1]
    for i in range(len(vals) - 2, -1, -1):
        out = jnp.where(idx == i, vals[i], out)
    return out


def _router_kernel(x_ref, wt_ref, bias_ref, e_ref, w_ref, rank_ref, cnt_ref, base, *, n_tok):
    i = pl.program_id(0)
    tn = x_ref.shape[0]

    @pl.when(i == 0)
    def _():
        base[...] = jnp.zeros(base.shape, F32)

    xh, xl = _split2(x_ref[...])
    wh, wl = _split2(wt_ref[...])
    logits = _dot_nt(wh, xh) + _dot_nt(wh, xl) + _dot_nt(wl, xh)
    aff = _sigmoid(logits)
    sel = aff + bias_ref[...]
    s = [sel[e:e + 1, :] for e in range(N_EXPERTS)]
    a = [aff[e:e + 1, :] for e in range(N_EXPERTS)]
    G = EXPERTS_PER_GROUP
    gscore = []
    for g in range(N_GROUPS):
        v0, v1, v2, v3 = s[G * g:G * g + G]
        hi1, lo1 = jnp.maximum(v0, v1), jnp.minimum(v0, v1)
        hi2, lo2 = jnp.maximum(v2, v3), jnp.minimum(v2, v3)
        gscore.append(jnp.maximum(hi1, hi2) + jnp.maximum(jnp.minimum(hi1, hi2), jnp.maximum(lo1, lo2)))
    _, gi = _first_argmax(gscore)
    sg = [_pick_by(gi, [s[G * g + j] for g in range(N_GROUPS)]) for j in range(G)]
    ag = [_pick_by(gi, [a[G * g + j] for g in range(N_GROUPS)]) for j in range(G)]
    _, l0 = _first_argmax(sg)
    _, l1 = _first_argmax([jnp.where(l0 == j, -jnp.inf, sg[j]) for j in range(G)])
    w0, w1 = _pick_by(l0, ag), _pick_by(l1, ag)
    wsum = w0 + w1
    e0, e1 = gi * G + l0, gi * G + l1
    e_ref[...] = jnp.concatenate([e0, e1], axis=0)
    w_ref[...] = jnp.concatenate([w0 / wsum, w1 / wsum], axis=0)
    tok = i * tn + lax.broadcasted_iota(jnp.int32, (N_EXPERTS, tn), 1)
    eid = lax.broadcasted_iota(jnp.int32, (N_EXPERTS, tn), 0)
    valid = tok < n_tok
    oh0 = jnp.where((eid == e0) & valid, 1.0, 0.0)
    oh1 = jnp.where((eid == e1) & valid, 1.0, 0.0)
    cnt = oh0 + oh1
    r_i = lax.broadcasted_iota(jnp.int32, (tn, tn), 0)
    c_i = lax.broadcasted_iota(jnp.int32, (tn, tn), 1)
    upper = jnp.where(r_i <= c_i, 1.0, 0.0).astype(BF16)
    excl = _dot(cnt.astype(BF16), upper) - cnt + base[...]
    rank_ref[...] = jnp.concatenate([jnp.sum(oh0 * excl, axis=0, keepdims=True),
                                     jnp.sum(oh1 * excl, axis=0, keepdims=True)], axis=0).astype(jnp.int32)
    base[...] = base[...] + jnp.sum(cnt, axis=1, keepdims=True)
    cnt_ref[...] = jnp.broadcast_to(base[...], cnt_ref.shape).astype(jnp.int32)


def router(x, w_t, bias_col):
    n, d = x.shape
    tn = ROUTER_TN
    nt = -(-n // tn)
    n_pad = nt * tn
    two = lambda dt: jax.ShapeDtypeStruct((TOP_K, n_pad), dt)
    out_row = pl.BlockSpec((TOP_K, tn), lambda i: (0, i))
    e, w, rank, cnt = pl.pallas_call(
        functools.partial(_router_kernel, n_tok=n),
        grid=(nt,),
        in_specs=[pl.BlockSpec((tn, d), lambda i: (i, 0)),
                  pl.BlockSpec((N_EXPERTS, d), lambda i: (0, 0)),
                  pl.BlockSpec((N_EXPERTS, 1), lambda i: (0, 0))],
        out_specs=[out_row, out_row, out_row, pl.BlockSpec((N_EXPERTS, LANES), lambda i: (0, 0))],
        out_shape=[two(jnp.int32), two(F32), two(jnp.int32), jax.ShapeDtypeStruct((N_EXPERTS, LANES), jnp.int32)],
        scratch_shapes=[pltpu.VMEM((N_EXPERTS, 1), F32)],
        compiler_params=_cparams("arbitrary"),
        name="router",
    )(x, w_t, bias_col)
    return e[:, :n], w[:, :n], rank[:, :n], cnt[:, 0]


def _expert_kernel(te_ref, x_ref, rw_ref, wg_ref, wu_ref, wd_ref, o_ref):
    xb = x_ref[...].astype(BF16)
    hg = _dot(xb, wg_ref[...])
    hu = _dot(xb, wu_ref[...])
    h = (hg * _sigmoid(hg)) * hu * rw_ref[...]
    o_ref[...] = _dot(h.astype(BF16), wd_ref[...])


def expert_ffn(x_sorted, row_w, tile_expert, wg, wu, wd):
    a_pad, d = x_sorted.shape
    f = wg.shape[2]
    n_tiles = a_pad // MOE_TM
    grid_spec = pltpu.PrefetchScalarGridSpec(
        num_scalar_prefetch=1,
        grid=(n_tiles,),
        in_specs=[pl.BlockSpec((MOE_TM, d), lambda t, te: (t, 0)),
                  pl.BlockSpec((MOE_TM, 1), lambda t, te: (t, 0)),
                  pl.BlockSpec((None, d, f), lambda t, te: (te[t], 0, 0)),
                  pl.BlockSpec((None, d, f), lambda t, te: (te[t], 0, 0)),
                  pl.BlockSpec((None, f, d), lambda t, te: (te[t], 0, 0))],
        out_specs=pl.BlockSpec((MOE_TM, d), lambda t, te: (t, 0)),
    )
    return pl.pallas_call(
        _expert_kernel,
        grid_spec=grid_spec,
        out_shape=jax.ShapeDtypeStruct((a_pad, d), F32),
        compiler_params=_cparams("arbitrary"),
        name="expert_ffn",
    )(tile_expert, x_sorted, row_w, wg, wu, wd)


def moe(x, router_wt, r_bias_col, w_g, w_u, w_d):
    n, d = x.shape
    e, w_sel, rank, counts = router(x, router_wt, r_bias_col)
    n_asg = n * TOP_K
    gsz = (counts + MOE_TM - 1) // MOE_TM * MOE_TM
    p_end = jnp.cumsum(gsz)
    p_start = p_end - gsz
    onehot = (e[:, :, None] == jnp.arange(N_EXPERTS, dtype=jnp.int32)[None, None, :]).astype(jnp.int32)
    dest = jnp.sum(onehot * p_start[None, None, :], axis=2) + rank
    a_pad = -(-(n_asg + N_EXPERTS * (MOE_TM - 1)) // MOE_TM) * MOE_TM
    tok = jnp.broadcast_to(jnp.arange(n, dtype=jnp.int32)[None, :], (TOP_K, n))
    src_tok = jnp.zeros((a_pad,), jnp.int32).at[dest.reshape(n_asg)].set(tok.reshape(n_asg))
    row_w = jnp.zeros((a_pad,), F32).at[dest.reshape(n_asg)].set(w_sel.reshape(n_asg))
    tile_start = jnp.arange(a_pad // MOE_TM, dtype=jnp.int32) * MOE_TM
    tile_expert = jnp.minimum(jnp.sum((p_end[None, :] <= tile_start[:, None]).astype(jnp.int32), axis=1),
                              N_EXPERTS - 1).astype(jnp.int32)
    ys = expert_ffn(x[src_tok], row_w.reshape(a_pad, 1), tile_expert,
                    w_g.astype(BF16), w_u.astype(BF16), w_d.astype(BF16))
    return ys[dest[0]] + ys[dest[1]]


def kernel(x_prompt, x_sample, cache_nsa_kv, cache_win_kv, state_rwkv, state_rwkv_shift, state_lru_h, state_lru_conv,
           page_table, ln1_g, ln1_b, ln2_g, ln2_b, rec_w_in, rec_mu, rwkv_w0, rwkv_w_up, rwkv_a0, rwkv_a_up, rwkv_g_up,
           rwkv_k_k, rwkv_k_a, rwkv_r_k, rwkv_gn_g, rwkv_gn_b, lru_conv_w, lru_conv_b, lru_wa, lru_ba, lru_wx, lru_bx,
           lru_lambda, rec_w_out, attn_w_in, attn_gate_b, w_cmp_k, w_cmp_v, attn_w_out, moe_w_router, moe_router_bias,
           moe_w_gate, moe_w_up, moe_w_down):
    Bp, Tp, D = x_prompt.shape
    Bs, Ts, _ = x_sample.shape
    n_p, n_s = Bp * Tp, Bs * Ts
    rec_args = (rec_w_in, rec_mu, rwkv_w0, rwkv_w_up, rwkv_a0, rwkv_a_up, rwkv_g_up, rwkv_k_k, rwkv_k_a,
                rwkv_r_k, rwkv_gn_g, rwkv_gn_b, lru_conv_w, lru_conv_b, lru_wa, lru_ba, lru_wx, lru_bx, lru_lambda,
                rec_w_out)
    x = jnp.concatenate([x_prompt.reshape(n_p, D), x_sample.reshape(n_s, D)], axis=0)
    router_w = moe_w_router.T
    moe_router_bias = moe_router_bias.reshape(N_EXPERTS, 1)
    nsa_p, nsa_s, win_p, win_s = [], [], [], []
    rs_p, rs_s, rsh_p, rsh_s, lh_p, lh_s, lc_p, lc_s = [], [], [], [], [], [], [], []
    for layer in range(DEPTH):
        li = layer // 2
        xp, xs = x[:n_p], x[n_p:]
        if layer % 2 == 0:
            prm = rec_params(li, *rec_args)
            yp, S, sh, h, cb = rec_mixer(xp, Bp, Tp, jnp.zeros((Bp, RWKV_HEADS, RWKV_HD, RWKV_HD), F32),
                                         jnp.zeros((Bp, RWKV_SHIFT_W), F32), jnp.zeros((Bp, LRU_W), F32),
                                         jnp.zeros((Bp, CONV_W - 1, LRU_W), F32), prm)
            ys, S2, sh2, h2, cb2 = rec_mixer(xs, Bs, Ts, state_rwkv[:, li], state_rwkv_shift[:, li],
                                             state_lru_h[:, li], state_lru_conv[:, li], prm)
            rs_p.append(S); rs_s.append(S2); rsh_p.append(sh); rsh_s.append(sh2)
            lh_p.append(h); lh_s.append(h2); lc_p.append(cb); lc_s.append(cb2)
        else:
            prm = attn_params(li, attn_w_in, attn_gate_b, w_cmp_k, w_cmp_v, attn_w_out)
            yp, rows, wb = attn_prompt(xp, Bp, Tp, prm)
            ys, rows2, wb2 = attn_sample(xs, Bs, Ts, prm, cache_nsa_kv, cache_win_kv[:, :, li], page_table, li)
            nsa_p.append(rows); nsa_s.append(rows2); win_p.append(wb); win_s.append(wb2)
        x = ln_res(x, jnp.concatenate([yp, ys], axis=0), ln1_g[layer], ln1_b[layer])
        y_moe = moe(x, router_w, moe_router_bias, moe_w_gate[layer], moe_w_up[layer], moe_w_down[layer])
        x = ln_res(x, y_moe, ln2_g[layer], ln2_b[layer])
    return (x[:n_p].reshape(Bp, Tp, D), x[n_p:].reshape(Bs, Ts, D),
            jnp.stack(nsa_p, axis=2), jnp.stack(nsa_s, axis=2), jnp.stack(win_p, axis=2), jnp.stack(win_s, axis=2),
            jnp.stack(rs_p, axis=1), jnp.stack(rs_s, axis=1), jnp.stack(rsh_p, axis=1), jnp.stack(rsh_s, axis=1),
            jnp.stack(lh_p, axis=1), jnp.stack(lh_s, axis=1), jnp.stack(lc_p, axis=1), jnp.stack(lc_s, axis=1))
```

```python
import functools
import math

import jax
import jax.numpy as jnp
from jax import lax
from jax.experimental import pallas as pl
from jax.experimental.pallas import tpu as pltpu

F32 = jnp.float32
BF16 = jnp.bfloat16

D_MODEL = 2048
DEPTH = 4
RWKV_HEADS = 16
RWKV_HD = 64
RWKV_W = RWKV_HEADS * RWKV_HD
LORA_DECAY = 64
LORA_A = 64
LORA_GATE = 128
RWKV_SHIFT_W = 3 * RWKV_W + LORA_DECAY + LORA_A + LORA_GATE
GN_EPS = 64e-5
LRU_W = D_MODEL - RWKV_W
LRU_BLOCKS = 16
CONV_W = 4
LRU_C = 8.0
N_HEADS = 16
HEAD_DIM = 128
N_KV = 4
GQA_R = N_HEADS // N_KV
CMP_BLOCK = 32
CMP_STRIDE = 16
SLC_BLOCK = 64
TOP_N = 16
WINDOW = 512
SCALE = HEAD_DIM ** -0.5
N_EXPERTS = 16
N_GROUPS = 4
EXPERTS_PER_GROUP = N_EXPERTS // N_GROUPS
TOP_K = 2
D_EXPERT = 1024
LN_EPS = 1e-5
DN_ALPHA = (2 * DEPTH) ** 0.25
NEG = -1e30
BIG = 1e4

LANES = 128
SUBLANES = 8
VMEM_LIMIT_BYTES = 48 * 1024 * 1024

RWKV_PAD_W = 3584
PAIRS = RWKV_HEADS // 2


def _cparams(*sem):
    return pltpu.CompilerParams(dimension_semantics=sem, vmem_limit_bytes=VMEM_LIMIT_BYTES)


def _split2(x):
    hi = x.astype(BF16)
    lo = (x - hi.astype(F32)).astype(BF16)
    return hi, lo


def _dot(a, b):
    return jnp.dot(a, b, preferred_element_type=F32)


def _dot2(x, w_bf):
    hi, lo = _split2(x)
    return _dot(hi, w_bf) + _dot(lo, w_bf)


def _sigmoid(x):
    return 1.0 / (1.0 + jnp.exp(-x))


def _softplus(x):
    return jnp.maximum(x, 0.0) + jnp.log(1.0 + jnp.exp(-jnp.abs(x)))


def _mdot(a, b, precise):
    return _dot3(a, b) if precise else _dot(a.astype(BF16), b.astype(BF16))


def _mdot_nt(a, b, precise):
    return _dot3_nt(a, b) if precise else _dot_nt(a.astype(BF16), b.astype(BF16))


def _mm_kernel(x_ref, w_ref, o_ref, *, precise):
    o_ref[...] = _mdot(x_ref[...], w_ref[...], precise)


def _pick(n, cands):
    for c in cands:
        if n % c == 0:
            return c
    return n


def mm(x, w, precise=False):
    m, k = x.shape
    n = w.shape[1]
    tm = _pick(m, (512, 256, 128, 64, 32, 16, 8))
    tn = _pick(n, (512, 256, 128) if precise else (1024, 768, 512, 256, 128))
    return pl.pallas_call(
        functools.partial(_mm_kernel, precise=precise),
        grid=(n // tn, m // tm),
        in_specs=[pl.BlockSpec((tm, k), lambda j, i: (i, 0)),
                  pl.BlockSpec((k, tn), lambda j, i: (0, j))],
        out_specs=pl.BlockSpec((tm, tn), lambda j, i: (i, j)),
        out_shape=jax.ShapeDtypeStruct((m, n), F32),
        compiler_params=_cparams("parallel", "parallel"),
        name="mm",
    )(x, w)


def _mm2_kernel(x1_ref, x2_ref, w1_ref, w2_ref, o_ref, *, precise):
    o_ref[...] = _mdot(x1_ref[...], w1_ref[...], precise) + _mdot(x2_ref[...], w2_ref[...], precise)


def mm_cat2(x1, x2, w, precise=False):
    m, k1 = x1.shape
    k2 = x2.shape[1]
    assert k1 == k2 and w.shape[0] == k1 + k2
    n = w.shape[1]
    tm = _pick(m, (512, 256, 128, 64, 32, 16, 8))
    tn = _pick(n, (512, 256, 128) if precise else (1024, 512, 256, 128))
    return pl.pallas_call(
        functools.partial(_mm2_kernel, precise=precise),
        grid=(n // tn, m // tm),
        in_specs=[pl.BlockSpec((tm, k1), lambda j, i: (i, 0)),
                  pl.BlockSpec((tm, k2), lambda j, i: (i, 0)),
                  pl.BlockSpec((k1, tn), lambda j, i: (0, j)),
                  pl.BlockSpec((k2, tn), lambda j, i: (1, j))],
        out_specs=pl.BlockSpec((tm, tn), lambda j, i: (i, j)),
        out_shape=jax.ShapeDtypeStruct((m, n), F32),
        compiler_params=_cparams("parallel", "parallel"),
        name="mm_cat2",
    )(x1, x2, w, w)


def _ln_res_kernel(x_ref, y_ref, g_ref, b_ref, o_ref):
    z = DN_ALPHA * x_ref[...] + y_ref[...]
    mu = jnp.mean(z, axis=-1, keepdims=True)
    zc = z - mu
    var = jnp.mean(zc * zc, axis=-1, keepdims=True)
    o_ref[...] = zc * lax.rsqrt(var + LN_EPS) * g_ref[...] + b_ref[...]


def ln_res(x, y, g, b):
    m, d = x.shape
    tm = _pick(m, (256, 192, 128, 64, 32, 16, 8))
    row = pl.BlockSpec((tm, d), lambda i: (i, 0))
    vec = pl.BlockSpec((1, d), lambda i: (0, 0))
    return pl.pallas_call(
        _ln_res_kernel,
        grid=(m // tm,),
        in_specs=[row, row, vec, vec],
        out_specs=row,
        out_shape=jax.ShapeDtypeStruct((m, d), F32),
        compiler_params=_cparams("parallel"),
        name="ln_res",
    )(x, y, g.reshape(1, d), b.reshape(1, d))


def _shifted(carry_ref, x, j):
    ext = jnp.concatenate([carry_ref[...], x], axis=0)
    return pltpu.roll(ext, j, axis=0)[SUBLANES:, :]


def _rwkv_prep_kernel(p_ref, sh0_ref, mu_ref, w0_ref, a0_ref, wlora_ref, alora_ref, gup_ref,
                      kk_ref, ka_ref, rk_ref, bd_ref, *refs, chunked, precise):
    outs, carry = refs[:-1], refs[-1]
    ti = pl.program_id(1)

    @pl.when(ti == 0)
    def _():
        carry[...] = jnp.broadcast_to(sh0_ref[0], carry.shape)

    p = p_ref[...]
    tm = p.shape[0]
    p_prev = _shifted(carry, p, 1)
    carry[...] = p[tm - SUBLANES:, :]
    pm = p + mu_ref[...] * (p_prev - p)
    W = RWKV_W
    r = pm[:, 0:W]
    k = pm[:, W:2 * W]
    v = pm[:, 2 * W:3 * W]
    lo = pm[:, 3 * W:3 * W + LORA_DECAY + LORA_A]
    gd = pm[:, 3 * W + LORA_DECAY + LORA_A:RWKV_SHIFT_W]
    lane = lax.broadcasted_iota(jnp.int32, lo.shape, 1)
    z = jnp.where(lane < LORA_DECAY, jnp.tanh(lo), lo)
    w_l = _mdot(z, wlora_ref[...], precise)
    a_l = _mdot(z, alora_ref[...], precise)
    w = -_softplus(-(w0_ref[...] + w_l)) - 0.5
    log_decay = -jnp.exp(w)
    decay = jnp.exp(log_decay)
    a = _sigmoid(a0_ref[...] + a_l)
    g = _mdot(_sigmoid(gd), gup_ref[...], precise)
    bd = bd_ref[...]
    kk = k * kk_ref[...]
    ss = _dot2(kk * kk, bd)
    kn = kk * lax.rsqrt(jnp.maximum(ss, 1e-24))
    k_eff = k * (1.0 + (a - 1.0) * ka_ref[...])
    bvec = kn * a
    bonus = _dot2(r * k_eff * rk_ref[...], bd) * v
    if chunked:
        vals = (r, log_decay, k_eff, kn, bvec, v, g, bonus)
    else:
        vals = (r, decay, k_eff, kn, bvec, v, decay * r, _dot2(bvec * r, bd), _dot2(k_eff * r, bd), g, bonus)
    for o_ref, val in zip(outs, vals):
        o_ref[...] = val


def rwkv_prep(p_rw, sh0, B, T, prm, chunked, precise):
    n_out = 8 if chunked else 11
    tm = _pick(T, (256, 128, 64, 32, 16, 8))
    nt = T // tm
    W = RWKV_W
    row_in = pl.BlockSpec((tm, RWKV_PAD_W), lambda b, t: (b * nt + t, 0))
    row_out = pl.BlockSpec((tm, W), lambda b, t: (b * nt + t, 0))
    full = lambda a: pl.BlockSpec(a.shape, lambda b, t: (0,) * a.ndim)
    sh0p = sh0.reshape(B, 1, RWKV_PAD_W)
    consts = [prm["mu"], prm["w0"], prm["a0"], prm["wlora"], prm["alora"], prm["gup"],
              prm["k_k"], prm["k_a"], prm["r_k"], prm["bd"]]
    outs = pl.pallas_call(
        functools.partial(_rwkv_prep_kernel, chunked=chunked, precise=precise),
        grid=(B, nt),
        in_specs=[row_in, pl.BlockSpec((1, 1, RWKV_PAD_W), lambda b, t: (b, 0, 0))] + [full(c) for c in consts],
        out_specs=[row_out] * n_out,
        out_shape=[jax.ShapeDtypeStruct((B * T, W), F32)] * n_out,
        scratch_shapes=[pltpu.VMEM((SUBLANES, RWKV_PAD_W), F32)],
        compiler_params=_cparams("arbitrary", "arbitrary"),
        name="rwkv_prep",
    )(p_rw, sh0p, *consts)
    return outs


PAIR_GROUP = 4
RWKV_SCAN_BATCH = 2


def _rwkv_scan_kernel(r_ref, w_ref, k_ref, kn_ref, b_ref, v_ref, wr_ref, br_ref, kr_ref, s0_ref, oseg_ref,
                      y_ref, sfin_ref, s_scr, *, nb, tc):
    ti = pl.program_id(1)

    @pl.when(ti == 0)
    def _():
        s_scr[...] = s0_ref[...]

    row = lax.broadcasted_iota(jnp.int32, (RWKV_HD, LANES), 0)
    lane = lax.broadcasted_iota(jnp.int32, (RWKV_HD, LANES), 1)
    diag = (lane & (RWKV_HD - 1)) == row
    oseg = oseg_ref[...]
    n_pairs = nb * PAIRS
    R = RWKV_HD

    def step8(t8, carry):
        base = pl.multiple_of(t8 * SUBLANES, SUBLANES)
        for g0 in range(0, n_pairs, PAIR_GROUP):
            ids = [(i // PAIRS, i % PAIRS) for i in range(g0, g0 + PAIR_GROUP)]
            n = len(ids)
            tile = lambda ref, b, p: ref[b, pl.ds(base, SUBLANES), p * LANES:(p + 1) * LANES]
            tiles = {nm: [tile(ref, b, p) for (b, p) in ids]
                     for nm, ref in (("kn", kn_ref), ("wr", wr_ref), ("v", v_ref), ("w", w_ref), ("b", b_ref),
                                     ("k", k_ref), ("br", br_ref), ("kr", kr_ref))}
            s_cur = [s_scr[b * PAIRS + p] for (b, p) in ids]
            y_rows = [[] for _ in ids]
            for j in range(SUBLANES):
                row = lambda nm, i: tiles[nm][i][j:j + 1, :]
                blocks = ([s_cur[i] * row("kn", i) for i in range(n)]
                          + [s_cur[i] * row("wr", i) for i in range(n)]
                          + [jnp.where(diag, row("v", i), 0.0) for i in range(n)])
                red = _dot2(jnp.concatenate(blocks, axis=0), oseg)
                for i in range(n):
                    skk = red[i * R:(i + 1) * R]
                    u = red[(n + i) * R:(n + i + 1) * R]
                    vb = red[(2 * n + i) * R:(2 * n + i + 1) * R]
                    s_cur[i] = s_cur[i] * row("w", i) - skk * row("b", i) + vb * row("k", i)
                    y_bc = u - skk * row("br", i) + vb * row("kr", i)
                    y_rows[i].append(jnp.sum(jnp.where(diag, y_bc, 0.0), axis=0, keepdims=True))
            for i, (b, p) in enumerate(ids):
                s_scr[b * PAIRS + p] = s_cur[i]
                y_ref[b, pl.ds(base, SUBLANES), p * LANES:(p + 1) * LANES] = jnp.concatenate(y_rows[i], axis=0)
        return carry

    lax.fori_loop(0, tc // SUBLANES, step8, 0)

    @pl.when(ti == pl.num_programs(1) - 1)
    def _():
        sfin_ref[...] = s_scr[...]


def rwkv_scan(seqs, s0, B, T):
    tc = _pick(T, (128, 64, 32, 16, 8))
    nb = RWKV_SCAN_BATCH
    assert B % nb == 0
    views = [s.reshape(B, T, RWKV_W) for s in seqs]
    s0p = s0.reshape(B, PAIRS, 2, RWKV_HD, RWKV_HD).transpose(0, 1, 3, 2, 4).reshape(B * PAIRS, RWKV_HD, LANES)
    li = jnp.arange(LANES)
    oseg = (li[:, None] // RWKV_HD == li[None, :] // RWKV_HD).astype(BF16)
    seq_spec = pl.BlockSpec((nb, tc, RWKV_W), lambda b, t: (b, t, 0))
    st_spec = pl.BlockSpec((nb * PAIRS, RWKV_HD, LANES), lambda b, t: (b, 0, 0))
    y, sfin = pl.pallas_call(
        functools.partial(_rwkv_scan_kernel, nb=nb, tc=tc),
        grid=(B // nb, T // tc),
        in_specs=[seq_spec] * 9 + [st_spec, pl.BlockSpec((LANES, LANES), lambda b, t: (0, 0))],
        out_specs=[seq_spec, st_spec],
        out_shape=[jax.ShapeDtypeStruct((B, T, RWKV_W), F32),
                   jax.ShapeDtypeStruct((B * PAIRS, RWKV_HD, LANES), F32)],
        scratch_shapes=[pltpu.VMEM((nb * PAIRS, RWKV_HD, LANES), F32)],
        compiler_params=_cparams("arbitrary", "arbitrary"),
        name="rwkv_scan",
    )(*views, s0p, oseg)
    sfin = sfin.reshape(B, PAIRS, RWKV_HD, 2, RWKV_HD).transpose(0, 1, 3, 2, 4).reshape(B, RWKV_HEADS, RWKV_HD, RWKV_HD)
    return y.reshape(B * T, RWKV_W), sfin


RWKV_CHUNK = 64
CHUNK_PAIRS = 8


def _split_bf(x):
    hi = x.astype(BF16)
    return hi, (x - hi.astype(F32)).astype(BF16)


def _dot3(a, b):
    ah, al = _split_bf(a)
    bh, bl = _split_bf(b)
    return _dot(ah, bh) + _dot(al, bh) + _dot(ah, bl)


def _dot3_nt(a, b):
    ah, al = _split_bf(a)
    bh, bl = _split_bf(b)
    return _dot_nt(ah, bh) + _dot_nt(al, bh) + _dot_nt(ah, bl)


def _rwkv_chunk_math(sts, rs, lws, ks, kns, bs, vs):
    C = RWKV_CHUNK
    lane = lax.broadcasted_iota(jnp.int32, (C, LANES), 1)
    row = lax.broadcasted_iota(jnp.int32, (C, LANES), 0)
    head0 = lane < RWKV_HD
    by_head = lambda x: jnp.concatenate([jnp.where(head0, x, 0.0), jnp.where(head0, 0.0, x)], axis=0)
    each = lambda f, *ls: [f(*a) for a in zip(*ls)]

    def cumsum_rows(lw):
        cs, d = lw, 1
        while d < C:
            cs = cs + jnp.where(row >= d, pltpu.roll(cs, d, axis=0), 0.0)
            d *= 2
        return cs

    css = each(cumsum_rows, lws)
    totals = [cs[C - 1:C, :] for cs in css]
    kps = each(lambda kn, cs, lw: kn * jnp.exp(cs - lw), kns, css, lws)
    rps = each(lambda r, cs: r * jnp.exp(cs), rs, css)
    e_negs = [jnp.exp(-cs) for cs in css]
    e_rems = each(lambda t, cs: jnp.exp(t - cs), totals, css)
    grams = each(lambda kp, rp, k, b, e: _dot3_nt(jnp.concatenate([kp, rp], axis=0),
                                                  jnp.concatenate([by_head(k * e), by_head(b * e)], axis=0)),
                 kps, rps, ks, bs, e_negs)
    s_idx = lane & (C - 1)
    strict, incl = s_idx < row, s_idx <= row
    a_ks = [jnp.where(strict, g[:C, :LANES], 0.0) for g in grams]
    a_bs = [jnp.where(strict, g[:C, LANES:], 0.0) for g in grams]
    d_ks = [jnp.where(incl, g[C:, :LANES], 0.0) for g in grams]
    d_bs = [jnp.where(incl, g[C:, LANES:], 0.0) for g in grams]
    r2 = lax.broadcasted_iota(jnp.int32, (2 * C, LANES), 0)
    l2 = lax.broadcasted_iota(jnp.int32, (2 * C, LANES), 1)
    eye = r2 == l2
    n_pows = [-by_head(a) for a in a_bs]
    t_invs = [jnp.where(eye, 1.0, 0.0) + n for n in n_pows]
    p = 2
    while p < C:
        n_pows = each(lambda n: _dot3(n, n), n_pows)
        t_invs = each(lambda t, n: t + _dot3(t, n), t_invs, n_pows)
        p *= 2
    v_bhs = [by_head(v) for v in vs]
    rhss = each(lambda kp, a, st, vb: _dot3(jnp.concatenate([kp, a], axis=1), jnp.concatenate([st, vb], axis=0)),
                kps, a_ks, sts, v_bhs)
    zs = each(lambda t, rhs: _dot3(t[:C] + t[C:], by_head(rhs)), t_invs, rhss)
    ys = each(lambda rp, dk, db, st, vb, z: _dot3(jnp.concatenate([rp, dk, -db], axis=1),
                                                  jnp.concatenate([st, vb, by_head(z)], axis=0)),
              rps, d_ks, d_bs, sts, v_bhs, zs)
    same_head = (r2 < RWKV_HD) == (l2 < RWKV_HD)

    def new_state(st, k, b, e, total, v, z):
        g_total = jnp.where(eye, jnp.broadcast_to(jnp.exp(total), (2 * C, LANES)), 0.0)
        kb_t = jnp.transpose(jnp.concatenate([k * e, b * e], axis=0))
        upd = _dot3(kb_t, jnp.concatenate([v, -z], axis=0))
        return _dot3(g_total, st) + jnp.where(same_head, upd, 0.0)

    return ys, each(new_state, sts, ks, bs, e_rems, totals, vs, zs)


def _rwkv_chunk_kernel(r_ref, lw_ref, k_ref, kn_ref, b_ref, v_ref, s0_ref, y_ref, sfin_ref, s_scr):
    ci = pl.program_id(2)

    @pl.when(ci == 0)
    def _():
        s_scr[...] = s0_ref[...]

    sls = [slice(i * LANES, (i + 1) * LANES) for i in range(CHUNK_PAIRS)]
    cols = lambda ref: [ref[:, sl] for sl in sls]
    ys, sts = _rwkv_chunk_math([s_scr[i] for i in range(CHUNK_PAIRS)], cols(r_ref), cols(lw_ref), cols(k_ref),
                               cols(kn_ref), cols(b_ref), cols(v_ref))
    for i, sl in enumerate(sls):
        y_ref[:, sl] = ys[i]
        s_scr[i] = sts[i]

    @pl.when(ci == pl.num_programs(2) - 1)
    def _():
        sfin_ref[...] = s_scr[...]


def rwkv_scan_chunked(seqs, s0, B, T):
    C, PP = RWKV_CHUNK, CHUNK_PAIRS
    assert T % C == 0 and PAIRS % PP == 0
    npg = PAIRS // PP
    views = [s.reshape(B, T, RWKV_W) for s in seqs]
    s0t = s0.reshape(B, PAIRS, 2, RWKV_HD, RWKV_HD).transpose(0, 1, 2, 4, 3)
    eye2 = jnp.eye(2, dtype=F32)
    s0bd = (s0t[:, :, :, :, None, :] * eye2[None, None, :, None, :, None]).reshape(B * PAIRS, LANES, LANES)
    seq_spec = pl.BlockSpec((None, C, PP * LANES), lambda b, g, c: (b, c, g))
    st_spec = pl.BlockSpec((PP, LANES, LANES), lambda b, g, c: (b * npg + g, 0, 0))
    y, sfin = pl.pallas_call(
        _rwkv_chunk_kernel,
        grid=(B, npg, T // C),
        in_specs=[seq_spec] * 6 + [st_spec],
        out_specs=[seq_spec, st_spec],
        out_shape=[jax.ShapeDtypeStruct((B, T, RWKV_W), F32),
                   jax.ShapeDtypeStruct((B * PAIRS, LANES, LANES), F32)],
        scratch_shapes=[pltpu.VMEM((PP, LANES, LANES), F32)],
        compiler_params=_cparams("parallel", "parallel", "arbitrary"),
        name="rwkv_chunk",
    )(*views, s0bd)
    sf = sfin.reshape(B, PAIRS, 2, RWKV_HD, 2, RWKV_HD)
    sf = jnp.stack([sf[:, :, 0, :, 0, :], sf[:, :, 1, :, 1, :]], axis=2)
    return y.reshape(B * T, RWKV_W), sf.transpose(0, 1, 2, 4, 3).reshape(B, RWKV_HEADS, RWKV_HD, RWKV_HD)


def _rwkv_post_kernel(y_ref, bonus_ref, g_ref, gng_ref, gnb_ref, bd_ref, o_ref):
    y = y_ref[...]
    bd = bd_ref[...]
    inv = 1.0 / RWKV_HD
    mean = _dot2(y, bd) * inv
    yc = y - mean
    var = _dot2(yc * yc, bd) * inv
    yn = yc * lax.rsqrt(var + GN_EPS) * gng_ref[...] + gnb_ref[...]
    o_ref[...] = (yn + bonus_ref[...]) * g_ref[...]


def rwkv_post(y, bonus, g, gn_g, gn_b, bd):
    m, W = y.shape
    tm = _pick(m, (256, 128, 64, 32, 16, 8))
    row = pl.BlockSpec((tm, W), lambda i: (i, 0))
    vec = pl.BlockSpec((1, W), lambda i: (0, 0))
    return pl.pallas_call(
        _rwkv_post_kernel,
        grid=(m // tm,),
        in_specs=[row, row, row, vec, vec, pl.BlockSpec((W, W), lambda i: (0, 0))],
        out_specs=row,
        out_shape=jax.ShapeDtypeStruct((m, W), F32),
        compiler_params=_cparams("parallel"),
        name="rwkv_post",
    )(y, bonus, g, gn_g.reshape(1, W), gn_b.reshape(1, W), bd)


def _gelu_tanh(x):
    return 0.5 * x * (1.0 + jnp.tanh(math.sqrt(2.0 / math.pi) * (x + 0.044715 * (x * x * x))))


def _neg_expm1(x):
    series = -x * (1.0 + x * (1.0 / 2.0) * (1.0 + x * (1.0 / 3.0) * (1.0 + x * (1.0 / 4.0) * (1.0 + x * (1.0 / 5.0)
             * (1.0 + x * (1.0 / 6.0))))))
    return jnp.where(x > -0.05, series, 1.0 - jnp.exp(x))


def _lru_kernel(px_ref, pg_ref, c0_ref, h0_ref, cw_ref, cb_ref, wa_ref, ba_ref, wx_ref, bx_ref, lam_ref,
                y_ref, hlast_ref, carry, hcar, *, precise):
    ti = pl.program_id(1)

    @pl.when(ti == 0)
    def _():
        carry[...] = c0_ref[0]
        hcar[...] = h0_ref[0]

    x = px_ref[...]
    tm = x.shape[0]
    cw = cw_ref[...]
    xc = cb_ref[...] + cw[CONV_W - 1:CONV_W, :] * x
    for j in range(1, CONV_W):
        xc = xc + cw[CONV_W - 1 - j:CONV_W - j, :] * _shifted(carry, x, j)
    carry[...] = x[tm - SUBLANES:, :]
    xcm = xc if precise else xc.astype(BF16)
    r = _sigmoid(_mdot(xcm, wa_ref[...], precise) + ba_ref[...])
    i = _sigmoid(_mdot(xcm, wx_ref[...], precise) + bx_ref[...])
    log_a = (-LRU_C * _softplus(-lam_ref[...])) * r
    a = jnp.exp(log_a)
    b = jnp.sqrt(_neg_expm1(2.0 * log_a)) * (i * xc)
    rows = lax.broadcasted_iota(jnp.int32, a.shape, 0)
    d = 1
    while d < tm:
        a_sh = pltpu.roll(a, d, axis=0)
        b_sh = pltpu.roll(b, d, axis=0)
        keep = rows >= d
        b = jnp.where(keep, a * b_sh + b, b)
        a = jnp.where(keep, a * a_sh, a)
        d *= 2
    h = a * hcar[...] + b
    hcar[...] = h[tm - 1:tm, :]
    hlast_ref[0] = h[tm - 1:tm, :]
    y_ref[...] = h * _gelu_tanh(pg_ref[...])


def lru_mix(p_l, c0, h0, B, T, prm, precise):
    W = LRU_W
    tm = _pick(T, (256, 128, 64, 32, 16, 8))
    nt = T // tm
    c0p = jnp.concatenate([jnp.zeros((B, SUBLANES - (CONV_W - 1), W), F32), c0], axis=1)
    h0p = h0.reshape(B, 1, W)
    full = lambda a: pl.BlockSpec(a.shape, lambda b, t: (0,) * a.ndim)
    consts = [prm["conv_w"], prm["conv_b"], prm["wa"], prm["ba"], prm["wx"], prm["bx"], prm["lam"]]
    y, hlast = pl.pallas_call(
        functools.partial(_lru_kernel, precise=precise),
        grid=(B, nt),
        in_specs=[pl.BlockSpec((tm, W), lambda b, t: (b * nt + t, 0)),
                  pl.BlockSpec((tm, W), lambda b, t: (b * nt + t, 1)),
                  pl.BlockSpec((1, SUBLANES, W), lambda b, t: (b, 0, 0)),
                  pl.BlockSpec((1, 1, W), lambda b, t: (b, 0, 0))] + [full(c) for c in consts],
        out_specs=[pl.BlockSpec((tm, W), lambda b, t: (b * nt + t, 0)),
                   pl.BlockSpec((1, 1, W), lambda b, t: (b, 0, 0))],
        out_shape=[jax.ShapeDtypeStruct((B * T, W), F32), jax.ShapeDtypeStruct((B, 1, W), F32)],
        scratch_shapes=[pltpu.VMEM((SUBLANES, W), F32), pltpu.VMEM((1, W), F32)],
        compiler_params=_cparams("arbitrary", "arbitrary"),
        name="lru_mix",
    )(p_l, p_l, c0p, h0p, *consts)
    return y, hlast.reshape(B, W)


def _block_diag(w):
    n, d, e = w.shape
    eye = jnp.eye(n, dtype=w.dtype)
    return (eye[:, None, :, None] * w[:, :, None, :]).reshape(n * d, n * e)


def rec_params(li, rec_w_in, rec_mu, rwkv_w0, rwkv_w_up, rwkv_a0, rwkv_a_up, rwkv_g_up, rwkv_k_k, rwkv_k_a,
               rwkv_r_k, rwkv_gn_g, rwkv_gn_b, lru_conv_w, lru_conv_b, lru_wa, lru_ba, lru_wx, lru_bx, lru_lambda,
               rec_w_out):
    W = RWKV_W
    pad = RWKV_PAD_W - RWKV_SHIFT_W
    hi = jnp.arange(W) // RWKV_HD
    zl = jnp.zeros((LORA_DECAY, W), F32)
    return dict(
        w_rw=jnp.pad(rec_w_in[li][:, :RWKV_SHIFT_W], ((0, 0), (0, pad))),
        w_l=rec_w_in[li][:, RWKV_SHIFT_W:],
        mu=jnp.pad(rec_mu[li], (0, pad)).reshape(1, RWKV_PAD_W),
        w0=rwkv_w0[li].reshape(1, W), a0=rwkv_a0[li].reshape(1, W),
        wlora=jnp.concatenate([rwkv_w_up[li], zl], axis=0),
        alora=jnp.concatenate([zl, rwkv_a_up[li]], axis=0),
        gup=rwkv_g_up[li],
        k_k=rwkv_k_k[li].reshape(1, W), k_a=rwkv_k_a[li].reshape(1, W), r_k=rwkv_r_k[li].reshape(1, W),
        bd=(hi[:, None] == hi[None, :]).astype(BF16),
        gn_g=rwkv_gn_g[li], gn_b=rwkv_gn_b[li],
        conv_w=lru_conv_w[li], conv_b=lru_conv_b[li].reshape(1, LRU_W),
        wa=_block_diag(lru_wa[li]), ba=lru_ba[li].reshape(1, LRU_W),
        wx=_block_diag(lru_wx[li]), bx=lru_bx[li].reshape(1, LRU_W),
        lam=lru_lambda[li].reshape(1, LRU_W),
        w_out=rec_w_out[li],
    )


def rec_mixer(x2d, B, T, S0, sh0, h0, c0, prm, precise=False):
    p_rw = mm(x2d, prm["w_rw"], precise)
    p_l = mm(x2d, prm["w_l"], precise)
    sh0p = jnp.pad(sh0, ((0, 0), (0, RWKV_PAD_W - RWKV_SHIFT_W)))
    chunked = T % RWKV_CHUNK == 0
    outs = rwkv_prep(p_rw, sh0p, B, T, prm, chunked, precise)
    g, bonus = outs[-2:]
    if chunked:
        y_raw, s_fin = rwkv_scan_chunked(outs[:6], S0, B, T)
    else:
        y_raw, s_fin = rwkv_scan(outs[:9], S0, B, T)
    y_rw = rwkv_post(y_raw, bonus, g, prm["gn_g"], prm["gn_b"], prm["bd"])
    y_lru, h_last = lru_mix(p_l, c0, h0, B, T, prm, precise)
    y = mm_cat2(y_rw, y_lru, prm["w_out"], precise)
    sh = p_rw.reshape(B, T, RWKV_PAD_W)[:, -1, :RWKV_SHIFT_W]
    px = p_l.reshape(B, T, 2 * LRU_W)[:, :, :LRU_W]
    cbuf = jnp.concatenate([c0, px], axis=1)[:, -(CONV_W - 1):] if T < CONV_W - 1 else px[:, -(CONV_W - 1):]
    return y, s_fin, sh, h_last, cbuf


def _dot_nt(a, b):
    return lax.dot_general(a, b, (((1,), (1,)), ((), ())), preferred_element_type=F32)


def _compress_kernel(slab_ref, w_ref, o_ref, *, n_pieces, precise):
    acc = None
    for l in range(CMP_STRIDE):
        d = _mdot(slab_ref[pl.ds(l, n_pieces, stride=CMP_STRIDE), :], w_ref[l], precise)
        acc = d if acc is None else acc + d
    out = acc[:, :HEAD_DIM] + pltpu.roll(acc[:, HEAD_DIM:], n_pieces - 1, axis=0)
    o_ref[0:n_pieces, :] = out
    pad = o_ref.shape[0] - n_pieces
    if pad:
        o_ref[n_pieces:, :] = jnp.zeros((pad, HEAD_DIM), F32)


def nsa_compress(kv2d, col0, B, t_kv, wcat, ncp, precise):
    n_pieces = t_kv // CMP_STRIDE
    return pl.pallas_call(
        functools.partial(_compress_kernel, n_pieces=n_pieces, precise=precise),
        grid=(B, 2, N_KV),
        in_specs=[pl.BlockSpec((t_kv, HEAD_DIM), lambda b, c, g: (b, col0 + c * N_KV + g)),
                  pl.BlockSpec((None, CMP_STRIDE, HEAD_DIM, 2 * HEAD_DIM), lambda b, c, g: (c, 0, 0, 0))],
        out_specs=pl.BlockSpec((None, None, None, ncp, HEAD_DIM), lambda b, c, g: (b, c, g, 0, 0)),
        out_shape=jax.ShapeDtypeStruct((B, 2, N_KV, ncp, HEAD_DIM), F32),
        compiler_params=_cparams("parallel", "parallel", "parallel"),
        name="nsa_compress",
    )(kv2d, wcat)


def _online_softmax_step(q, k, v, bias, mask, m, l, acc, precise):
    s = _mdot_nt(q, k, precise) * SCALE - bias
    s = jnp.where(mask, s, NEG)
    m_new = jnp.maximum(m, jnp.max(s, axis=-1, keepdims=True))
    alpha = jnp.exp(m - m_new)
    p = jnp.where(mask, jnp.exp(s - m_new), 0.0)
    l = alpha * l + jnp.sum(p, axis=-1, keepdims=True)
    acc = alpha * acc + _mdot(p, v, precise)
    return m_new, l, acc


def _nsa_kernel(q_ref, kc_ref, vc_ref, ks_ref, vs_ref, kw_ref, vw_ref, gate_ref, gb_ref, ov_ref, o_ref, *,
                tq, q_off, nc_real, ns_real, tk, n_kv_static, w_off, w_len, w_follows_q, precise):
    g = pl.program_id(1)
    cast = (lambda a: a) if precise else (lambda a: a.astype(BF16))
    qi = pl.program_id(2)
    q0 = qi * tq
    ncp = kc_ref.shape[0]
    nsp = ov_ref.shape[1]
    qpos_col = q_off + q0 + lax.broadcasted_iota(jnp.int32, (tq, 1), 0)
    q_all = q_ref[...]
    q_bf = [cast(q_all[:, r * HEAD_DIM:(r + 1) * HEAD_DIM]) for r in range(GQA_R)]
    head_f = (g * GQA_R).astype(F32)
    slopes = [jnp.exp2(-0.5 * (jnp.full((1, 1), 1.0 + r, F32) + head_f)) for r in range(GQA_R)]

    n_idx = lax.broadcasted_iota(jnp.int32, (tq, ncp), 1)
    dist_c = qpos_col - (n_idx * CMP_STRIDE + (CMP_BLOCK - 1))
    ok_c = (dist_c >= 0) & (n_idx < nc_real)
    dist_cf = dist_c.astype(F32)
    kc_bf = cast(kc_ref[...])
    vc_bf = cast(vc_ref[...])
    o_c = []
    p_sum = None
    for r in range(GQA_R):
        s = _mdot_nt(q_bf[r], kc_bf, precise) * SCALE - slopes[r] * dist_cf
        s = jnp.where(ok_c, s, NEG)
        m = jnp.max(s, axis=-1, keepdims=True)
        e = jnp.where(ok_c, jnp.exp(s - m), 0.0)
        p = e / jnp.maximum(jnp.sum(e, axis=-1, keepdims=True), 1e-30)
        o_c.append(_mdot(p, vc_bf, precise))
        p_sum = p if p_sum is None else p_sum + p
    imp = _dot2(p_sum, ov_ref[...])
    s_idx = lax.broadcasted_iota(jnp.int32, (tq, nsp), 1)
    cur = lax.shift_right_logical(qpos_col, int(math.log2(SLC_BLOCK)))
    forced = (s_idx == 0) | (s_idx == cur) | (s_idx == cur - 1)
    causal_blk = (s_idx * SLC_BLOCK <= qpos_col) & (s_idx < ns_real)
    score = jnp.where(causal_blk, imp + jnp.where(forced, BIG, 0.0), NEG)
    rank = jnp.zeros((tq, nsp), jnp.int32)
    for j in range(ns_real):
        col = score[:, j:j + 1]
        ahead = (col > score) | ((col == score) & (s_idx > j))
        rank = rank + jnp.where(ahead, 1, 0)
    sel = (rank < min(TOP_N, ns_real)) & (score > 0.5 * NEG)
    sel_bf = jnp.where(sel, 1.0, 0.0).astype(BF16)

    def init():
        return tuple((jnp.full((tq, 1), NEG, F32), jnp.zeros((tq, 1), F32), jnp.zeros((tq, HEAD_DIM), F32))
                     for _ in range(GQA_R))

    blk_per_tile = tk // SLC_BLOCK
    ex_row = lax.broadcasted_iota(jnp.int32, (nsp, tk), 0)
    ex_col = lax.shift_right_logical(lax.broadcasted_iota(jnp.int32, (nsp, tk), 1), int(math.log2(SLC_BLOCK)))
    key_i = lax.broadcasted_iota(jnp.int32, (tq, tk), 1)

    def slc_body(kt, carry):
        base = pl.multiple_of(kt * tk, tk)
        expand = jnp.where(ex_row == ex_col + kt * blk_per_tile, 1.0, 0.0).astype(BF16)
        selexp = _dot(sel_bf, expand)
        dist = qpos_col - (key_i + base)
        mask = (dist >= 0) & (selexp > 0.5)
        distf = dist.astype(F32)
        k_bf = cast(ks_ref[pl.ds(base, tk), :])
        v_bf = cast(vs_ref[pl.ds(base, tk), :])
        return tuple(_online_softmax_step(q_bf[r], k_bf, v_bf, slopes[r] * distf, mask, *carry[r], precise)
                     for r in range(GQA_R))

    if n_kv_static is None:
        n_kv = (q0 + tq + tk - 1) // tk
    else:
        n_kv = n_kv_static
    st_s = lax.fori_loop(0, n_kv, slc_body, init())

    if w_follows_q:
        w_base = pl.multiple_of(jnp.maximum(q0 + tq - w_len, 0), SUBLANES)
    else:
        w_base = 0
    dist_w = qpos_col - (w_off + w_base + lax.broadcasted_iota(jnp.int32, (tq, w_len), 1))
    ok_w = jnp.where(dist_w >= 0, dist_w, WINDOW) < WINDOW
    dist_wf = dist_w.astype(F32)
    kw_bf = cast(kw_ref[pl.ds(w_base, w_len), :])
    vw_bf = cast(vw_ref[pl.ds(w_base, w_len), :])
    st_w = tuple(_online_softmax_step(q_bf[r], kw_bf, vw_bf, slopes[r] * dist_wf, ok_w, *init()[r], precise)
                 for r in range(GQA_R))

    gates = _sigmoid(gate_ref[...] + gb_ref[...])
    lane = lax.broadcasted_iota(jnp.int32, gates.shape, 1)
    for r in range(GQA_R):
        head = g * GQA_R + r
        gcol = lambda br: jnp.sum(jnp.where(lane == br * N_HEADS + head, gates, 0.0), axis=-1, keepdims=True)
        o_s = st_s[r][2] / jnp.maximum(st_s[r][1], 1e-30)
        o_w = st_w[r][2] / jnp.maximum(st_w[r][1], 1e-30)
        o_ref[:, r * HEAD_DIM:(r + 1) * HEAD_DIM] = gcol(0) * o_c[r] + gcol(1) * o_s + gcol(2) * o_w


def nsa_attention(q2d, kcvc, kv2d, kv_col0, win2d, win_col0, gates, gate_b, overlap, *, B, t_q, t_kv, t_win, tq,
                  q_off, nc_real, ns_real, tk, n_kv_static, w_off, w_len, w_follows_q, precise):
    nq = t_q // tq
    ncp = kcvc.shape[3]
    nsp = overlap.shape[1]
    qw = GQA_R * HEAD_DIM
    slab = lambda arr_t, col: pl.BlockSpec((arr_t, HEAD_DIM), col)
    kern = functools.partial(_nsa_kernel, tq=tq, q_off=q_off, nc_real=nc_real, ns_real=ns_real, tk=tk,
                             n_kv_static=n_kv_static, w_off=w_off, w_len=w_len, w_follows_q=w_follows_q,
                             precise=precise)
    return pl.pallas_call(
        kern,
        grid=(B, N_KV, nq),
        in_specs=[pl.BlockSpec((tq, qw), lambda b, g, i: (b * nq + i, g)),
                  pl.BlockSpec((None, None, None, ncp, HEAD_DIM), lambda b, g, i: (b, 0, g, 0, 0)),
                  pl.BlockSpec((None, None, None, ncp, HEAD_DIM), lambda b, g, i: (b, 1, g, 0, 0)),
                  slab(t_kv, lambda b, g, i: (b, kv_col0 + 2 * N_KV + g)),
                  slab(t_kv, lambda b, g, i: (b, kv_col0 + 3 * N_KV + g)),
                  slab(t_win, lambda b, g, i: (b, win_col0 + g)),
                  slab(t_win, lambda b, g, i: (b, win_col0 + N_KV + g)),
                  pl.BlockSpec((tq, LANES), lambda b, g, i: (b * nq + i, 0)),
                  pl.BlockSpec((1, LANES), lambda b, g, i: (0, 0)),
                  pl.BlockSpec((ncp, nsp), lambda b, g, i: (0, 0))],
        out_specs=pl.BlockSpec((tq, qw), lambda b, g, i: (b * nq + i, g)),
        out_shape=jax.ShapeDtypeStruct((B * t_q, N_HEADS * HEAD_DIM), F32),
        compiler_params=_cparams("parallel", "parallel", "arbitrary"),
        name="nsa_attention",
    )(q2d, kcvc, kcvc, kv2d, kv2d, win2d, win2d, gates, gate_b, overlap)


def _overlap_matrix(ncp, nsp, nc_real, ns_real):
    n = jnp.arange(ncp)[:, None]
    s = jnp.arange(nsp)[None, :]
    c0 = n * CMP_STRIDE
    s0 = s * SLC_BLOCK
    ov = (c0 < s0 + SLC_BLOCK) & (c0 + CMP_BLOCK > s0) & (n < nc_real) & (s < ns_real)
    return ov.astype(BF16)


def _gather_pages_kernel(pt_ref, page_ref, tail_ref, o_ref):
    j = pl.program_id(1)
    n_pages = pl.num_programs(1) - 1

    @pl.when(j < n_pages)
    def _():
        o_ref[...] = page_ref[...]

    @pl.when(j == n_pages)
    def _():
        o_ref[...] = tail_ref[...]


def gather_pages(cache5, page_table, tail, li):
    B, n_pages = page_table.shape
    page, row_w = cache5.shape[1], cache5.shape[4]
    grid_spec = pltpu.PrefetchScalarGridSpec(
        num_scalar_prefetch=1,
        grid=(B, n_pages + 1),
        in_specs=[pl.BlockSpec((None, page, None, None, row_w),
                               lambda b, j, pt: (pt[b, jnp.minimum(j, n_pages - 1)], 0, li, 0, 0)),
                  pl.BlockSpec((None, page, row_w), lambda b, j, pt: (b, 0, 0))],
        out_specs=pl.BlockSpec((None, page, row_w), lambda b, j, pt: (b, j, 0)),
    )
    return pl.pallas_call(
        _gather_pages_kernel,
        grid_spec=grid_spec,
        out_shape=jax.ShapeDtypeStruct((B, (n_pages + 1) * page, row_w), F32),
        compiler_params=_cparams("parallel", "arbitrary"),
        name="gather_pages",
    )(page_table, cache5, tail)


def attn_params(li, attn_w_in, attn_gate_b, w_cmp_k, w_cmp_v, attn_w_out):
    qkv_w = N_HEADS * HEAD_DIM + 6 * N_KV * HEAD_DIM
    ng = 3 * N_HEADS
    half = lambda w: jnp.concatenate([w[:CMP_STRIDE], w[CMP_STRIDE:]], axis=-1)
    return dict(
        w_main=attn_w_in[li][:, :qkv_w],
        w_gate=jnp.pad(attn_w_in[li][:, qkv_w:], ((0, 0), (0, LANES - ng))),
        gate_b=jnp.pad(attn_gate_b[li], (0, LANES - ng)).reshape(1, LANES),
        wcat=jnp.stack([half(w_cmp_k[li]), half(w_cmp_v[li])]),
        w_out=attn_w_out[li],
    )


def attn_prompt(x2d, B, T, prm):
    p = mm(x2d, prm["w_main"])
    gates = mm(x2d, prm["w_gate"])
    q_blocks = N_HEADS
    n_pieces = T // CMP_STRIDE
    nc_real = n_pieces - CMP_BLOCK // CMP_STRIDE + 1
    ns_real = T // SLC_BLOCK
    ncp = -(-n_pieces // LANES) * LANES
    nsp = -(-ns_real // LANES) * LANES
    kcvc = nsa_compress(p, q_blocks, B, T, prm["wcat"], ncp, False)
    ov = _overlap_matrix(ncp, nsp, nc_real, ns_real)
    tq = min(128, T)
    o = nsa_attention(p, kcvc, p, q_blocks, p, q_blocks + 4 * N_KV, gates, prm["gate_b"], ov,
                      B=B, t_q=T, t_kv=T, t_win=T, tq=tq, q_off=0, nc_real=nc_real, ns_real=ns_real,
                      tk=min(512, T), n_kv_static=None, w_off=0, w_len=min(WINDOW + tq, T), w_follows_q=True,
                      precise=False)
    y = mm(o, prm["w_out"])
    kv = p.reshape(B, T, -1)[:, :, N_HEADS * HEAD_DIM:].reshape(B, T, 6, N_KV, HEAD_DIM)
    w_buf = min(WINDOW, T)
    win = jnp.pad(kv[:, :, 4:], ((0, 0), (WINDOW, 0), (0, 0), (0, 0), (0, 0)))[:, -w_buf:]
    return y, kv[:, :, :4], win


def attn_sample(x2d, B, T, prm, cache_nsa_kv, cache_win, page_table, li):
    n_pool, page = cache_nsa_kv.shape[:2]
    n_attn = cache_nsa_kv.shape[2]
    n_pages = page_table.shape[1]
    past_len = n_pages * page
    row_w = 4 * N_KV * HEAD_DIM
    p = mm(x2d, prm["w_main"], True)
    gates = mm(x2d, prm["w_gate"], True)
    kv_new = p[:, N_HEADS * HEAD_DIM:].reshape(B, T, 6, N_KV, HEAD_DIM)
    tail = jnp.pad(kv_new[:, :, :4].reshape(B, T, row_w), ((0, 0), (0, page - T), (0, 0)))
    full = gather_pages(cache_nsa_kv.reshape(n_pool, page, n_attn, 1, row_w), page_table, tail, li)
    t_kv = (n_pages + 1) * page
    full2d = full.reshape(B * t_kv, row_w)
    L = past_len + T
    l_pad = -(-L // SLC_BLOCK) * SLC_BLOCK
    nc_real = l_pad // CMP_STRIDE - CMP_BLOCK // CMP_STRIDE + 1
    ns_real = l_pad // SLC_BLOCK
    n_pieces = t_kv // CMP_STRIDE
    ncp = -(-n_pieces // LANES) * LANES
    nsp = -(-ns_real // LANES) * LANES
    kcvc = nsa_compress(full2d, 0, B, t_kv, prm["wcat"], ncp, True)
    ov = _overlap_matrix(ncp, nsp, nc_real, ns_real)
    w_buf = cache_win.shape[1]
    win = jnp.concatenate([cache_win, kv_new[:, :, 4:]], axis=1)
    t_win = -(-(w_buf + T) // LANES) * LANES
    win2d = jnp.pad(win.reshape(B, w_buf + T, 2 * N_KV * HEAD_DIM), ((0, 0), (0, t_win - (w_buf + T)), (0, 0)))
    win2d = win2d.reshape(B * t_win, 2 * N_KV * HEAD_DIM)
    n_kv = 3
    assert t_kv % (n_kv * LANES) == 0
    o = nsa_attention(p, kcvc, full2d, 0, win2d, 0, gates, prm["gate_b"], ov,
                      B=B, t_q=T, t_kv=t_kv, t_win=t_win, tq=T, q_off=past_len, nc_real=nc_real, ns_real=ns_real,
                      tk=t_kv // n_kv, n_kv_static=n_kv, w_off=past_len - w_buf, w_len=t_win, w_follows_q=False,
                      precise=True)
    y = mm(o, prm["w_out"], True)
    return y, kv_new[:, :, :4], win[:, -w_buf:]


MOE_TM = 256


ROUTER_TN = 256


def _first_argmax(vals):
    best, idx = vals[0], jnp.zeros(vals[0].shape, jnp.int32)
    for i in range(1, len(vals)):
        better = vals[i] > best
        best = jnp.where(better, vals[i], best)
        idx = jnp.where(better, i, idx)
    return best, idx


def _pick_by(idx, vals):
    out = vals[-1]
    for i in range(len(vals) - 2, -1, -1):
        out = jnp.where(idx == i, vals[i], out)
    return out


def _router_kernel(x_ref, wt_ref, bias_ref, e_ref, w_ref, rank_ref, cnt_ref, base, *, n_tok):
    i = pl.program_id(0)
    tn = x_ref.shape[0]

    @pl.when(i == 0)
    def _():
        base[...] = jnp.zeros(base.shape, F32)

    xh, xl = _split2(x_ref[...])
    wh, wl = _split2(wt_ref[...])
    logits = _dot_nt(wh, xh) + _dot_nt(wh, xl) + _dot_nt(wl, xh)
    aff = _sigmoid(logits)
    sel = aff + bias_ref[...]
    s = [sel[e:e + 1, :] for e in range(N_EXPERTS)]
    a = [aff[e:e + 1, :] for e in range(N_EXPERTS)]
    G = EXPERTS_PER_GROUP
    gscore = []
    for g in range(N_GROUPS):
        v0, v1, v2, v3 = s[G * g:G * g + G]
        hi1, lo1 = jnp.maximum(v0, v1), jnp.minimum(v0, v1)
        hi2, lo2 = jnp.maximum(v2, v3), jnp.minimum(v2, v3)
        gscore.append(jnp.maximum(hi1, hi2) + jnp.maximum(jnp.minimum(hi1, hi2), jnp.maximum(lo1, lo2)))
    _, gi = _first_argmax(gscore)
    sg = [_pick_by(gi, [s[G * g + j] for g in range(N_GROUPS)]) for j in range(G)]
    ag = [_pick_by(gi, [a[G * g + j] for g in range(N_GROUPS)]) for j in range(G)]
    _, l0 = _first_argmax(sg)
    _, l1 = _first_argmax([jnp.where(l0 == j, -jnp.inf, sg[j]) for j in range(G)])
    w0, w1 = _pick_by(l0, ag), _pick_by(l1, ag)
    wsum = w0 + w1
    e0, e1 = gi * G + l0, gi * G + l1
    e_ref[...] = jnp.concatenate([e0, e1], axis=0)
    w_ref[...] = jnp.concatenate([w0 / wsum, w1 / wsum], axis=0)
    tok = i * tn + lax.broadcasted_iota(jnp.int32, (N_EXPERTS, tn), 1)
    eid = lax.broadcasted_iota(jnp.int32, (N_EXPERTS, tn), 0)
    valid = tok < n_tok
    oh0 = jnp.where((eid == e0) & valid, 1.0, 0.0)
    oh1 = jnp.where((eid == e1) & valid, 1.0, 0.0)
    cnt = oh0 + oh1
    r_i = lax.broadcasted_iota(jnp.int32, (tn, tn), 0)
    c_i = lax.broadcasted_iota(jnp.int32, (tn, tn), 1)
    upper = jnp.where(r_i <= c_i, 1.0, 0.0).astype(BF16)
    excl = _dot(cnt.astype(BF16), upper) - cnt + base[...]
    rank_ref[...] = jnp.concatenate([jnp.sum(oh0 * excl, axis=0, keepdims=True),
                                     jnp.sum(oh1 * excl, axis=0, keepdims=True)], axis=0).astype(jnp.int32)
    base[...] = base[...] + jnp.sum(cnt, axis=1, keepdims=True)
    cnt_ref[...] = jnp.broadcast_to(base[...], cnt_ref.shape).astype(jnp.int32)


def router(x, w_t, bias_col):
    n, d = x.shape
    tn = ROUTER_TN if n >= ROUTER_TN else LANES
    nt = -(-n // tn)
    n_pad = nt * tn
    if n < tn:
        x = jnp.pad(x, ((0, tn - n), (0, 0)))
    two = lambda dt: jax.ShapeDtypeStruct((TOP_K, n_pad), dt)
    out_row = pl.BlockSpec((TOP_K, tn), lambda i: (0, i))
    e, w, rank, cnt = pl.pallas_call(
        functools.partial(_router_kernel, n_tok=n),
        grid=(nt,),
        in_specs=[pl.BlockSpec((tn, d), lambda i: (i, 0)),
                  pl.BlockSpec((N_EXPERTS, d), lambda i: (0, 0)),
                  pl.BlockSpec((N_EXPERTS, 1), lambda i: (0, 0))],
        out_specs=[out_row, out_row, out_row, pl.BlockSpec((N_EXPERTS, LANES), lambda i: (0, 0))],
        out_shape=[two(jnp.int32), two(F32), two(jnp.int32), jax.ShapeDtypeStruct((N_EXPERTS, LANES), jnp.int32)],
        scratch_shapes=[pltpu.VMEM((N_EXPERTS, 1), F32)],
        compiler_params=_cparams("arbitrary"),
        name="router",
    )(x, w_t, bias_col)
    return e[:, :n], w[:, :n], rank[:, :n], cnt[:, 0]


def _expert_kernel(te_ref, x_ref, rw_ref, wg_ref, wu_ref, wd_ref, o_ref):
    xb = x_ref[...].astype(BF16)
    hg = _dot(xb, wg_ref[...])
    hu = _dot(xb, wu_ref[...])
    h = (hg * _sigmoid(hg)) * hu * rw_ref[...]
    o_ref[...] = _dot(h.astype(BF16), wd_ref[...])


def expert_ffn(x_sorted, row_w, tile_expert, wg, wu, wd, layer):
    a_pad, d = x_sorted.shape
    f = wg.shape[3]
    n_tiles = a_pad // MOE_TM
    grid_spec = pltpu.PrefetchScalarGridSpec(
        num_scalar_prefetch=1,
        grid=(n_tiles,),
        in_specs=[pl.BlockSpec((MOE_TM, d), lambda t, te: (t, 0)),
                  pl.BlockSpec((MOE_TM, 1), lambda t, te: (t, 0)),
                  pl.BlockSpec((None, None, d, f), lambda t, te: (layer, te[t], 0, 0)),
                  pl.BlockSpec((None, None, d, f), lambda t, te: (layer, te[t], 0, 0)),
                  pl.BlockSpec((None, None, f, d), lambda t, te: (layer, te[t], 0, 0))],
        out_specs=pl.BlockSpec((MOE_TM, d), lambda t, te: (t, 0)),
    )
    return pl.pallas_call(
        _expert_kernel,
        grid_spec=grid_spec,
        out_shape=jax.ShapeDtypeStruct((a_pad, d), F32),
        compiler_params=_cparams("arbitrary"),
        name="expert_ffn",
    )(tile_expert, x_sorted, row_w, wg, wu, wd)


MOE_DENSE_FT = 512


def _moe_dense_kernel(x_ref, gate_ref, wg_ref, wu_ref, wd_ref, o_ref):
    e = pl.program_id(0)

    @pl.when((e == 0) & (pl.program_id(1) == 0))
    def _():
        o_ref[...] = jnp.zeros(o_ref.shape, F32)

    x = x_ref[...]
    hg = _dot3(x, wg_ref[...])
    hu = _dot3(x, wu_ref[...])
    gate = gate_ref[...]
    lane = lax.broadcasted_iota(jnp.int32, gate.shape, 1)
    gcol = jnp.sum(jnp.where(lane == e, gate, 0.0), axis=-1, keepdims=True)
    o_ref[...] += _dot3((hg * _sigmoid(hg)) * hu * gcol, wd_ref[...])


def moe_dense_precise(x, router_wt, r_bias_col, w_g, w_u, w_d, layer):
    n, d = x.shape
    f = w_g.shape[3]
    ft = MOE_DENSE_FT
    e, w_sel, _, _ = router(x, router_wt, r_bias_col)
    lanes = jnp.arange(LANES, dtype=jnp.int32)[None, :]
    gate = sum(jnp.where(e[k][:, None] == lanes, w_sel[k][:, None], 0.0) for k in range(TOP_K))
    return pl.pallas_call(
        _moe_dense_kernel,
        grid=(N_EXPERTS, f // ft),
        in_specs=[pl.BlockSpec((n, d), lambda e, j: (0, 0)),
                  pl.BlockSpec((n, LANES), lambda e, j: (0, 0)),
                  pl.BlockSpec((None, None, d, ft), lambda e, j: (layer, e, 0, j)),
                  pl.BlockSpec((None, None, d, ft), lambda e, j: (layer, e, 0, j)),
                  pl.BlockSpec((None, None, ft, d), lambda e, j: (layer, e, j, 0))],
        out_specs=pl.BlockSpec((n, d), lambda e, j: (0, 0)),
        out_shape=jax.ShapeDtypeStruct((n, d), F32),
        compiler_params=_cparams("arbitrary", "arbitrary"),
        name="moe_dense_precise",
    )(x, gate, w_g, w_u, w_d)


def moe(x, router_wt, r_bias_col, w_g, w_u, w_d, layer):
    n, d = x.shape
    e, w_sel, rank, counts = router(x, router_wt, r_bias_col)
    n_asg = n * TOP_K
    gsz = (counts + MOE_TM - 1) // MOE_TM * MOE_TM
    p_end = jnp.cumsum(gsz)
    p_start = p_end - gsz
    onehot = (e[:, :, None] == jnp.arange(N_EXPERTS, dtype=jnp.int32)[None, None, :]).astype(jnp.int32)
    dest = jnp.sum(onehot * p_start[None, None, :], axis=2) + rank
    a_pad = -(-(n_asg + N_EXPERTS * (MOE_TM - 1)) // MOE_TM) * MOE_TM
    tok = jnp.broadcast_to(jnp.arange(n, dtype=jnp.int32)[None, :], (TOP_K, n))
    src_tok = jnp.zeros((a_pad,), jnp.int32).at[dest.reshape(n_asg)].set(tok.reshape(n_asg))
    row_w = jnp.zeros((a_pad,), F32).at[dest.reshape(n_asg)].set(w_sel.reshape(n_asg))
    tile_start = jnp.arange(a_pad // MOE_TM, dtype=jnp.int32) * MOE_TM
    tile_expert = jnp.minimum(jnp.sum((p_end[None, :] <= tile_start[:, None]).astype(jnp.int32), axis=1),
                              N_EXPERTS - 1).astype(jnp.int32)
    ys = expert_ffn(x[src_tok], row_w.reshape(a_pad, 1), tile_expert, w_g, w_u, w_d, layer)
    return ys[dest[0]] + ys[dest[1]]


def kernel(x_prompt, x_sample, cache_nsa_kv, cache_win_kv, state_rwkv, state_rwkv_shift, state_lru_h, state_lru_conv,
           page_table, ln1_g, ln1_b, ln2_g, ln2_b, rec_w_in, rec_mu, rwkv_w0, rwkv_w_up, rwkv_a0, rwkv_a_up, rwkv_g_up,
           rwkv_k_k, rwkv_k_a, rwkv_r_k, rwkv_gn_g, rwkv_gn_b, lru_conv_w, lru_conv_b, lru_wa, lru_ba, lru_wx, lru_bx,
           lru_lambda, rec_w_out, attn_w_in, attn_gate_b, w_cmp_k, w_cmp_v, attn_w_out, moe_w_router, moe_router_bias,
           moe_w_gate, moe_w_up, moe_w_down):
    Bp, Tp, D = x_prompt.shape
    Bs, Ts, _ = x_sample.shape
    n_p, n_s = Bp * Tp, Bs * Ts
    rec_args = (rec_w_in, rec_mu, rwkv_w0, rwkv_w_up, rwkv_a0, rwkv_a_up, rwkv_g_up, rwkv_k_k, rwkv_k_a,
                rwkv_r_k, rwkv_gn_g, rwkv_gn_b, lru_conv_w, lru_conv_b, lru_wa, lru_ba, lru_wx, lru_bx, lru_lambda,
                rec_w_out)
    xp, xs = x_prompt.reshape(n_p, D), x_sample.reshape(n_s, D)
    router_w = moe_w_router.T
    moe_router_bias = moe_router_bias.reshape(N_EXPERTS, 1)
    wg_bf, wu_bf, wd_bf = moe_w_gate.astype(BF16), moe_w_up.astype(BF16), moe_w_down.astype(BF16)
    nsa_p, nsa_s, win_p, win_s = [], [], [], []
    rs_p, rs_s, rsh_p, rsh_s, lh_p, lh_s, lc_p, lc_s = [], [], [], [], [], [], [], []
    for layer in range(DEPTH):
        li = layer // 2
        if layer % 2 == 0:
            prm = rec_params(li, *rec_args)
            yp, S, sh, h, cb = rec_mixer(xp, Bp, Tp, jnp.zeros((Bp, RWKV_HEADS, RWKV_HD, RWKV_HD), F32),
                                         jnp.zeros((Bp, RWKV_SHIFT_W), F32), jnp.zeros((Bp, LRU_W), F32),
                                         jnp.zeros((Bp, CONV_W - 1, LRU_W), F32), prm)
            ys, S2, sh2, h2, cb2 = rec_mixer(xs, Bs, Ts, state_rwkv[:, li], state_rwkv_shift[:, li],
                                             state_lru_h[:, li], state_lru_conv[:, li], prm, precise=True)
            rs_p.append(S); rs_s.append(S2); rsh_p.append(sh); rsh_s.append(sh2)
            lh_p.append(h); lh_s.append(h2); lc_p.append(cb); lc_s.append(cb2)
        else:
            prm = attn_params(li, attn_w_in, attn_gate_b, w_cmp_k, w_cmp_v, attn_w_out)
            yp, rows, wb = attn_prompt(xp, Bp, Tp, prm)
            ys, rows2, wb2 = attn_sample(xs, Bs, Ts, prm, cache_nsa_kv, cache_win_kv[:, :, li], page_table, li)
            nsa_p.append(rows); nsa_s.append(rows2); win_p.append(wb); win_s.append(wb2)
        xp = ln_res(xp, yp, ln1_g[layer], ln1_b[layer])
        xs = ln_res(xs, ys, ln1_g[layer], ln1_b[layer])
        xp = ln_res(xp, moe(xp, router_w, moe_router_bias, wg_bf, wu_bf, wd_bf, layer), ln2_g[layer], ln2_b[layer])
        xs = ln_res(xs, moe_dense_precise(xs, router_w, moe_router_bias, moe_w_gate, moe_w_up, moe_w_down, layer),
                    ln2_g[layer], ln2_b[layer])
    return (xp.reshape(Bp, Tp, D), xs.reshape(Bs, Ts, D),
            jnp.stack(nsa_p, axis=2), jnp.stack(nsa_s, axis=2), jnp.stack(win_p, axis=2), jnp.stack(win_s, axis=2),
            jnp.stack(rs_p, axis=1), jnp.stack(rs_s, axis=1), jnp.stack(rsh_p, axis=1), jnp.stack(rsh_s, axis=1),
            jnp.stack(lh_p, axis=1), jnp.stack(lh_s, axis=1), jnp.stack(lc_p, axis=1), jnp.stack(lc_s, axis=1))
```

```python
import functools
import math

import jax
import jax.numpy as jnp
from jax import lax
from jax.experimental import pallas as pl
from jax.experimental.pallas import tpu as pltpu

F32 = jnp.float32
BF16 = jnp.bfloat16

D_MODEL = 2048
DEPTH = 4
RWKV_HEADS = 16
RWKV_HD = 64
RWKV_W = RWKV_HEADS * RWKV_HD
LORA_DECAY = 64
LORA_A = 64
LORA_GATE = 128
RWKV_SHIFT_W = 3 * RWKV_W + LORA_DECAY + LORA_A + LORA_GATE
GN_EPS = 64e-5
LRU_W = D_MODEL - RWKV_W
LRU_BLOCKS = 16
CONV_W = 4
LRU_C = 8.0
N_HEADS = 16
HEAD_DIM = 128
N_KV = 4
GQA_R = N_HEADS // N_KV
CMP_BLOCK = 32
CMP_STRIDE = 16
SLC_BLOCK = 64
TOP_N = 16
WINDOW = 512
SCALE = HEAD_DIM ** -0.5
N_EXPERTS = 16
N_GROUPS = 4
EXPERTS_PER_GROUP = N_EXPERTS // N_GROUPS
TOP_K = 2
D_EXPERT = 1024
LN_EPS = 1e-5
DN_ALPHA = (2 * DEPTH) ** 0.25
NEG = -1e30
BIG = 1e4

LANES = 128
SUBLANES = 8
VMEM_LIMIT_BYTES = 48 * 1024 * 1024

RWKV_PAD_W = 3584
PAIRS = RWKV_HEADS // 2


def _cparams(*sem):
    return pltpu.CompilerParams(dimension_semantics=sem, vmem_limit_bytes=VMEM_LIMIT_BYTES)


def _split2(x):
    hi = x.astype(BF16)
    lo = (x - hi.astype(F32)).astype(BF16)
    return hi, lo


def _dot(a, b):
    return jnp.dot(a, b, preferred_element_type=F32)


def _dot2(x, w_bf):
    hi, lo = _split2(x)
    return _dot(hi, w_bf) + _dot(lo, w_bf)


def _sigmoid(x):
    return 1.0 / (1.0 + jnp.exp(-x))


def _softplus(x):
    return jnp.maximum(x, 0.0) + jnp.log(1.0 + jnp.exp(-jnp.abs(x)))


def _mdot(a, b, precise):
    return _dot3(a, b) if precise else _dot(a.astype(BF16), b.astype(BF16))


def _mdot_nt(a, b, precise):
    return _dot3_nt(a, b) if precise else _dot_nt(a.astype(BF16), b.astype(BF16))


def _mm_kernel(x_ref, w_ref, o_ref, *, precise):
    o_ref[...] = _mdot(x_ref[...], w_ref[...], precise)


def _pick(n, cands):
    for c in cands:
        if n % c == 0:
            return c
    return n


def mm(x, w, precise=False):
    m, k = x.shape
    n = w.shape[1]
    tm = _pick(m, (512, 256, 128, 64, 32, 16, 8))
    tn = _pick(n, (512, 256, 128) if precise else (1024, 768, 512, 256, 128))
    return pl.pallas_call(
        functools.partial(_mm_kernel, precise=precise),
        grid=(n // tn, m // tm),
        in_specs=[pl.BlockSpec((tm, k), lambda j, i: (i, 0)),
                  pl.BlockSpec((k, tn), lambda j, i: (0, j))],
        out_specs=pl.BlockSpec((tm, tn), lambda j, i: (i, j)),
        out_shape=jax.ShapeDtypeStruct((m, n), F32),
        compiler_params=_cparams("parallel", "parallel"),
        name="mm",
    )(x, w)


def _mm2_kernel(x1_ref, x2_ref, w1_ref, w2_ref, o_ref, *, precise):
    o_ref[...] = _mdot(x1_ref[...], w1_ref[...], precise) + _mdot(x2_ref[...], w2_ref[...], precise)


def mm_cat2(x1, x2, w, precise=False):
    m, k1 = x1.shape
    k2 = x2.shape[1]
    assert k1 == k2 and w.shape[0] == k1 + k2
    n = w.shape[1]
    tm = _pick(m, (512, 256, 128, 64, 32, 16, 8))
    tn = _pick(n, (512, 256, 128) if precise else (1024, 512, 256, 128))
    return pl.pallas_call(
        functools.partial(_mm2_kernel, precise=precise),
        grid=(n // tn, m // tm),
        in_specs=[pl.BlockSpec((tm, k1), lambda j, i: (i, 0)),
                  pl.BlockSpec((tm, k2), lambda j, i: (i, 0)),
                  pl.BlockSpec((k1, tn), lambda j, i: (0, j)),
                  pl.BlockSpec((k2, tn), lambda j, i: (1, j))],
        out_specs=pl.BlockSpec((tm, tn), lambda j, i: (i, j)),
        out_shape=jax.ShapeDtypeStruct((m, n), F32),
        compiler_params=_cparams("parallel", "parallel"),
        name="mm_cat2",
    )(x1, x2, w, w)


def _ln_res_kernel(x_ref, y_ref, g_ref, b_ref, o_ref):
    z = DN_ALPHA * x_ref[...] + y_ref[...]
    mu = jnp.mean(z, axis=-1, keepdims=True)
    zc = z - mu
    var = jnp.mean(zc * zc, axis=-1, keepdims=True)
    o_ref[...] = zc * lax.rsqrt(var + LN_EPS) * g_ref[...] + b_ref[...]


def ln_res(x, y, g, b):
    m, d = x.shape
    tm = _pick(m, (256, 192, 128, 64, 32, 16, 8))
    row = pl.BlockSpec((tm, d), lambda i: (i, 0))
    vec = pl.BlockSpec((1, d), lambda i: (0, 0))
    return pl.pallas_call(
        _ln_res_kernel,
        grid=(m // tm,),
        in_specs=[row, row, vec, vec],
        out_specs=row,
        out_shape=jax.ShapeDtypeStruct((m, d), F32),
        compiler_params=_cparams("parallel"),
        name="ln_res",
    )(x, y, g.reshape(1, d), b.reshape(1, d))


def _shifted(carry_ref, x, j):
    ext = jnp.concatenate([carry_ref[...], x], axis=0)
    return pltpu.roll(ext, j, axis=0)[SUBLANES:, :]


def _rwkv_prep_kernel(p_ref, sh0_ref, mu_ref, w0_ref, a0_ref, wlora_ref, alora_ref, gup_ref,
                      kk_ref, ka_ref, rk_ref, bd_ref, *refs, chunked, precise):
    outs, carry = refs[:-1], refs[-1]
    ti = pl.program_id(1)

    @pl.when(ti == 0)
    def _():
        carry[...] = jnp.broadcast_to(sh0_ref[0], carry.shape)

    p = p_ref[...]
    tm = p.shape[0]
    p_prev = _shifted(carry, p, 1)
    carry[...] = p[tm - SUBLANES:, :]
    pm = p + mu_ref[...] * (p_prev - p)
    W = RWKV_W
    r = pm[:, 0:W]
    k = pm[:, W:2 * W]
    v = pm[:, 2 * W:3 * W]
    lo = pm[:, 3 * W:3 * W + LORA_DECAY + LORA_A]
    gd = pm[:, 3 * W + LORA_DECAY + LORA_A:RWKV_SHIFT_W]
    lane = lax.broadcasted_iota(jnp.int32, lo.shape, 1)
    z = jnp.where(lane < LORA_DECAY, jnp.tanh(lo), lo)
    w_l = _mdot(z, wlora_ref[...], precise)
    a_l = _mdot(z, alora_ref[...], precise)
    w = -_softplus(-(w0_ref[...] + w_l)) - 0.5
    log_decay = -jnp.exp(w)
    decay = jnp.exp(log_decay)
    a = _sigmoid(a0_ref[...] + a_l)
    g = _mdot(_sigmoid(gd), gup_ref[...], precise)
    bd = bd_ref[...]
    kk = k * kk_ref[...]
    ss = _dot2(kk * kk, bd)
    kn = kk * lax.rsqrt(jnp.maximum(ss, 1e-24))
    k_eff = k * (1.0 + (a - 1.0) * ka_ref[...])
    bvec = kn * a
    bonus = _dot2(r * k_eff * rk_ref[...], bd) * v
    if chunked:
        vals = (r, log_decay, k_eff, kn, bvec, v, g, bonus)
    else:
        vals = (r, decay, k_eff, kn, bvec, v, decay * r, _dot2(bvec * r, bd), _dot2(k_eff * r, bd), g, bonus)
    for o_ref, val in zip(outs, vals):
        o_ref[...] = val


def rwkv_prep(p_rw, sh0, B, T, prm, chunked, precise):
    n_out = 8 if chunked else 11
    tm = _pick(T, (256, 128, 64, 32, 16, 8))
    nt = T // tm
    W = RWKV_W
    row_in = pl.BlockSpec((tm, RWKV_PAD_W), lambda b, t: (b * nt + t, 0))
    row_out = pl.BlockSpec((tm, W), lambda b, t: (b * nt + t, 0))
    full = lambda a: pl.BlockSpec(a.shape, lambda b, t: (0,) * a.ndim)
    sh0p = sh0.reshape(B, 1, RWKV_PAD_W)
    consts = [prm["mu"], prm["w0"], prm["a0"], prm["wlora"], prm["alora"], prm["gup"],
              prm["k_k"], prm["k_a"], prm["r_k"], prm["bd"]]
    outs = pl.pallas_call(
        functools.partial(_rwkv_prep_kernel, chunked=chunked, precise=precise),
        grid=(B, nt),
        in_specs=[row_in, pl.BlockSpec((1, 1, RWKV_PAD_W), lambda b, t: (b, 0, 0))] + [full(c) for c in consts],
        out_specs=[row_out] * n_out,
        out_shape=[jax.ShapeDtypeStruct((B * T, W), F32)] * n_out,
        scratch_shapes=[pltpu.VMEM((SUBLANES, RWKV_PAD_W), F32)],
        compiler_params=_cparams("arbitrary", "arbitrary"),
        name="rwkv_prep",
    )(p_rw, sh0p, *consts)
    return outs


PAIR_GROUP = 4
RWKV_SCAN_BATCH = 2


def _rwkv_scan_kernel(r_ref, w_ref, k_ref, kn_ref, b_ref, v_ref, wr_ref, br_ref, kr_ref, s0_ref, oseg_ref,
                      y_ref, sfin_ref, s_scr, *, nb, tc):
    ti = pl.program_id(1)

    @pl.when(ti == 0)
    def _():
        s_scr[...] = s0_ref[...]

    row = lax.broadcasted_iota(jnp.int32, (RWKV_HD, LANES), 0)
    lane = lax.broadcasted_iota(jnp.int32, (RWKV_HD, LANES), 1)
    diag = (lane & (RWKV_HD - 1)) == row
    oseg = oseg_ref[...]
    n_pairs = nb * PAIRS
    R = RWKV_HD

    def step8(t8, carry):
        base = pl.multiple_of(t8 * SUBLANES, SUBLANES)
        for g0 in range(0, n_pairs, PAIR_GROUP):
            ids = [(i // PAIRS, i % PAIRS) for i in range(g0, g0 + PAIR_GROUP)]
            n = len(ids)
            tile = lambda ref, b, p: ref[b, pl.ds(base, SUBLANES), p * LANES:(p + 1) * LANES]
            tiles = {nm: [tile(ref, b, p) for (b, p) in ids]
                     for nm, ref in (("kn", kn_ref), ("wr", wr_ref), ("v", v_ref), ("w", w_ref), ("b", b_ref),
                                     ("k", k_ref), ("br", br_ref), ("kr", kr_ref))}
            s_cur = [s_scr[b * PAIRS + p] for (b, p) in ids]
            y_rows = [[] for _ in ids]
            for j in range(SUBLANES):
                row = lambda nm, i: tiles[nm][i][j:j + 1, :]
                blocks = ([s_cur[i] * row("kn", i) for i in range(n)]
                          + [s_cur[i] * row("wr", i) for i in range(n)]
                          + [jnp.where(diag, row("v", i), 0.0) for i in range(n)])
                red = _dot2(jnp.concatenate(blocks, axis=0), oseg)
                for i in range(n):
                    skk = red[i * R:(i + 1) * R]
                    u = red[(n + i) * R:(n + i + 1) * R]
                    vb = red[(2 * n + i) * R:(2 * n + i + 1) * R]
                    s_cur[i] = s_cur[i] * row("w", i) - skk * row("b", i) + vb * row("k", i)
                    y_bc = u - skk * row("br", i) + vb * row("kr", i)
                    y_rows[i].append(jnp.sum(jnp.where(diag, y_bc, 0.0), axis=0, keepdims=True))
            for i, (b, p) in enumerate(ids):
                s_scr[b * PAIRS + p] = s_cur[i]
                y_ref[b, pl.ds(base, SUBLANES), p * LANES:(p + 1) * LANES] = jnp.concatenate(y_rows[i], axis=0)
        return carry

    lax.fori_loop(0, tc // SUBLANES, step8, 0)

    @pl.when(ti == pl.num_programs(1) - 1)
    def _():
        sfin_ref[...] = s_scr[...]


def rwkv_scan(seqs, s0, B, T):
    tc = _pick(T, (128, 64, 32, 16, 8))
    nb = RWKV_SCAN_BATCH
    assert B % nb == 0
    views = [s.reshape(B, T, RWKV_W) for s in seqs]
    s0p = s0.reshape(B, PAIRS, 2, RWKV_HD, RWKV_HD).transpose(0, 1, 3, 2, 4).reshape(B * PAIRS, RWKV_HD, LANES)
    li = jnp.arange(LANES)
    oseg = (li[:, None] // RWKV_HD == li[None, :] // RWKV_HD).astype(BF16)
    seq_spec = pl.BlockSpec((nb, tc, RWKV_W), lambda b, t: (b, t, 0))
    st_spec = pl.BlockSpec((nb * PAIRS, RWKV_HD, LANES), lambda b, t: (b, 0, 0))
    y, sfin = pl.pallas_call(
        functools.partial(_rwkv_scan_kernel, nb=nb, tc=tc),
        grid=(B // nb, T // tc),
        in_specs=[seq_spec] * 9 + [st_spec, pl.BlockSpec((LANES, LANES), lambda b, t: (0, 0))],
        out_specs=[seq_spec, st_spec],
        out_shape=[jax.ShapeDtypeStruct((B, T, RWKV_W), F32),
                   jax.ShapeDtypeStruct((B * PAIRS, RWKV_HD, LANES), F32)],
        scratch_shapes=[pltpu.VMEM((nb * PAIRS, RWKV_HD, LANES), F32)],
        compiler_params=_cparams("arbitrary", "arbitrary"),
        name="rwkv_scan",
    )(*views, s0p, oseg)
    sfin = sfin.reshape(B, PAIRS, RWKV_HD, 2, RWKV_HD).transpose(0, 1, 3, 2, 4).reshape(B, RWKV_HEADS, RWKV_HD, RWKV_HD)
    return y.reshape(B * T, RWKV_W), sfin


RWKV_CHUNK = 64
CHUNK_PAIRS = 8


def _split_bf(x):
    hi = x.astype(BF16)
    return hi, (x - hi.astype(F32)).astype(BF16)


def _dot3(a, b):
    ah, al = _split_bf(a)
    bh, bl = _split_bf(b)
    return _dot(ah, bh) + _dot(al, bh) + _dot(ah, bl)


def _dot3_nt(a, b):
    ah, al = _split_bf(a)
    bh, bl = _split_bf(b)
    return _dot_nt(ah, bh) + _dot_nt(al, bh) + _dot_nt(ah, bl)


def _rwkv_chunk_math(sts, rs, lws, ks, kns, bs, vs):
    C = RWKV_CHUNK
    lane = lax.broadcasted_iota(jnp.int32, (C, LANES), 1)
    row = lax.broadcasted_iota(jnp.int32, (C, LANES), 0)
    head0 = lane < RWKV_HD
    by_head = lambda x: jnp.concatenate([jnp.where(head0, x, 0.0), jnp.where(head0, 0.0, x)], axis=0)
    each = lambda f, *ls: [f(*a) for a in zip(*ls)]

    def cumsum_rows(lw):
        cs, d = lw, 1
        while d < C:
            cs = cs + jnp.where(row >= d, pltpu.roll(cs, d, axis=0), 0.0)
            d *= 2
        return cs

    css = each(cumsum_rows, lws)
    totals = [cs[C - 1:C, :] for cs in css]
    kps = each(lambda kn, cs, lw: kn * jnp.exp(cs - lw), kns, css, lws)
    rps = each(lambda r, cs: r * jnp.exp(cs), rs, css)
    e_negs = [jnp.exp(-cs) for cs in css]
    e_rems = each(lambda t, cs: jnp.exp(t - cs), totals, css)
    grams = each(lambda kp, rp, k, b, e: _dot3_nt(jnp.concatenate([kp, rp], axis=0),
                                                  jnp.concatenate([by_head(k * e), by_head(b * e)], axis=0)),
                 kps, rps, ks, bs, e_negs)
    s_idx = lane & (C - 1)
    strict, incl = s_idx < row, s_idx <= row
    a_ks = [jnp.where(strict, g[:C, :LANES], 0.0) for g in grams]
    a_bs = [jnp.where(strict, g[:C, LANES:], 0.0) for g in grams]
    d_ks = [jnp.where(incl, g[C:, :LANES], 0.0) for g in grams]
    d_bs = [jnp.where(incl, g[C:, LANES:], 0.0) for g in grams]
    r2 = lax.broadcasted_iota(jnp.int32, (2 * C, LANES), 0)
    l2 = lax.broadcasted_iota(jnp.int32, (2 * C, LANES), 1)
    eye = r2 == l2
    n_pows = [-by_head(a) for a in a_bs]
    t_invs = [jnp.where(eye, 1.0, 0.0) + n for n in n_pows]
    p = 2
    while p < C:
        n_pows = each(lambda n: _dot3(n, n), n_pows)
        t_invs = each(lambda t, n: t + _dot3(t, n), t_invs, n_pows)
        p *= 2
    v_bhs = [by_head(v) for v in vs]
    rhss = each(lambda kp, a, st, vb: _dot3(jnp.concatenate([kp, a], axis=1), jnp.concatenate([st, vb], axis=0)),
                kps, a_ks, sts, v_bhs)
    zs = each(lambda t, rhs: _dot3(t[:C] + t[C:], by_head(rhs)), t_invs, rhss)
    ys = each(lambda rp, dk, db, st, vb, z: _dot3(jnp.concatenate([rp, dk, -db], axis=1),
                                                  jnp.concatenate([st, vb, by_head(z)], axis=0)),
              rps, d_ks, d_bs, sts, v_bhs, zs)
    same_head = (r2 < RWKV_HD) == (l2 < RWKV_HD)

    def new_state(st, k, b, e, total, v, z):
        g_total = jnp.where(eye, jnp.broadcast_to(jnp.exp(total), (2 * C, LANES)), 0.0)
        kb_t = jnp.transpose(jnp.concatenate([k * e, b * e], axis=0))
        upd = _dot3(kb_t, jnp.concatenate([v, -z], axis=0))
        return _dot3(g_total, st) + jnp.where(same_head, upd, 0.0)

    return ys, each(new_state, sts, ks, bs, e_rems, totals, vs, zs)


def _rwkv_chunk_kernel(r_ref, lw_ref, k_ref, kn_ref, b_ref, v_ref, s0_ref, y_ref, sfin_ref, s_scr):
    ci = pl.program_id(2)

    @pl.when(ci == 0)
    def _():
        s_scr[...] = s0_ref[...]

    sls = [slice(i * LANES, (i + 1) * LANES) for i in range(CHUNK_PAIRS)]
    cols = lambda ref: [ref[:, sl] for sl in sls]
    ys, sts = _rwkv_chunk_math([s_scr[i] for i in range(CHUNK_PAIRS)], cols(r_ref), cols(lw_ref), cols(k_ref),
                               cols(kn_ref), cols(b_ref), cols(v_ref))
    for i, sl in enumerate(sls):
        y_ref[:, sl] = ys[i]
        s_scr[i] = sts[i]

    @pl.when(ci == pl.num_programs(2) - 1)
    def _():
        sfin_ref[...] = s_scr[...]


def rwkv_scan_chunked(seqs, s0, B, T):
    C, PP = RWKV_CHUNK, CHUNK_PAIRS
    assert T % C == 0 and PAIRS % PP == 0
    npg = PAIRS // PP
    views = [s.reshape(B, T, RWKV_W) for s in seqs]
    s0t = s0.reshape(B, PAIRS, 2, RWKV_HD, RWKV_HD).transpose(0, 1, 2, 4, 3)
    eye2 = jnp.eye(2, dtype=F32)
    s0bd = (s0t[:, :, :, :, None, :] * eye2[None, None, :, None, :, None]).reshape(B * PAIRS, LANES, LANES)
    seq_spec = pl.BlockSpec((None, C, PP * LANES), lambda b, g, c: (b, c, g))
    st_spec = pl.BlockSpec((PP, LANES, LANES), lambda b, g, c: (b * npg + g, 0, 0))
    y, sfin = pl.pallas_call(
        _rwkv_chunk_kernel,
        grid=(B, npg, T // C),
        in_specs=[seq_spec] * 6 + [st_spec],
        out_specs=[seq_spec, st_spec],
        out_shape=[jax.ShapeDtypeStruct((B, T, RWKV_W), F32),
                   jax.ShapeDtypeStruct((B * PAIRS, LANES, LANES), F32)],
        scratch_shapes=[pltpu.VMEM((PP, LANES, LANES), F32)],
        compiler_params=_cparams("parallel", "parallel", "arbitrary"),
        name="rwkv_chunk",
    )(*views, s0bd)
    sf = sfin.reshape(B, PAIRS, 2, RWKV_HD, 2, RWKV_HD)
    sf = jnp.stack([sf[:, :, 0, :, 0, :], sf[:, :, 1, :, 1, :]], axis=2)
    return y.reshape(B * T, RWKV_W), sf.transpose(0, 1, 2, 4, 3).reshape(B, RWKV_HEADS, RWKV_HD, RWKV_HD)


def _rwkv_post_kernel(y_ref, bonus_ref, g_ref, gng_ref, gnb_ref, bd_ref, o_ref):
    y = y_ref[...]
    bd = bd_ref[...]
    inv = 1.0 / RWKV_HD
    mean = _dot2(y, bd) * inv
    yc = y - mean
    var = _dot2(yc * yc, bd) * inv
    yn = yc * lax.rsqrt(var + GN_EPS) * gng_ref[...] + gnb_ref[...]
    o_ref[...] = (yn + bonus_ref[...]) * g_ref[...]


def rwkv_post(y, bonus, g, gn_g, gn_b, bd):
    m, W = y.shape
    tm = _pick(m, (256, 128, 64, 32, 16, 8))
    row = pl.BlockSpec((tm, W), lambda i: (i, 0))
    vec = pl.BlockSpec((1, W), lambda i: (0, 0))
    return pl.pallas_call(
        _rwkv_post_kernel,
        grid=(m // tm,),
        in_specs=[row, row, row, vec, vec, pl.BlockSpec((W, W), lambda i: (0, 0))],
        out_specs=row,
        out_shape=jax.ShapeDtypeStruct((m, W), F32),
        compiler_params=_cparams("parallel"),
        name="rwkv_post",
    )(y, bonus, g, gn_g.reshape(1, W), gn_b.reshape(1, W), bd)


def _gelu_tanh(x):
    return 0.5 * x * (1.0 + jnp.tanh(math.sqrt(2.0 / math.pi) * (x + 0.044715 * (x * x * x))))


def _neg_expm1(x):
    series = -x * (1.0 + x * (1.0 / 2.0) * (1.0 + x * (1.0 / 3.0) * (1.0 + x * (1.0 / 4.0) * (1.0 + x * (1.0 / 5.0)
             * (1.0 + x * (1.0 / 6.0))))))
    return jnp.where(x > -0.05, series, 1.0 - jnp.exp(x))


def _lru_kernel(px_ref, pg_ref, c0_ref, h0_ref, cw_ref, cb_ref, wa_ref, ba_ref, wx_ref, bx_ref, lam_ref,
                y_ref, hlast_ref, carry, hcar, *, precise):
    ti = pl.program_id(1)

    @pl.when(ti == 0)
    def _():
        carry[...] = c0_ref[0]
        hcar[...] = h0_ref[0]

    x = px_ref[...]
    tm = x.shape[0]
    cw = cw_ref[...]
    xc = cb_ref[...] + cw[CONV_W - 1:CONV_W, :] * x
    for j in range(1, CONV_W):
        xc = xc + cw[CONV_W - 1 - j:CONV_W - j, :] * _shifted(carry, x, j)
    carry[...] = x[tm - SUBLANES:, :]
    xcm = xc if precise else xc.astype(BF16)
    r = _sigmoid(_mdot(xcm, wa_ref[...], precise) + ba_ref[...])
    i = _sigmoid(_mdot(xcm, wx_ref[...], precise) + bx_ref[...])
    log_a = (-LRU_C * _softplus(-lam_ref[...])) * r
    a = jnp.exp(log_a)
    b = jnp.sqrt(_neg_expm1(2.0 * log_a)) * (i * xc)
    rows = lax.broadcasted_iota(jnp.int32, a.shape, 0)
    d = 1
    while d < tm:
        a_sh = pltpu.roll(a, d, axis=0)
        b_sh = pltpu.roll(b, d, axis=0)
        keep = rows >= d
        b = jnp.where(keep, a * b_sh + b, b)
        a = jnp.where(keep, a * a_sh, a)
        d *= 2
    h = a * hcar[...] + b
    hcar[...] = h[tm - 1:tm, :]
    hlast_ref[0] = h[tm - 1:tm, :]
    y_ref[...] = h * _gelu_tanh(pg_ref[...])


def lru_mix(p_l, c0, h0, B, T, prm, precise):
    W = LRU_W
    tm = _pick(T, (256, 128, 64, 32, 16, 8))
    nt = T // tm
    c0p = jnp.concatenate([jnp.zeros((B, SUBLANES - (CONV_W - 1), W), F32), c0], axis=1)
    h0p = h0.reshape(B, 1, W)
    full = lambda a: pl.BlockSpec(a.shape, lambda b, t: (0,) * a.ndim)
    consts = [prm["conv_w"], prm["conv_b"], prm["wa"], prm["ba"], prm["wx"], prm["bx"], prm["lam"]]
    y, hlast = pl.pallas_call(
        functools.partial(_lru_kernel, precise=precise),
        grid=(B, nt),
        in_specs=[pl.BlockSpec((tm, W), lambda b, t: (b * nt + t, 0)),
                  pl.BlockSpec((tm, W), lambda b, t: (b * nt + t, 1)),
                  pl.BlockSpec((1, SUBLANES, W), lambda b, t: (b, 0, 0)),
                  pl.BlockSpec((1, 1, W), lambda b, t: (b, 0, 0))] + [full(c) for c in consts],
        out_specs=[pl.BlockSpec((tm, W), lambda b, t: (b * nt + t, 0)),
                   pl.BlockSpec((1, 1, W), lambda b, t: (b, 0, 0))],
        out_shape=[jax.ShapeDtypeStruct((B * T, W), F32), jax.ShapeDtypeStruct((B, 1, W), F32)],
        scratch_shapes=[pltpu.VMEM((SUBLANES, W), F32), pltpu.VMEM((1, W), F32)],
        compiler_params=_cparams("arbitrary", "arbitrary"),
        name="lru_mix",
    )(p_l, p_l, c0p, h0p, *consts)
    return y, hlast.reshape(B, W)


def _block_diag(w):
    n, d, e = w.shape
    eye = jnp.eye(n, dtype=w.dtype)
    return (eye[:, None, :, None] * w[:, :, None, :]).reshape(n * d, n * e)


def rec_params(li, rec_w_in, rec_mu, rwkv_w0, rwkv_w_up, rwkv_a0, rwkv_a_up, rwkv_g_up, rwkv_k_k, rwkv_k_a,
               rwkv_r_k, rwkv_gn_g, rwkv_gn_b, lru_conv_w, lru_conv_b, lru_wa, lru_ba, lru_wx, lru_bx, lru_lambda,
               rec_w_out):
    W = RWKV_W
    pad = RWKV_PAD_W - RWKV_SHIFT_W
    hi = jnp.arange(W) // RWKV_HD
    zl = jnp.zeros((LORA_DECAY, W), F32)
    return dict(
        w_rw=jnp.pad(rec_w_in[li][:, :RWKV_SHIFT_W], ((0, 0), (0, pad))),
        w_l=rec_w_in[li][:, RWKV_SHIFT_W:],
        mu=jnp.pad(rec_mu[li], (0, pad)).reshape(1, RWKV_PAD_W),
        w0=rwkv_w0[li].reshape(1, W), a0=rwkv_a0[li].reshape(1, W),
        wlora=jnp.concatenate([rwkv_w_up[li], zl], axis=0),
        alora=jnp.concatenate([zl, rwkv_a_up[li]], axis=0),
        gup=rwkv_g_up[li],
        k_k=rwkv_k_k[li].reshape(1, W), k_a=rwkv_k_a[li].reshape(1, W), r_k=rwkv_r_k[li].reshape(1, W),
        bd=(hi[:, None] == hi[None, :]).astype(BF16),
        gn_g=rwkv_gn_g[li], gn_b=rwkv_gn_b[li],
        conv_w=lru_conv_w[li], conv_b=lru_conv_b[li].reshape(1, LRU_W),
        wa=_block_diag(lru_wa[li]), ba=lru_ba[li].reshape(1, LRU_W),
        wx=_block_diag(lru_wx[li]), bx=lru_bx[li].reshape(1, LRU_W),
        lam=lru_lambda[li].reshape(1, LRU_W),
        w_out=rec_w_out[li],
    )


def rec_mixer(x2d, B, T, S0, sh0, h0, c0, prm, precise=False):
    p_rw = mm(x2d, prm["w_rw"], precise)
    p_l = mm(x2d, prm["w_l"], precise)
    sh0p = jnp.pad(sh0, ((0, 0), (0, RWKV_PAD_W - RWKV_SHIFT_W)))
    chunked = T % RWKV_CHUNK == 0
    outs = rwkv_prep(p_rw, sh0p, B, T, prm, chunked, precise)
    g, bonus = outs[-2:]
    if chunked:
        y_raw, s_fin = rwkv_scan_chunked(outs[:6], S0, B, T)
    else:
        y_raw, s_fin = rwkv_scan(outs[:9], S0, B, T)
    y_rw = rwkv_post(y_raw, bonus, g, prm["gn_g"], prm["gn_b"], prm["bd"])
    y_lru, h_last = lru_mix(p_l, c0, h0, B, T, prm, precise)
    y = mm_cat2(y_rw, y_lru, prm["w_out"], precise)
    sh = p_rw.reshape(B, T, RWKV_PAD_W)[:, -1, :RWKV_SHIFT_W]
    px = p_l.reshape(B, T, 2 * LRU_W)[:, :, :LRU_W]
    cbuf = jnp.concatenate([c0, px], axis=1)[:, -(CONV_W - 1):] if T < CONV_W - 1 else px[:, -(CONV_W - 1):]
    return y, s_fin, sh, h_last, cbuf


def _dot_nt(a, b):
    return lax.dot_general(a, b, (((1,), (1,)), ((), ())), preferred_element_type=F32)


def _compress_kernel(slab_ref, w_ref, o_ref, *, n_pieces, precise):
    acc = None
    for l in range(CMP_STRIDE):
        d = _mdot(slab_ref[pl.ds(l, n_pieces, stride=CMP_STRIDE), :], w_ref[l], precise)
        acc = d if acc is None else acc + d
    out = acc[:, :HEAD_DIM] + pltpu.roll(acc[:, HEAD_DIM:], n_pieces - 1, axis=0)
    o_ref[0:n_pieces, :] = out
    pad = o_ref.shape[0] - n_pieces
    if pad:
        o_ref[n_pieces:, :] = jnp.zeros((pad, HEAD_DIM), F32)


def nsa_compress(kv2d, col0, B, t_kv, wcat, ncp, precise):
    n_pieces = t_kv // CMP_STRIDE
    return pl.pallas_call(
        functools.partial(_compress_kernel, n_pieces=n_pieces, precise=precise),
        grid=(B, 2, N_KV),
        in_specs=[pl.BlockSpec((t_kv, HEAD_DIM), lambda b, c, g: (b, col0 + c * N_KV + g)),
                  pl.BlockSpec((None, CMP_STRIDE, HEAD_DIM, 2 * HEAD_DIM), lambda b, c, g: (c, 0, 0, 0))],
        out_specs=pl.BlockSpec((None, None, None, ncp, HEAD_DIM), lambda b, c, g: (b, c, g, 0, 0)),
        out_shape=jax.ShapeDtypeStruct((B, 2, N_KV, ncp, HEAD_DIM), F32),
        compiler_params=_cparams("parallel", "parallel", "parallel"),
        name="nsa_compress",
    )(kv2d, wcat)


MASKED_DIST = 1e30
EXP2_SCALE = SCALE * math.log2(math.e)


def _parts(a, precise):
    return _split_bf(a) if precise else (a.astype(BF16),)


def _pdot(a, b, nt=False):
    dot = _dot_nt if nt else _dot
    out = dot(a[0], b[0])
    if len(a) > 1:
        out = out + dot(a[1], b[0]) + dot(a[0], b[1])
    return out


def _online_softmax_step(q, k, v, cdist, m, l, acc, precise):
    u = _pdot(q, k, nt=True) - cdist
    m_new = jnp.maximum(m, jnp.max(u, axis=-1, keepdims=True))
    alpha = jnp.exp2((m - m_new) * EXP2_SCALE)
    p = jnp.exp2((u - m_new) * EXP2_SCALE)
    l = alpha * l + jnp.sum(p, axis=-1, keepdims=True)
    acc = alpha * acc + _pdot(_parts(p, precise), v)
    return m_new, l, acc


def _nsa_kernel(q_ref, kc_ref, vc_ref, ks_ref, vs_ref, kw_ref, vw_ref, gate_ref, gb_ref, ov_ref, o_ref, *,
                tq, q_off, nc_real, ns_real, tk, n_kv_static, w_off, w_len, w_follows_q, precise):
    g = pl.program_id(1)
    qi = pl.program_id(2)
    q0 = qi * tq
    ncp = kc_ref.shape[0]
    nsp = ov_ref.shape[1]
    qpos_col = q_off + q0 + lax.broadcasted_iota(jnp.int32, (tq, 1), 0)
    q_all = q_ref[...]
    q_bf = [_parts(q_all[:, r * HEAD_DIM:(r + 1) * HEAD_DIM], precise) for r in range(GQA_R)]
    head_f = (g * GQA_R).astype(F32)
    slopes = [jnp.exp2(-0.5 * (jnp.full((1, 1), 1.0 + r, F32) + head_f)) for r in range(GQA_R)]

    n_idx = lax.broadcasted_iota(jnp.int32, (tq, ncp), 1)
    dist_c = qpos_col - (n_idx * CMP_STRIDE + (CMP_BLOCK - 1))
    ok_c = (dist_c >= 0) & (n_idx < nc_real)
    dist_cf = dist_c.astype(F32)
    kc_bf = _parts(kc_ref[...], precise)
    vc_bf = _parts(vc_ref[...], precise)
    o_c = []
    p_sum = None
    for r in range(GQA_R):
        s = _pdot(q_bf[r], kc_bf, nt=True) * SCALE - slopes[r] * dist_cf
        s = jnp.where(ok_c, s, NEG)
        m = jnp.max(s, axis=-1, keepdims=True)
        e = jnp.where(ok_c, jnp.exp(s - m), 0.0)
        p = e / jnp.maximum(jnp.sum(e, axis=-1, keepdims=True), 1e-30)
        o_c.append(_pdot(_parts(p, precise), vc_bf))
        p_sum = p if p_sum is None else p_sum + p
    imp = _dot2(p_sum, ov_ref[...])
    s_idx = lax.broadcasted_iota(jnp.int32, (tq, nsp), 1)
    cur = lax.shift_right_logical(qpos_col, int(math.log2(SLC_BLOCK)))
    forced = (s_idx == 0) | (s_idx == cur) | (s_idx == cur - 1)
    causal_blk = (s_idx * SLC_BLOCK <= qpos_col) & (s_idx < ns_real)
    score = jnp.where(causal_blk, imp + jnp.where(forced, BIG, 0.0), NEG)
    rank = jnp.zeros((tq, nsp), jnp.int32)
    for j in range(ns_real):
        col = score[:, j:j + 1]
        ahead = (col > score) | ((col == score) & (s_idx > j))
        rank = rank + jnp.where(ahead, 1, 0)
    sel = (rank < min(TOP_N, ns_real)) & (score > 0.5 * NEG)
    sel_bf = jnp.where(sel, 1.0, 0.0).astype(BF16)

    def init():
        return tuple((jnp.full((tq, 1), NEG, F32), jnp.zeros((tq, 1), F32), jnp.zeros((tq, HEAD_DIM), F32))
                     for _ in range(GQA_R))

    blk_per_tile = tk // SLC_BLOCK
    ex_row = lax.broadcasted_iota(jnp.int32, (nsp, tk), 0)
    ex_col = lax.shift_right_logical(lax.broadcasted_iota(jnp.int32, (nsp, tk), 1), int(math.log2(SLC_BLOCK)))
    key_i = lax.broadcasted_iota(jnp.int32, (tq, tk), 1)

    cslopes = [sl * (1.0 / SCALE) for sl in slopes]

    def slc_body(kt, carry):
        base = pl.multiple_of(kt * tk, tk)
        expand = jnp.where(ex_row == ex_col + kt * blk_per_tile, 1.0, 0.0).astype(BF16)
        selexp = _dot(sel_bf, expand)
        dist = qpos_col - (key_i + base)
        distm = jnp.where(selexp > 0.5, jnp.where(dist >= 0, dist.astype(F32), MASKED_DIST), MASKED_DIST)
        k_p = _parts(ks_ref[pl.ds(base, tk), :], precise)
        v_p = _parts(vs_ref[pl.ds(base, tk), :], precise)
        return tuple(_online_softmax_step(q_bf[r], k_p, v_p, cslopes[r] * distm, *carry[r], precise)
                     for r in range(GQA_R))

    if n_kv_static is None:
        n_kv = (q0 + tq + tk - 1) // tk
    else:
        n_kv = n_kv_static
    st_s = lax.fori_loop(0, n_kv, slc_body, init())

    if w_follows_q:
        w_base = pl.multiple_of(jnp.maximum(q0 + tq - w_len, 0), SUBLANES)
    else:
        w_base = 0
    dist_w = qpos_col - (w_off + w_base + lax.broadcasted_iota(jnp.int32, (tq, w_len), 1))
    ok_w = jnp.where(dist_w >= 0, dist_w, WINDOW) < WINDOW
    dist_wm = jnp.where(ok_w, dist_w.astype(F32), MASKED_DIST)
    kw_p = _parts(kw_ref[pl.ds(w_base, w_len), :], precise)
    vw_p = _parts(vw_ref[pl.ds(w_base, w_len), :], precise)
    st_w = tuple(_online_softmax_step(q_bf[r], kw_p, vw_p, cslopes[r] * dist_wm, *init()[r], precise)
                 for r in range(GQA_R))

    gates = _sigmoid(gate_ref[...] + gb_ref[...])
    lane = lax.broadcasted_iota(jnp.int32, gates.shape, 1)
    for r in range(GQA_R):
        head = g * GQA_R + r
        gcol = lambda br: jnp.sum(jnp.where(lane == br * N_HEADS + head, gates, 0.0), axis=-1, keepdims=True)
        o_s = st_s[r][2] / jnp.maximum(st_s[r][1], 1e-30)
        o_w = st_w[r][2] / jnp.maximum(st_w[r][1], 1e-30)
        o_ref[:, r * HEAD_DIM:(r + 1) * HEAD_DIM] = gcol(0) * o_c[r] + gcol(1) * o_s + gcol(2) * o_w


def nsa_attention(q2d, kcvc, kv2d, kv_col0, win2d, win_col0, gates, gate_b, overlap, *, B, t_q, t_kv, t_win, tq,
                  q_off, nc_real, ns_real, tk, n_kv_static, w_off, w_len, w_follows_q, precise):
    nq = t_q // tq
    ncp = kcvc.shape[3]
    nsp = overlap.shape[1]
    qw = GQA_R * HEAD_DIM
    slab = lambda arr_t, col: pl.BlockSpec((arr_t, HEAD_DIM), col)
    kern = functools.partial(_nsa_kernel, tq=tq, q_off=q_off, nc_real=nc_real, ns_real=ns_real, tk=tk,
                             n_kv_static=n_kv_static, w_off=w_off, w_len=w_len, w_follows_q=w_follows_q,
                             precise=precise)
    return pl.pallas_call(
        kern,
        grid=(B, N_KV, nq),
        in_specs=[pl.BlockSpec((tq, qw), lambda b, g, i: (b * nq + i, g)),
                  pl.BlockSpec((None, None, None, ncp, HEAD_DIM), lambda b, g, i: (b, 0, g, 0, 0)),
                  pl.BlockSpec((None, None, None, ncp, HEAD_DIM), lambda b, g, i: (b, 1, g, 0, 0)),
                  slab(t_kv, lambda b, g, i: (b, kv_col0 + 2 * N_KV + g)),
                  slab(t_kv, lambda b, g, i: (b, kv_col0 + 3 * N_KV + g)),
                  slab(t_win, lambda b, g, i: (b, win_col0 + g)),
                  slab(t_win, lambda b, g, i: (b, win_col0 + N_KV + g)),
                  pl.BlockSpec((tq, LANES), lambda b, g, i: (b * nq + i, 0)),
                  pl.BlockSpec((1, LANES), lambda b, g, i: (0, 0)),
                  pl.BlockSpec((ncp, nsp), lambda b, g, i: (0, 0))],
        out_specs=pl.BlockSpec((tq, qw), lambda b, g, i: (b * nq + i, g)),
        out_shape=jax.ShapeDtypeStruct((B * t_q, N_HEADS * HEAD_DIM), F32),
        compiler_params=_cparams("parallel", "parallel", "arbitrary"),
        name="nsa_attention",
    )(q2d, kcvc, kcvc, kv2d, kv2d, win2d, win2d, gates, gate_b, overlap)


def _overlap_matrix(ncp, nsp, nc_real, ns_real):
    n = jnp.arange(ncp)[:, None]
    s = jnp.arange(nsp)[None, :]
    c0 = n * CMP_STRIDE
    s0 = s * SLC_BLOCK
    ov = (c0 < s0 + SLC_BLOCK) & (c0 + CMP_BLOCK > s0) & (n < nc_real) & (s < ns_real)
    return ov.astype(BF16)


def _gather_pages_kernel(pt_ref, page_ref, tail_ref, o_ref):
    j = pl.program_id(1)
    n_pages = pl.num_programs(1) - 1

    @pl.when(j < n_pages)
    def _():
        o_ref[...] = page_ref[...]

    @pl.when(j == n_pages)
    def _():
        o_ref[...] = tail_ref[...]


def gather_pages(cache5, page_table, tail, li):
    B, n_pages = page_table.shape
    page, row_w = cache5.shape[1], cache5.shape[4]
    grid_spec = pltpu.PrefetchScalarGridSpec(
        num_scalar_prefetch=1,
        grid=(B, n_pages + 1),
        in_specs=[pl.BlockSpec((None, page, None, None, row_w),
                               lambda b, j, pt: (pt[b, jnp.minimum(j, n_pages - 1)], 0, li, 0, 0)),
                  pl.BlockSpec((None, page, row_w), lambda b, j, pt: (b, 0, 0))],
        out_specs=pl.BlockSpec((None, page, row_w), lambda b, j, pt: (b, j, 0)),
    )
    return pl.pallas_call(
        _gather_pages_kernel,
        grid_spec=grid_spec,
        out_shape=jax.ShapeDtypeStruct((B, (n_pages + 1) * page, row_w), F32),
        compiler_params=_cparams("parallel", "arbitrary"),
        name="gather_pages",
    )(page_table, cache5, tail)


def attn_params(li, attn_w_in, attn_gate_b, w_cmp_k, w_cmp_v, attn_w_out):
    qkv_w = N_HEADS * HEAD_DIM + 6 * N_KV * HEAD_DIM
    ng = 3 * N_HEADS
    half = lambda w: jnp.concatenate([w[:CMP_STRIDE], w[CMP_STRIDE:]], axis=-1)
    return dict(
        w_main=attn_w_in[li][:, :qkv_w],
        w_gate=jnp.pad(attn_w_in[li][:, qkv_w:], ((0, 0), (0, LANES - ng))),
        gate_b=jnp.pad(attn_gate_b[li], (0, LANES - ng)).reshape(1, LANES),
        wcat=jnp.stack([half(w_cmp_k[li]), half(w_cmp_v[li])]),
        w_out=attn_w_out[li],
    )


def attn_prompt(x2d, B, T, prm):
    p = mm(x2d, prm["w_main"])
    gates = mm(x2d, prm["w_gate"])
    q_blocks = N_HEADS
    n_pieces = T // CMP_STRIDE
    nc_real = n_pieces - CMP_BLOCK // CMP_STRIDE + 1
    ns_real = T // SLC_BLOCK
    ncp = -(-n_pieces // LANES) * LANES
    nsp = -(-ns_real // LANES) * LANES
    kcvc = nsa_compress(p, q_blocks, B, T, prm["wcat"], ncp, False)
    ov = _overlap_matrix(ncp, nsp, nc_real, ns_real)
    tq = min(128, T)
    o = nsa_attention(p, kcvc, p, q_blocks, p, q_blocks + 4 * N_KV, gates, prm["gate_b"], ov,
                      B=B, t_q=T, t_kv=T, t_win=T, tq=tq, q_off=0, nc_real=nc_real, ns_real=ns_real,
                      tk=min(256, T), n_kv_static=None, w_off=0, w_len=min(WINDOW + tq, T), w_follows_q=True,
                      precise=False)
    y = mm(o, prm["w_out"])
    kv = p.reshape(B, T, -1)[:, :, N_HEADS * HEAD_DIM:].reshape(B, T, 6, N_KV, HEAD_DIM)
    w_buf = min(WINDOW, T)
    win = jnp.pad(kv[:, :, 4:], ((0, 0), (WINDOW, 0), (0, 0), (0, 0), (0, 0)))[:, -w_buf:]
    return y, kv[:, :, :4], win


def attn_sample(x2d, B, T, prm, cache_nsa_kv, cache_win, page_table, li):
    n_pool, page = cache_nsa_kv.shape[:2]
    n_attn = cache_nsa_kv.shape[2]
    n_pages = page_table.shape[1]
    past_len = n_pages * page
    row_w = 4 * N_KV * HEAD_DIM
    p = mm(x2d, prm["w_main"], True)
    gates = mm(x2d, prm["w_gate"], True)
    kv_new = p[:, N_HEADS * HEAD_DIM:].reshape(B, T, 6, N_KV, HEAD_DIM)
    tail = jnp.pad(kv_new[:, :, :4].reshape(B, T, row_w), ((0, 0), (0, page - T), (0, 0)))
    full = gather_pages(cache_nsa_kv.reshape(n_pool, page, n_attn, 1, row_w), page_table, tail, li)
    t_kv = (n_pages + 1) * page
    full2d = full.reshape(B * t_kv, row_w)
    L = past_len + T
    l_pad = -(-L // SLC_BLOCK) * SLC_BLOCK
    nc_real = l_pad // CMP_STRIDE - CMP_BLOCK // CMP_STRIDE + 1
    ns_real = l_pad // SLC_BLOCK
    n_pieces = t_kv // CMP_STRIDE
    ncp = -(-n_pieces // LANES) * LANES
    nsp = -(-ns_real // LANES) * LANES
    kcvc = nsa_compress(full2d, 0, B, t_kv, prm["wcat"], ncp, True)
    ov = _overlap_matrix(ncp, nsp, nc_real, ns_real)
    w_buf = cache_win.shape[1]
    win = jnp.concatenate([cache_win, kv_new[:, :, 4:]], axis=1)
    t_win = -(-(w_buf + T) // LANES) * LANES
    win2d = jnp.pad(win.reshape(B, w_buf + T, 2 * N_KV * HEAD_DIM), ((0, 0), (0, t_win - (w_buf + T)), (0, 0)))
    win2d = win2d.reshape(B * t_win, 2 * N_KV * HEAD_DIM)
    n_kv = 3
    assert t_kv % (n_kv * LANES) == 0
    o = nsa_attention(p, kcvc, full2d, 0, win2d, 0, gates, prm["gate_b"], ov,
                      B=B, t_q=T, t_kv=t_kv, t_win=t_win, tq=T, q_off=past_len, nc_real=nc_real, ns_real=ns_real,
                      tk=t_kv // n_kv, n_kv_static=n_kv, w_off=past_len - w_buf, w_len=t_win, w_follows_q=False,
                      precise=True)
    y = mm(o, prm["w_out"], True)
    return y, kv_new[:, :, :4], win[:, -w_buf:]


MOE_TM = 256


ROUTER_TN = 256


def _first_argmax(vals):
    best, idx = vals[0], jnp.zeros(vals[0].shape, jnp.int32)
    for i in range(1, len(vals)):
        better = vals[i] > best
        best = jnp.where(better, vals[i], best)
        idx = jnp.where(better, i, idx)
    return best, idx


def _pick_by(idx, vals):
    out = vals[-1]
    for i in range(len(vals) - 2, -1, -1):
        out = jnp.where(idx == i, vals[i], out)
    return out


def _router_kernel(x_ref, wt_ref, bias_ref, e_ref, w_ref, rank_ref, cnt_ref, base, *, n_tok):
    i = pl.program_id(0)
    tn = x_ref.shape[0]

    @pl.when(i == 0)
    def _():
        base[...] = jnp.zeros(base.shape, F32)

    xh, xl = _split2(x_ref[...])
    wh, wl = _split2(wt_ref[...])
    logits = _dot_nt(wh, xh) + _dot_nt(wh, xl) + _dot_nt(wl, xh)
    aff = _sigmoid(logits)
    sel = aff + bias_ref[...]
    s = [sel[e:e + 1, :] for e in range(N_EXPERTS)]
    a = [aff[e:e + 1, :] for e in range(N_EXPERTS)]
    G = EXPERTS_PER_GROUP
    gscore = []
    for g in range(N_GROUPS):
        v0, v1, v2, v3 = s[G * g:G * g + G]
        hi1, lo1 = jnp.maximum(v0, v1), jnp.minimum(v0, v1)
        hi2, lo2 = jnp.maximum(v2, v3), jnp.minimum(v2, v3)
        gscore.append(jnp.maximum(hi1, hi2) + jnp.maximum(jnp.minimum(hi1, hi2), jnp.maximum(lo1, lo2)))
    _, gi = _first_argmax(gscore)
    sg = [_pick_by(gi, [s[G * g + j] for g in range(N_GROUPS)]) for j in range(G)]
    ag = [_pick_by(gi, [a[G * g + j] for g in range(N_GROUPS)]) for j in range(G)]
    _, l0 = _first_argmax(sg)
    _, l1 = _first_argmax([jnp.where(l0 == j, -jnp.inf, sg[j]) for j in range(G)])
    w0, w1 = _pick_by(l0, ag), _pick_by(l1, ag)
    wsum = w0 + w1
    e0, e1 = gi * G + l0, gi * G + l1
    e_ref[...] = jnp.concatenate([e0, e1], axis=0)
    w_ref[...] = jnp.concatenate([w0 / wsum, w1 / wsum], axis=0)
    tok = i * tn + lax.broadcasted_iota(jnp.int32, (N_EXPERTS, tn), 1)
    eid = lax.broadcasted_iota(jnp.int32, (N_EXPERTS, tn), 0)
    valid = tok < n_tok
    oh0 = jnp.where((eid == e0) & valid, 1.0, 0.0)
    oh1 = jnp.where((eid == e1) & valid, 1.0, 0.0)
    cnt = oh0 + oh1
    r_i = lax.broadcasted_iota(jnp.int32, (tn, tn), 0)
    c_i = lax.broadcasted_iota(jnp.int32, (tn, tn), 1)
    upper = jnp.where(r_i <= c_i, 1.0, 0.0).astype(BF16)
    excl = _dot(cnt.astype(BF16), upper) - cnt + base[...]
    rank_ref[...] = jnp.concatenate([jnp.sum(oh0 * excl, axis=0, keepdims=True),
                                     jnp.sum(oh1 * excl, axis=0, keepdims=True)], axis=0).astype(jnp.int32)
    base[...] = base[...] + jnp.sum(cnt, axis=1, keepdims=True)
    cnt_ref[...] = jnp.broadcast_to(base[...], cnt_ref.shape).astype(jnp.int32)


def router(x, w_t, bias_col):
    n, d = x.shape
    tn = ROUTER_TN if n >= ROUTER_TN else LANES
    nt = -(-n // tn)
    n_pad = nt * tn
    if n < tn:
        x = jnp.pad(x, ((0, tn - n), (0, 0)))
    two = lambda dt: jax.ShapeDtypeStruct((TOP_K, n_pad), dt)
    out_row = pl.BlockSpec((TOP_K, tn), lambda i: (0, i))
    e, w, rank, cnt = pl.pallas_call(
        functools.partial(_router_kernel, n_tok=n),
        grid=(nt,),
        in_specs=[pl.BlockSpec((tn, d), lambda i: (i, 0)),
                  pl.BlockSpec((N_EXPERTS, d), lambda i: (0, 0)),
                  pl.BlockSpec((N_EXPERTS, 1), lambda i: (0, 0))],
        out_specs=[out_row, out_row, out_row, pl.BlockSpec((N_EXPERTS, LANES), lambda i: (0, 0))],
        out_shape=[two(jnp.int32), two(F32), two(jnp.int32), jax.ShapeDtypeStruct((N_EXPERTS, LANES), jnp.int32)],
        scratch_shapes=[pltpu.VMEM((N_EXPERTS, 1), F32)],
        compiler_params=_cparams("arbitrary"),
        name="router",
    )(x, w_t, bias_col)
    return e[:, :n], w[:, :n], rank[:, :n], cnt[:, 0]


def _expert_kernel(te_ref, x_ref, rw_ref, wg_ref, wu_ref, wd_ref, o_ref):
    xb = x_ref[...].astype(BF16)
    hg = _dot(xb, wg_ref[...])
    hu = _dot(xb, wu_ref[...])
    h = (hg * _sigmoid(hg)) * hu * rw_ref[...]
    o_ref[...] = _dot(h.astype(BF16), wd_ref[...])


def expert_ffn(x_sorted, row_w, tile_expert, wg, wu, wd, layer):
    a_pad, d = x_sorted.shape
    f = wg.shape[3]
    n_tiles = a_pad // MOE_TM
    grid_spec = pltpu.PrefetchScalarGridSpec(
        num_scalar_prefetch=1,
        grid=(n_tiles,),
        in_specs=[pl.BlockSpec((MOE_TM, d), lambda t, te: (t, 0)),
                  pl.BlockSpec((MOE_TM, 1), lambda t, te: (t, 0)),
                  pl.BlockSpec((None, None, d, f), lambda t, te: (layer, te[t], 0, 0)),
                  pl.BlockSpec((None, None, d, f), lambda t, te: (layer, te[t], 0, 0)),
                  pl.BlockSpec((None, None, f, d), lambda t, te: (layer, te[t], 0, 0))],
        out_specs=pl.BlockSpec((MOE_TM, d), lambda t, te: (t, 0)),
    )
    return pl.pallas_call(
        _expert_kernel,
        grid_spec=grid_spec,
        out_shape=jax.ShapeDtypeStruct((a_pad, d), F32),
        compiler_params=_cparams("arbitrary"),
        name="expert_ffn",
    )(tile_expert, x_sorted, row_w, wg, wu, wd)


MOE_DENSE_FT = 512


def _moe_dense_kernel(x_ref, gate_ref, wg_ref, wu_ref, wd_ref, o_ref):
    e = pl.program_id(0)

    @pl.when((e == 0) & (pl.program_id(1) == 0))
    def _():
        o_ref[...] = jnp.zeros(o_ref.shape, F32)

    x = x_ref[...]
    hg = _dot3(x, wg_ref[...])
    hu = _dot3(x, wu_ref[...])
    gate = gate_ref[...]
    lane = lax.broadcasted_iota(jnp.int32, gate.shape, 1)
    gcol = jnp.sum(jnp.where(lane == e, gate, 0.0), axis=-1, keepdims=True)
    o_ref[...] += _dot3((hg * _sigmoid(hg)) * hu * gcol, wd_ref[...])


def moe_dense_precise(x, router_wt, r_bias_col, w_g, w_u, w_d, layer):
    n, d = x.shape
    f = w_g.shape[3]
    ft = MOE_DENSE_FT
    e, w_sel, _, _ = router(x, router_wt, r_bias_col)
    lanes = jnp.arange(LANES, dtype=jnp.int32)[None, :]
    gate = sum(jnp.where(e[k][:, None] == lanes, w_sel[k][:, None], 0.0) for k in range(TOP_K))
    return pl.pallas_call(
        _moe_dense_kernel,
        grid=(N_EXPERTS, f // ft),
        in_specs=[pl.BlockSpec((n, d), lambda e, j: (0, 0)),
                  pl.BlockSpec((n, LANES), lambda e, j: (0, 0)),
                  pl.BlockSpec((None, None, d, ft), lambda e, j: (layer, e, 0, j)),
                  pl.BlockSpec((None, None, d, ft), lambda e, j: (layer, e, 0, j)),
                  pl.BlockSpec((None, None, ft, d), lambda e, j: (layer, e, j, 0))],
        out_specs=pl.BlockSpec((n, d), lambda e, j: (0, 0)),
        out_shape=jax.ShapeDtypeStruct((n, d), F32),
        compiler_params=_cparams("arbitrary", "arbitrary"),
        name="moe_dense_precise",
    )(x, gate, w_g, w_u, w_d)


def moe(x, router_wt, r_bias_col, w_g, w_u, w_d, layer):
    n, d = x.shape
    e, w_sel, rank, counts = router(x, router_wt, r_bias_col)
    n_asg = n * TOP_K
    gsz = (counts + MOE_TM - 1) // MOE_TM * MOE_TM
    p_end = jnp.cumsum(gsz)
    p_start = p_end - gsz
    onehot = (e[:, :, None] == jnp.arange(N_EXPERTS, dtype=jnp.int32)[None, None, :]).astype(jnp.int32)
    dest = jnp.sum(onehot * p_start[None, None, :], axis=2) + rank
    a_pad = -(-(n_asg + N_EXPERTS * (MOE_TM - 1)) // MOE_TM) * MOE_TM
    tok = jnp.broadcast_to(jnp.arange(n, dtype=jnp.int32)[None, :], (TOP_K, n))
    src_tok = jnp.zeros((a_pad,), jnp.int32).at[dest.reshape(n_asg)].set(tok.reshape(n_asg))
    row_w = jnp.zeros((a_pad,), F32).at[dest.reshape(n_asg)].set(w_sel.reshape(n_asg))
    tile_start = jnp.arange(a_pad // MOE_TM, dtype=jnp.int32) * MOE_TM
    tile_expert = jnp.minimum(jnp.sum((p_end[None, :] <= tile_start[:, None]).astype(jnp.int32), axis=1),
                              N_EXPERTS - 1).astype(jnp.int32)
    ys = expert_ffn(x[src_tok], row_w.reshape(a_pad, 1), tile_expert, w_g, w_u, w_d, layer)
    return ys[dest[0]] + ys[dest[1]]


def kernel(x_prompt, x_sample, cache_nsa_kv, cache_win_kv, state_rwkv, state_rwkv_shift, state_lru_h, state_lru_conv,
           page_table, ln1_g, ln1_b, ln2_g, ln2_b, rec_w_in, rec_mu, rwkv_w0, rwkv_w_up, rwkv_a0, rwkv_a_up, rwkv_g_up,
           rwkv_k_k, rwkv_k_a, rwkv_r_k, rwkv_gn_g, rwkv_gn_b, lru_conv_w, lru_conv_b, lru_wa, lru_ba, lru_wx, lru_bx,
           lru_lambda, rec_w_out, attn_w_in, attn_gate_b, w_cmp_k, w_cmp_v, attn_w_out, moe_w_router, moe_router_bias,
           moe_w_gate, moe_w_up, moe_w_down):
    Bp, Tp, D = x_prompt.shape
    Bs, Ts, _ = x_sample.shape
    n_p, n_s = Bp * Tp, Bs * Ts
    rec_args = (rec_w_in, rec_mu, rwkv_w0, rwkv_w_up, rwkv_a0, rwkv_a_up, rwkv_g_up, rwkv_k_k, rwkv_k_a,
                rwkv_r_k, rwkv_gn_g, rwkv_gn_b, lru_conv_w, lru_conv_b, lru_wa, lru_ba, lru_wx, lru_bx, lru_lambda,
                rec_w_out)
    xp, xs = x_prompt.reshape(n_p, D), x_sample.reshape(n_s, D)
    router_w = moe_w_router.T
    moe_router_bias = moe_router_bias.reshape(N_EXPERTS, 1)
    wg_bf, wu_bf, wd_bf = moe_w_gate.astype(BF16), moe_w_up.astype(BF16), moe_w_down.astype(BF16)
    nsa_p, nsa_s, win_p, win_s = [], [], [], []
    rs_p, rs_s, rsh_p, rsh_s, lh_p, lh_s, lc_p, lc_s = [], [], [], [], [], [], [], []
    for layer in range(DEPTH):
        li = layer // 2
        if layer % 2 == 0:
            prm = rec_params(li, *rec_args)
            yp, S, sh, h, cb = rec_mixer(xp, Bp, Tp, jnp.zeros((Bp, RWKV_HEADS, RWKV_HD, RWKV_HD), F32),
                                         jnp.zeros((Bp, RWKV_SHIFT_W), F32), jnp.zeros((Bp, LRU_W), F32),
                                         jnp.zeros((Bp, CONV_W - 1, LRU_W), F32), prm)
            ys, S2, sh2, h2, cb2 = rec_mixer(xs, Bs, Ts, state_rwkv[:, li], state_rwkv_shift[:, li],
                                             state_lru_h[:, li], state_lru_conv[:, li], prm, precise=True)
            rs_p.append(S); rs_s.append(S2); rsh_p.append(sh); rsh_s.append(sh2)
            lh_p.append(h); lh_s.append(h2); lc_p.append(cb); lc_s.append(cb2)
        else:
            prm = attn_params(li, attn_w_in, attn_gate_b, w_cmp_k, w_cmp_v, attn_w_out)
            yp, rows, wb = attn_prompt(xp, Bp, Tp, prm)
            ys, rows2, wb2 = attn_sample(xs, Bs, Ts, prm, cache_nsa_kv, cache_win_kv[:, :, li], page_table, li)
            nsa_p.append(rows); nsa_s.append(rows2); win_p.append(wb); win_s.append(wb2)
        xp = ln_res(xp, yp, ln1_g[layer], ln1_b[layer])
        xs = ln_res(xs, ys, ln1_g[layer], ln1_b[layer])
        xp = ln_res(xp, moe(xp, router_w, moe_router_bias, wg_bf, wu_bf, wd_bf, layer), ln2_g[layer], ln2_b[layer])
        xs = ln_res(xs, moe_dense_precise(xs, router_w, moe_router_bias, moe_w_gate, moe_w_up, moe_w_down, layer),
                    ln2_g[layer], ln2_b[layer])
    return (xp.reshape(Bp, Tp, D), xs.reshape(Bs, Ts, D),
            jnp.stack(nsa_p, axis=2), jnp.stack(nsa_s, axis=2), jnp.stack(win_p, axis=2), jnp.stack(win_s, axis=2),
            jnp.stack(rs_p, axis=1), jnp.stack(rs_s, axis=1), jnp.stack(rsh_p, axis=1), jnp.stack(rsh_s, axis=1),
            jnp.stack(lh_p, axis=1), jnp.stack(lh_s, axis=1), jnp.stack(lc_p, axis=1), jnp.stack(lc_s, axis=1))
```

```python
import functools
import math

import jax
import jax.numpy as jnp
from jax import lax
from jax.experimental import pallas as pl
from jax.experimental.pallas import tpu as pltpu

F32 = jnp.float32
BF16 = jnp.bfloat16

D_MODEL = 2048
DEPTH = 4
RWKV_HEADS = 16
RWKV_HD = 64
RWKV_W = RWKV_HEADS * RWKV_HD
LORA_DECAY = 64
LORA_A = 64
LORA_GATE = 128
RWKV_SHIFT_W = 3 * RWKV_W + LORA_DECAY + LORA_A + LORA_GATE
GN_EPS = 64e-5
LRU_W = D_MODEL - RWKV_W
LRU_BLOCKS = 16
CONV_W = 4
LRU_C = 8.0
N_HEADS = 16
HEAD_DIM = 128
N_KV = 4
GQA_R = N_HEADS // N_KV
CMP_BLOCK = 32
CMP_STRIDE = 16
SLC_BLOCK = 64
TOP_N = 16
WINDOW = 512
SCALE = HEAD_DIM ** -0.5
N_EXPERTS = 16
N_GROUPS = 4
EXPERTS_PER_GROUP = N_EXPERTS // N_GROUPS
TOP_K = 2
D_EXPERT = 1024
LN_EPS = 1e-5
DN_ALPHA = (2 * DEPTH) ** 0.25
NEG = -1e30
BIG = 1e4

LANES = 128
SUBLANES = 8
VMEM_LIMIT_BYTES = 48 * 1024 * 1024

RWKV_PAD_W = 3584
PAIRS = RWKV_HEADS // 2


def _cparams(*sem):
    return pltpu.CompilerParams(dimension_semantics=sem, vmem_limit_bytes=VMEM_LIMIT_BYTES)


def _split2(x):
    hi = x.astype(BF16)
    lo = (x - hi.astype(F32)).astype(BF16)
    return hi, lo


def _dot(a, b):
    return jnp.dot(a, b, preferred_element_type=F32)


def _dot2(x, w_bf):
    hi, lo = _split2(x)
    return _dot(hi, w_bf) + _dot(lo, w_bf)


def _sigmoid(x):
    return 1.0 / (1.0 + jnp.exp(-x))


def _softplus(x):
    return jnp.maximum(x, 0.0) + jnp.log(1.0 + jnp.exp(-jnp.abs(x)))


def _mdot(a, b, precise):
    return _dot3(a, b) if precise else _dot(a.astype(BF16), b.astype(BF16))


def _mdot_nt(a, b, precise):
    return _dot3_nt(a, b) if precise else _dot_nt(a.astype(BF16), b.astype(BF16))


def _mm_kernel(x_ref, w_ref, o_ref, *, precise):
    o_ref[...] = _mdot(x_ref[...], w_ref[...], precise)


def _pick(n, cands):
    for c in cands:
        if n % c == 0:
            return c
    return n


def mm(x, w, precise=False):
    m, k = x.shape
    n = w.shape[1]
    tm = _pick(m, (512, 256, 128, 64, 32, 16, 8))
    tn = _pick(n, (512, 256, 128) if precise else (1024, 768, 512, 256, 128))
    return pl.pallas_call(
        functools.partial(_mm_kernel, precise=precise),
        grid=(n // tn, m // tm),
        in_specs=[pl.BlockSpec((tm, k), lambda j, i: (i, 0)),
                  pl.BlockSpec((k, tn), lambda j, i: (0, j))],
        out_specs=pl.BlockSpec((tm, tn), lambda j, i: (i, j)),
        out_shape=jax.ShapeDtypeStruct((m, n), F32),
        compiler_params=_cparams("parallel", "parallel"),
        name="mm",
    )(x, w)


def _mm2_kernel(x1_ref, x2_ref, w1_ref, w2_ref, o_ref, *, precise):
    o_ref[...] = _mdot(x1_ref[...], w1_ref[...], precise) + _mdot(x2_ref[...], w2_ref[...], precise)


def mm_cat2(x1, x2, w, precise=False):
    m, k1 = x1.shape
    k2 = x2.shape[1]
    assert k1 == k2 and w.shape[0] == k1 + k2
    n = w.shape[1]
    tm = _pick(m, (512, 256, 128, 64, 32, 16, 8))
    tn = _pick(n, (512, 256, 128) if precise else (1024, 512, 256, 128))
    return pl.pallas_call(
        functools.partial(_mm2_kernel, precise=precise),
        grid=(n // tn, m // tm),
        in_specs=[pl.BlockSpec((tm, k1), lambda j, i: (i, 0)),
                  pl.BlockSpec((tm, k2), lambda j, i: (i, 0)),
                  pl.BlockSpec((k1, tn), lambda j, i: (0, j)),
                  pl.BlockSpec((k2, tn), lambda j, i: (1, j))],
        out_specs=pl.BlockSpec((tm, tn), lambda j, i: (i, j)),
        out_shape=jax.ShapeDtypeStruct((m, n), F32),
        compiler_params=_cparams("parallel", "parallel"),
        name="mm_cat2",
    )(x1, x2, w, w)


def _ln_res_kernel(x_ref, y_ref, g_ref, b_ref, o_ref):
    z = DN_ALPHA * x_ref[...] + y_ref[...]
    mu = jnp.mean(z, axis=-1, keepdims=True)
    zc = z - mu
    var = jnp.mean(zc * zc, axis=-1, keepdims=True)
    o_ref[...] = zc * lax.rsqrt(var + LN_EPS) * g_ref[...] + b_ref[...]


def ln_res(x, y, g, b):
    m, d = x.shape
    tm = _pick(m, (256, 192, 128, 64, 32, 16, 8))
    row = pl.BlockSpec((tm, d), lambda i: (i, 0))
    vec = pl.BlockSpec((1, d), lambda i: (0, 0))
    return pl.pallas_call(
        _ln_res_kernel,
        grid=(m // tm,),
        in_specs=[row, row, vec, vec],
        out_specs=row,
        out_shape=jax.ShapeDtypeStruct((m, d), F32),
        compiler_params=_cparams("parallel"),
        name="ln_res",
    )(x, y, g.reshape(1, d), b.reshape(1, d))


def _shifted(carry_ref, x, j):
    ext = jnp.concatenate([carry_ref[...], x], axis=0)
    return pltpu.roll(ext, j, axis=0)[SUBLANES:, :]


def _rwkv_prep_kernel(p_ref, sh0_ref, mu_ref, w0_ref, a0_ref, wlora_ref, alora_ref, gup_ref,
                      kk_ref, ka_ref, rk_ref, bd_ref, *refs, chunked, precise):
    outs, carry = refs[:-1], refs[-1]
    ti = pl.program_id(1)

    @pl.when(ti == 0)
    def _():
        carry[...] = jnp.broadcast_to(sh0_ref[0], carry.shape)

    p = p_ref[...]
    tm = p.shape[0]
    p_prev = _shifted(carry, p, 1)
    carry[...] = p[tm - SUBLANES:, :]
    pm = p + mu_ref[...] * (p_prev - p)
    W = RWKV_W
    r = pm[:, 0:W]
    k = pm[:, W:2 * W]
    v = pm[:, 2 * W:3 * W]
    lo = pm[:, 3 * W:3 * W + LORA_DECAY + LORA_A]
    gd = pm[:, 3 * W + LORA_DECAY + LORA_A:RWKV_SHIFT_W]
    lane = lax.broadcasted_iota(jnp.int32, lo.shape, 1)
    z = jnp.where(lane < LORA_DECAY, jnp.tanh(lo), lo)
    w_l = _mdot(z, wlora_ref[...], precise)
    a_l = _mdot(z, alora_ref[...], precise)
    w = -_softplus(-(w0_ref[...] + w_l)) - 0.5
    log_decay = -jnp.exp(w)
    decay = jnp.exp(log_decay)
    a = _sigmoid(a0_ref[...] + a_l)
    g = _mdot(_sigmoid(gd), gup_ref[...], precise)
    bd = bd_ref[...]
    kk = k * kk_ref[...]
    ss = _dot2(kk * kk, bd)
    kn = kk * lax.rsqrt(jnp.maximum(ss, 1e-24))
    k_eff = k * (1.0 + (a - 1.0) * ka_ref[...])
    bvec = kn * a
    bonus = _dot2(r * k_eff * rk_ref[...], bd) * v
    if chunked:
        vals = (r, log_decay, k_eff, kn, bvec, v, g, bonus)
    else:
        vals = (r, decay, k_eff, kn, bvec, v, decay * r, _dot2(bvec * r, bd), _dot2(k_eff * r, bd), g, bonus)
    for o_ref, val in zip(outs, vals):
        o_ref[...] = val


def rwkv_prep(p_rw, sh0, B, T, prm, chunked, precise):
    n_out = 8 if chunked else 11
    tm = _pick(T, (256, 128, 64, 32, 16, 8))
    nt = T // tm
    W = RWKV_W
    row_in = pl.BlockSpec((tm, RWKV_PAD_W), lambda b, t: (b * nt + t, 0))
    row_out = pl.BlockSpec((tm, W), lambda b, t: (b * nt + t, 0))
    full = lambda a: pl.BlockSpec(a.shape, lambda b, t: (0,) * a.ndim)
    sh0p = sh0.reshape(B, 1, RWKV_PAD_W)
    consts = [prm["mu"], prm["w0"], prm["a0"], prm["wlora"], prm["alora"], prm["gup"],
              prm["k_k"], prm["k_a"], prm["r_k"], prm["bd"]]
    outs = pl.pallas_call(
        functools.partial(_rwkv_prep_kernel, chunked=chunked, precise=precise),
        grid=(B, nt),
        in_specs=[row_in, pl.BlockSpec((1, 1, RWKV_PAD_W), lambda b, t: (b, 0, 0))] + [full(c) for c in consts],
        out_specs=[row_out] * n_out,
        out_shape=[jax.ShapeDtypeStruct((B * T, W), F32)] * n_out,
        scratch_shapes=[pltpu.VMEM((SUBLANES, RWKV_PAD_W), F32)],
        compiler_params=_cparams("arbitrary", "arbitrary"),
        name="rwkv_prep",
    )(p_rw, sh0p, *consts)
    return outs


PAIR_GROUP = 4
RWKV_SCAN_BATCH = 2


def _rwkv_scan_kernel(r_ref, w_ref, k_ref, kn_ref, b_ref, v_ref, wr_ref, br_ref, kr_ref, s0_ref, oseg_ref,
                      y_ref, sfin_ref, s_scr, *, nb, tc):
    ti = pl.program_id(1)

    @pl.when(ti == 0)
    def _():
        s_scr[...] = s0_ref[...]

    row = lax.broadcasted_iota(jnp.int32, (RWKV_HD, LANES), 0)
    lane = lax.broadcasted_iota(jnp.int32, (RWKV_HD, LANES), 1)
    diag = (lane & (RWKV_HD - 1)) == row
    oseg = oseg_ref[...]
    n_pairs = nb * PAIRS
    R = RWKV_HD

    def step8(t8, carry):
        base = pl.multiple_of(t8 * SUBLANES, SUBLANES)
        for g0 in range(0, n_pairs, PAIR_GROUP):
            ids = [(i // PAIRS, i % PAIRS) for i in range(g0, g0 + PAIR_GROUP)]
            n = len(ids)
            tile = lambda ref, b, p: ref[b, pl.ds(base, SUBLANES), p * LANES:(p + 1) * LANES]
            tiles = {nm: [tile(ref, b, p) for (b, p) in ids]
                     for nm, ref in (("kn", kn_ref), ("wr", wr_ref), ("v", v_ref), ("w", w_ref), ("b", b_ref),
                                     ("k", k_ref), ("br", br_ref), ("kr", kr_ref))}
            s_cur = [s_scr[b * PAIRS + p] for (b, p) in ids]
            y_rows = [[] for _ in ids]
            for j in range(SUBLANES):
                row = lambda nm, i: tiles[nm][i][j:j + 1, :]
                blocks = ([s_cur[i] * row("kn", i) for i in range(n)]
                          + [s_cur[i] * row("wr", i) for i in range(n)]
                          + [jnp.where(diag, row("v", i), 0.0) for i in range(n)])
                red = _dot2(jnp.concatenate(blocks, axis=0), oseg)
                for i in range(n):
                    skk = red[i * R:(i + 1) * R]
                    u = red[(n + i) * R:(n + i + 1) * R]
                    vb = red[(2 * n + i) * R:(2 * n + i + 1) * R]
                    s_cur[i] = s_cur[i] * row("w", i) - skk * row("b", i) + vb * row("k", i)
                    y_bc = u - skk * row("br", i) + vb * row("kr", i)
                    y_rows[i].append(jnp.sum(jnp.where(diag, y_bc, 0.0), axis=0, keepdims=True))
            for i, (b, p) in enumerate(ids):
                s_scr[b * PAIRS + p] = s_cur[i]
                y_ref[b, pl.ds(base, SUBLANES), p * LANES:(p + 1) * LANES] = jnp.concatenate(y_rows[i], axis=0)
        return carry

    lax.fori_loop(0, tc // SUBLANES, step8, 0)

    @pl.when(ti == pl.num_programs(1) - 1)
    def _():
        sfin_ref[...] = s_scr[...]


def rwkv_scan(seqs, s0, B, T):
    tc = _pick(T, (128, 64, 32, 16, 8))
    nb = RWKV_SCAN_BATCH
    assert B % nb == 0
    views = [s.reshape(B, T, RWKV_W) for s in seqs]
    s0p = s0.reshape(B, PAIRS, 2, RWKV_HD, RWKV_HD).transpose(0, 1, 3, 2, 4).reshape(B * PAIRS, RWKV_HD, LANES)
    li = jnp.arange(LANES)
    oseg = (li[:, None] // RWKV_HD == li[None, :] // RWKV_HD).astype(BF16)
    seq_spec = pl.BlockSpec((nb, tc, RWKV_W), lambda b, t: (b, t, 0))
    st_spec = pl.BlockSpec((nb * PAIRS, RWKV_HD, LANES), lambda b, t: (b, 0, 0))
    y, sfin = pl.pallas_call(
        functools.partial(_rwkv_scan_kernel, nb=nb, tc=tc),
        grid=(B // nb, T // tc),
        in_specs=[seq_spec] * 9 + [st_spec, pl.BlockSpec((LANES, LANES), lambda b, t: (0, 0))],
        out_specs=[seq_spec, st_spec],
        out_shape=[jax.ShapeDtypeStruct((B, T, RWKV_W), F32),
                   jax.ShapeDtypeStruct((B * PAIRS, RWKV_HD, LANES), F32)],
        scratch_shapes=[pltpu.VMEM((nb * PAIRS, RWKV_HD, LANES), F32)],
        compiler_params=_cparams("arbitrary", "arbitrary"),
        name="rwkv_scan",
    )(*views, s0p, oseg)
    sfin = sfin.reshape(B, PAIRS, RWKV_HD, 2, RWKV_HD).transpose(0, 1, 3, 2, 4).reshape(B, RWKV_HEADS, RWKV_HD, RWKV_HD)
    return y.reshape(B * T, RWKV_W), sfin


RWKV_CHUNK = 64
CHUNK_PAIRS = 8


def _split_bf(x):
    hi = x.astype(BF16)
    return hi, (x - hi.astype(F32)).astype(BF16)


def _dot3(a, b):
    ah, al = _split_bf(a)
    bh, bl = _split_bf(b)
    return _dot(ah, bh) + _dot(al, bh) + _dot(ah, bl)


def _dot3_nt(a, b):
    ah, al = _split_bf(a)
    bh, bl = _split_bf(b)
    return _dot_nt(ah, bh) + _dot_nt(al, bh) + _dot_nt(ah, bl)


def _rwkv_chunk_math(sts, rs, lws, ks, kns, bs, vs):
    C = RWKV_CHUNK
    lane = lax.broadcasted_iota(jnp.int32, (C, LANES), 1)
    row = lax.broadcasted_iota(jnp.int32, (C, LANES), 0)
    head0 = lane < RWKV_HD
    by_head = lambda x: jnp.concatenate([jnp.where(head0, x, 0.0), jnp.where(head0, 0.0, x)], axis=0)
    each = lambda f, *ls: [f(*a) for a in zip(*ls)]

    def cumsum_rows(lw):
        cs, d = lw, 1
        while d < C:
            cs = cs + jnp.where(row >= d, pltpu.roll(cs, d, axis=0), 0.0)
            d *= 2
        return cs

    css = each(cumsum_rows, lws)
    totals = [cs[C - 1:C, :] for cs in css]
    kps = each(lambda kn, cs, lw: kn * jnp.exp(cs - lw), kns, css, lws)
    rps = each(lambda r, cs: r * jnp.exp(cs), rs, css)
    e_negs = [jnp.exp(-cs) for cs in css]
    e_rems = each(lambda t, cs: jnp.exp(t - cs), totals, css)
    grams = each(lambda kp, rp, k, b, e: _dot3_nt(jnp.concatenate([kp, rp], axis=0),
                                                  jnp.concatenate([by_head(k * e), by_head(b * e)], axis=0)),
                 kps, rps, ks, bs, e_negs)
    s_idx = lane & (C - 1)
    strict, incl = s_idx < row, s_idx <= row
    a_ks = [jnp.where(strict, g[:C, :LANES], 0.0) for g in grams]
    a_bs = [jnp.where(strict, g[:C, LANES:], 0.0) for g in grams]
    d_ks = [jnp.where(incl, g[C:, :LANES], 0.0) for g in grams]
    d_bs = [jnp.where(incl, g[C:, LANES:], 0.0) for g in grams]
    r2 = lax.broadcasted_iota(jnp.int32, (2 * C, LANES), 0)
    l2 = lax.broadcasted_iota(jnp.int32, (2 * C, LANES), 1)
    eye = r2 == l2
    n_pows = [-by_head(a) for a in a_bs]
    t_invs = [jnp.where(eye, 1.0, 0.0) + n for n in n_pows]
    p = 2
    while p < C:
        n_pows = each(lambda n: _dot3(n, n), n_pows)
        t_invs = each(lambda t, n: t + _dot3(t, n), t_invs, n_pows)
        p *= 2
    v_bhs = [by_head(v) for v in vs]
    rhss = each(lambda kp, a, st, vb: _dot3(jnp.concatenate([kp, a], axis=1), jnp.concatenate([st, vb], axis=0)),
                kps, a_ks, sts, v_bhs)
    zs = each(lambda t, rhs: _dot3(t[:C] + t[C:], by_head(rhs)), t_invs, rhss)
    ys = each(lambda rp, dk, db, st, vb, z: _dot3(jnp.concatenate([rp, dk, -db], axis=1),
                                                  jnp.concatenate([st, vb, by_head(z)], axis=0)),
              rps, d_ks, d_bs, sts, v_bhs, zs)
    same_head = (r2 < RWKV_HD) == (l2 < RWKV_HD)

    def new_state(st, k, b, e, total, v, z):
        g_total = jnp.where(eye, jnp.broadcast_to(jnp.exp(total), (2 * C, LANES)), 0.0)
        kb_t = jnp.transpose(jnp.concatenate([k * e, b * e], axis=0))
        upd = _dot3(kb_t, jnp.concatenate([v, -z], axis=0))
        return _dot3(g_total, st) + jnp.where(same_head, upd, 0.0)

    return ys, each(new_state, sts, ks, bs, e_rems, totals, vs, zs)


def _rwkv_chunk_kernel(r_ref, lw_ref, k_ref, kn_ref, b_ref, v_ref, s0_ref, y_ref, sfin_ref, s_scr):
    ci = pl.program_id(2)

    @pl.when(ci == 0)
    def _():
        s_scr[...] = s0_ref[...]

    sls = [slice(i * LANES, (i + 1) * LANES) for i in range(CHUNK_PAIRS)]
    cols = lambda ref: [ref[:, sl] for sl in sls]
    ys, sts = _rwkv_chunk_math([s_scr[i] for i in range(CHUNK_PAIRS)], cols(r_ref), cols(lw_ref), cols(k_ref),
                               cols(kn_ref), cols(b_ref), cols(v_ref))
    for i, sl in enumerate(sls):
        y_ref[:, sl] = ys[i]
        s_scr[i] = sts[i]

    @pl.when(ci == pl.num_programs(2) - 1)
    def _():
        sfin_ref[...] = s_scr[...]


def rwkv_scan_chunked(seqs, s0, B, T):
    C, PP = RWKV_CHUNK, CHUNK_PAIRS
    assert T % C == 0 and PAIRS % PP == 0
    npg = PAIRS // PP
    views = [s.reshape(B, T, RWKV_W) for s in seqs]
    s0t = s0.reshape(B, PAIRS, 2, RWKV_HD, RWKV_HD).transpose(0, 1, 2, 4, 3)
    eye2 = jnp.eye(2, dtype=F32)
    s0bd = (s0t[:, :, :, :, None, :] * eye2[None, None, :, None, :, None]).reshape(B * PAIRS, LANES, LANES)
    seq_spec = pl.BlockSpec((None, C, PP * LANES), lambda b, g, c: (b, c, g))
    st_spec = pl.BlockSpec((PP, LANES, LANES), lambda b, g, c: (b * npg + g, 0, 0))
    y, sfin = pl.pallas_call(
        _rwkv_chunk_kernel,
        grid=(B, npg, T // C),
        in_specs=[seq_spec] * 6 + [st_spec],
        out_specs=[seq_spec, st_spec],
        out_shape=[jax.ShapeDtypeStruct((B, T, RWKV_W), F32),
                   jax.ShapeDtypeStruct((B * PAIRS, LANES, LANES), F32)],
        scratch_shapes=[pltpu.VMEM((PP, LANES, LANES), F32)],
        compiler_params=_cparams("parallel", "parallel", "arbitrary"),
        name="rwkv_chunk",
    )(*views, s0bd)
    sf = sfin.reshape(B, PAIRS, 2, RWKV_HD, 2, RWKV_HD)
    sf = jnp.stack([sf[:, :, 0, :, 0, :], sf[:, :, 1, :, 1, :]], axis=2)
    return y.reshape(B * T, RWKV_W), sf.transpose(0, 1, 2, 4, 3).reshape(B, RWKV_HEADS, RWKV_HD, RWKV_HD)


def _rwkv_post_kernel(y_ref, bonus_ref, g_ref, gng_ref, gnb_ref, bd_ref, o_ref):
    y = y_ref[...]
    bd = bd_ref[...]
    inv = 1.0 / RWKV_HD
    mean = _dot2(y, bd) * inv
    yc = y - mean
    var = _dot2(yc * yc, bd) * inv
    yn = yc * lax.rsqrt(var + GN_EPS) * gng_ref[...] + gnb_ref[...]
    o_ref[...] = (yn + bonus_ref[...]) * g_ref[...]


def rwkv_post(y, bonus, g, gn_g, gn_b, bd):
    m, W = y.shape
    tm = _pick(m, (256, 128, 64, 32, 16, 8))
    row = pl.BlockSpec((tm, W), lambda i: (i, 0))
    vec = pl.BlockSpec((1, W), lambda i: (0, 0))
    return pl.pallas_call(
        _rwkv_post_kernel,
        grid=(m // tm,),
        in_specs=[row, row, row, vec, vec, pl.BlockSpec((W, W), lambda i: (0, 0))],
        out_specs=row,
        out_shape=jax.ShapeDtypeStruct((m, W), F32),
        compiler_params=_cparams("parallel"),
        name="rwkv_post",
    )(y, bonus, g, gn_g.reshape(1, W), gn_b.reshape(1, W), bd)


def _gelu_tanh(x):
    return 0.5 * x * (1.0 + jnp.tanh(math.sqrt(2.0 / math.pi) * (x + 0.044715 * (x * x * x))))


def _neg_expm1(x):
    series = -x * (1.0 + x * (1.0 / 2.0) * (1.0 + x * (1.0 / 3.0) * (1.0 + x * (1.0 / 4.0) * (1.0 + x * (1.0 / 5.0)
             * (1.0 + x * (1.0 / 6.0))))))
    return jnp.where(x > -0.05, series, 1.0 - jnp.exp(x))


def _lru_kernel(px_ref, pg_ref, c0_ref, h0_ref, cw_ref, cb_ref, wa_ref, ba_ref, wx_ref, bx_ref, lam_ref,
                y_ref, hlast_ref, carry, hcar, *, precise):
    ti = pl.program_id(1)

    @pl.when(ti == 0)
    def _():
        carry[...] = c0_ref[0]
        hcar[...] = h0_ref[0]

    x = px_ref[...]
    tm = x.shape[0]
    cw = cw_ref[...]
    xc = cb_ref[...] + cw[CONV_W - 1:CONV_W, :] * x
    for j in range(1, CONV_W):
        xc = xc + cw[CONV_W - 1 - j:CONV_W - j, :] * _shifted(carry, x, j)
    carry[...] = x[tm - SUBLANES:, :]
    xcm = xc if precise else xc.astype(BF16)
    r = _sigmoid(_mdot(xcm, wa_ref[...], precise) + ba_ref[...])
    i = _sigmoid(_mdot(xcm, wx_ref[...], precise) + bx_ref[...])
    log_a = (-LRU_C * _softplus(-lam_ref[...])) * r
    a = jnp.exp(log_a)
    b = jnp.sqrt(_neg_expm1(2.0 * log_a)) * (i * xc)
    rows = lax.broadcasted_iota(jnp.int32, a.shape, 0)
    d = 1
    while d < tm:
        a_sh = pltpu.roll(a, d, axis=0)
        b_sh = pltpu.roll(b, d, axis=0)
        keep = rows >= d
        b = jnp.where(keep, a * b_sh + b, b)
        a = jnp.where(keep, a * a_sh, a)
        d *= 2
    h = a * hcar[...] + b
    hcar[...] = h[tm - 1:tm, :]
    hlast_ref[0] = h[tm - 1:tm, :]
    y_ref[...] = h * _gelu_tanh(pg_ref[...])


def lru_mix(p_l, c0, h0, B, T, prm, precise):
    W = LRU_W
    tm = _pick(T, (256, 128, 64, 32, 16, 8))
    nt = T // tm
    c0p = jnp.concatenate([jnp.zeros((B, SUBLANES - (CONV_W - 1), W), F32), c0], axis=1)
    h0p = h0.reshape(B, 1, W)
    full = lambda a: pl.BlockSpec(a.shape, lambda b, t: (0,) * a.ndim)
    consts = [prm["conv_w"], prm["conv_b"], prm["wa"], prm["ba"], prm["wx"], prm["bx"], prm["lam"]]
    y, hlast = pl.pallas_call(
        functools.partial(_lru_kernel, precise=precise),
        grid=(B, nt),
        in_specs=[pl.BlockSpec((tm, W), lambda b, t: (b * nt + t, 0)),
                  pl.BlockSpec((tm, W), lambda b, t: (b * nt + t, 1)),
                  pl.BlockSpec((1, SUBLANES, W), lambda b, t: (b, 0, 0)),
                  pl.BlockSpec((1, 1, W), lambda b, t: (b, 0, 0))] + [full(c) for c in consts],
        out_specs=[pl.BlockSpec((tm, W), lambda b, t: (b * nt + t, 0)),
                   pl.BlockSpec((1, 1, W), lambda b, t: (b, 0, 0))],
        out_shape=[jax.ShapeDtypeStruct((B * T, W), F32), jax.ShapeDtypeStruct((B, 1, W), F32)],
        scratch_shapes=[pltpu.VMEM((SUBLANES, W), F32), pltpu.VMEM((1, W), F32)],
        compiler_params=_cparams("arbitrary", "arbitrary"),
        name="lru_mix",
    )(p_l, p_l, c0p, h0p, *consts)
    return y, hlast.reshape(B, W)


def _block_diag(w):
    n, d, e = w.shape
    eye = jnp.eye(n, dtype=w.dtype)
    return (eye[:, None, :, None] * w[:, :, None, :]).reshape(n * d, n * e)


def rec_params(li, rec_w_in, rec_mu, rwkv_w0, rwkv_w_up, rwkv_a0, rwkv_a_up, rwkv_g_up, rwkv_k_k, rwkv_k_a,
               rwkv_r_k, rwkv_gn_g, rwkv_gn_b, lru_conv_w, lru_conv_b, lru_wa, lru_ba, lru_wx, lru_bx, lru_lambda,
               rec_w_out):
    W = RWKV_W
    pad = RWKV_PAD_W - RWKV_SHIFT_W
    hi = jnp.arange(W) // RWKV_HD
    zl = jnp.zeros((LORA_DECAY, W), F32)
    return dict(
        w_rw=jnp.pad(rec_w_in[li][:, :RWKV_SHIFT_W], ((0, 0), (0, pad))),
        w_l=rec_w_in[li][:, RWKV_SHIFT_W:],
        mu=jnp.pad(rec_mu[li], (0, pad)).reshape(1, RWKV_PAD_W),
        w0=rwkv_w0[li].reshape(1, W), a0=rwkv_a0[li].reshape(1, W),
        wlora=jnp.concatenate([rwkv_w_up[li], zl], axis=0),
        alora=jnp.concatenate([zl, rwkv_a_up[li]], axis=0),
        gup=rwkv_g_up[li],
        k_k=rwkv_k_k[li].reshape(1, W), k_a=rwkv_k_a[li].reshape(1, W), r_k=rwkv_r_k[li].reshape(1, W),
        bd=(hi[:, None] == hi[None, :]).astype(BF16),
        gn_g=rwkv_gn_g[li], gn_b=rwkv_gn_b[li],
        conv_w=lru_conv_w[li], conv_b=lru_conv_b[li].reshape(1, LRU_W),
        wa=_block_diag(lru_wa[li]), ba=lru_ba[li].reshape(1, LRU_W),
        wx=_block_diag(lru_wx[li]), bx=lru_bx[li].reshape(1, LRU_W),
        lam=lru_lambda[li].reshape(1, LRU_W),
        w_out=rec_w_out[li],
    )


def rec_mixer(x2d, B, T, S0, sh0, h0, c0, prm, precise=False):
    p_rw = mm(x2d, prm["w_rw"], precise)
    p_l = mm(x2d, prm["w_l"], precise)
    sh0p = jnp.pad(sh0, ((0, 0), (0, RWKV_PAD_W - RWKV_SHIFT_W)))
    chunked = T % RWKV_CHUNK == 0
    outs = rwkv_prep(p_rw, sh0p, B, T, prm, chunked, precise)
    g, bonus = outs[-2:]
    if chunked:
        y_raw, s_fin = rwkv_scan_chunked(outs[:6], S0, B, T)
    else:
        y_raw, s_fin = rwkv_scan(outs[:9], S0, B, T)
    y_rw = rwkv_post(y_raw, bonus, g, prm["gn_g"], prm["gn_b"], prm["bd"])
    y_lru, h_last = lru_mix(p_l, c0, h0, B, T, prm, precise)
    y = mm_cat2(y_rw, y_lru, prm["w_out"], precise)
    sh = p_rw.reshape(B, T, RWKV_PAD_W)[:, -1, :RWKV_SHIFT_W]
    px = p_l.reshape(B, T, 2 * LRU_W)[:, :, :LRU_W]
    cbuf = jnp.concatenate([c0, px], axis=1)[:, -(CONV_W - 1):] if T < CONV_W - 1 else px[:, -(CONV_W - 1):]
    return y, s_fin, sh, h_last, cbuf


def _dot_nt(a, b):
    return lax.dot_general(a, b, (((1,), (1,)), ((), ())), preferred_element_type=F32)


def _compress_kernel(slab_ref, w_ref, o_ref, *, n_pieces, precise):
    acc = None
    for l in range(CMP_STRIDE):
        d = _mdot(slab_ref[pl.ds(l, n_pieces, stride=CMP_STRIDE), :], w_ref[l], precise)
        acc = d if acc is None else acc + d
    out = acc[:, :HEAD_DIM] + pltpu.roll(acc[:, HEAD_DIM:], n_pieces - 1, axis=0)
    o_ref[0:n_pieces, :] = out
    pad = o_ref.shape[0] - n_pieces
    if pad:
        o_ref[n_pieces:, :] = jnp.zeros((pad, HEAD_DIM), F32)


def nsa_compress(kv2d, col0, B, t_kv, wcat, ncp, precise):
    n_pieces = t_kv // CMP_STRIDE
    return pl.pallas_call(
        functools.partial(_compress_kernel, n_pieces=n_pieces, precise=precise),
        grid=(B, 2, N_KV),
        in_specs=[pl.BlockSpec((t_kv, HEAD_DIM), lambda b, c, g: (b, col0 + c * N_KV + g)),
                  pl.BlockSpec((None, CMP_STRIDE, HEAD_DIM, 2 * HEAD_DIM), lambda b, c, g: (c, 0, 0, 0))],
        out_specs=pl.BlockSpec((None, None, None, ncp, HEAD_DIM), lambda b, c, g: (b, c, g, 0, 0)),
        out_shape=jax.ShapeDtypeStruct((B, 2, N_KV, ncp, HEAD_DIM), F32),
        compiler_params=_cparams("parallel", "parallel", "parallel"),
        name="nsa_compress",
    )(kv2d, wcat)


MASKED_DIST = 1e30
EXP2_SCALE = SCALE * math.log2(math.e)


def _parts(a, precise):
    return _split_bf(a) if precise else (a.astype(BF16),)


def _pdot(a, b, nt=False):
    dot = _dot_nt if nt else _dot
    out = dot(a[0], b[0])
    if len(a) > 1:
        out = out + dot(a[1], b[0]) + dot(a[0], b[1])
    return out


def _online_softmax_step(q, k, v, cdist, m, l, acc, precise):
    u = _pdot(q, k, nt=True) - cdist
    m_new = jnp.maximum(m, jnp.max(u, axis=-1, keepdims=True))
    alpha = jnp.exp2((m - m_new) * EXP2_SCALE)
    p = jnp.exp2((u - m_new) * EXP2_SCALE)
    l = alpha * l + jnp.sum(p, axis=-1, keepdims=True)
    acc = alpha * acc + _pdot(_parts(p, precise), v)
    return m_new, l, acc


def _nsa_kernel(q_ref, kc_ref, vc_ref, ks_ref, vs_ref, kw_ref, vw_ref, gate_ref, gb_ref, ov_ref, o_ref, *,
                tq, q_off, nc_real, ns_real, tk, n_kv_static, w_off, w_len, w_follows_q, precise, stack_heads):
    g = pl.program_id(1)
    qi = pl.program_id(2)
    q0 = qi * tq
    ncp = kc_ref.shape[0]
    nsp = ov_ref.shape[1]
    qpos_col = q_off + q0 + lax.broadcasted_iota(jnp.int32, (tq, 1), 0)
    q_all = q_ref[...]
    q_bf = [_parts(q_all[:, r * HEAD_DIM:(r + 1) * HEAD_DIM], precise) for r in range(GQA_R)]
    head_f = (g * GQA_R).astype(F32)
    slopes = [jnp.exp2(-0.5 * (jnp.full((1, 1), 1.0 + r, F32) + head_f)) for r in range(GQA_R)]

    n_idx = lax.broadcasted_iota(jnp.int32, (tq, ncp), 1)
    dist_c = qpos_col - (n_idx * CMP_STRIDE + (CMP_BLOCK - 1))
    ok_c = (dist_c >= 0) & (n_idx < nc_real)
    dist_cf = dist_c.astype(F32)
    kc_bf = _parts(kc_ref[...], precise)
    vc_bf = _parts(vc_ref[...], precise)
    o_c = []
    p_sum = None
    for r in range(GQA_R):
        s = _pdot(q_bf[r], kc_bf, nt=True) * SCALE - slopes[r] * dist_cf
        s = jnp.where(ok_c, s, NEG)
        m = jnp.max(s, axis=-1, keepdims=True)
        e = jnp.where(ok_c, jnp.exp(s - m), 0.0)
        p = e / jnp.maximum(jnp.sum(e, axis=-1, keepdims=True), 1e-30)
        o_c.append(_pdot(_parts(p, precise), vc_bf))
        p_sum = p if p_sum is None else p_sum + p
    imp = _dot2(p_sum, ov_ref[...])
    s_idx = lax.broadcasted_iota(jnp.int32, (tq, nsp), 1)
    cur = lax.shift_right_logical(qpos_col, int(math.log2(SLC_BLOCK)))
    forced = (s_idx == 0) | (s_idx == cur) | (s_idx == cur - 1)
    causal_blk = (s_idx * SLC_BLOCK <= qpos_col) & (s_idx < ns_real)
    score = jnp.where(causal_blk, imp + jnp.where(forced, BIG, 0.0), NEG)
    n_acc = 4
    ranks = [jnp.zeros((tq, nsp), jnp.int32) for _ in range(n_acc)]
    for j in range(ns_real):
        col = score[:, j:j + 1]
        ahead = (col > score) | ((col == score) & (s_idx > j))
        ranks[j % n_acc] = ranks[j % n_acc] + jnp.where(ahead, 1, 0)
    rank = (ranks[0] + ranks[1]) + (ranks[2] + ranks[3])
    sel = (rank < min(TOP_N, ns_real)) & (score > 0.5 * NEG)
    sel_bf = jnp.where(sel, 1.0, 0.0).astype(BF16)

    cslopes = [sl * (1.0 / SCALE) for sl in slopes]
    if stack_heads:
        qg = [_parts(jnp.concatenate([q_all[:, r * HEAD_DIM:(r + 1) * HEAD_DIM] for r in range(GQA_R)], axis=0),
                     precise)]
        csg = [jnp.concatenate([jnp.broadcast_to(c, (tq, 1)) for c in cslopes], axis=0)]
        rep = lambda x: jnp.concatenate([x] * GQA_R, axis=0)
        rows = GQA_R * tq
        head_rows = lambda st, r: tuple(a[r * tq:(r + 1) * tq] for a in st[0])
    else:
        qg, csg, rep, rows = q_bf, cslopes, (lambda x: x), tq
        head_rows = lambda st, r: st[r]

    def init():
        return tuple((jnp.full((rows, 1), NEG, F32), jnp.zeros((rows, 1), F32), jnp.zeros((rows, HEAD_DIM), F32))
                     for _ in qg)

    blk_per_tile = tk // SLC_BLOCK
    ex_row = lax.broadcasted_iota(jnp.int32, (nsp, tk), 0)
    ex_col = lax.shift_right_logical(lax.broadcasted_iota(jnp.int32, (nsp, tk), 1), int(math.log2(SLC_BLOCK)))
    key_i = lax.broadcasted_iota(jnp.int32, (tq, tk), 1)

    def slc_body(kt, carry):
        base = pl.multiple_of(kt * tk, tk)
        expand = jnp.where(ex_row == ex_col + kt * blk_per_tile, 1.0, 0.0).astype(BF16)
        selexp = _dot(sel_bf, expand)
        dist = qpos_col - (key_i + base)
        distm = rep(jnp.where(selexp > 0.5, jnp.where(dist >= 0, dist.astype(F32), MASKED_DIST), MASKED_DIST))
        k_p = _parts(ks_ref[pl.ds(base, tk), :], precise)
        v_p = _parts(vs_ref[pl.ds(base, tk), :], precise)
        return tuple(_online_softmax_step(qg[i], k_p, v_p, csg[i] * distm, *carry[i], precise)
                     for i in range(len(qg)))

    if n_kv_static is None:
        n_kv = (q0 + tq + tk - 1) // tk
    else:
        n_kv = n_kv_static
    st_s = lax.fori_loop(0, n_kv, slc_body, init())

    if w_follows_q:
        w_base = pl.multiple_of(jnp.maximum(q0 + tq - w_len, 0), SUBLANES)
    else:
        w_base = 0
    dist_w = qpos_col - (w_off + w_base + lax.broadcasted_iota(jnp.int32, (tq, w_len), 1))
    ok_w = jnp.where(dist_w >= 0, dist_w, WINDOW) < WINDOW
    dist_wm = rep(jnp.where(ok_w, dist_w.astype(F32), MASKED_DIST))
    kw_p = _parts(kw_ref[pl.ds(w_base, w_len), :], precise)
    vw_p = _parts(vw_ref[pl.ds(w_base, w_len), :], precise)
    st_w = tuple(_online_softmax_step(qg[i], kw_p, vw_p, csg[i] * dist_wm, *init()[i], precise)
                 for i in range(len(qg)))

    gates = _sigmoid(gate_ref[...] + gb_ref[...])
    lane = lax.broadcasted_iota(jnp.int32, gates.shape, 1)
    for r in range(GQA_R):
        head = g * GQA_R + r
        gcol = lambda br: jnp.sum(jnp.where(lane == br * N_HEADS + head, gates, 0.0), axis=-1, keepdims=True)
        _, l_s, a_s = head_rows(st_s, r)
        _, l_w, a_w = head_rows(st_w, r)
        o_s = a_s / jnp.maximum(l_s, 1e-30)
        o_w = a_w / jnp.maximum(l_w, 1e-30)
        o_ref[:, r * HEAD_DIM:(r + 1) * HEAD_DIM] = gcol(0) * o_c[r] + gcol(1) * o_s + gcol(2) * o_w


def nsa_attention(q2d, kcvc, kv2d, kv_col0, win2d, win_col0, gates, gate_b, overlap, *, B, t_q, t_kv, t_win, tq,
                  q_off, nc_real, ns_real, tk, n_kv_static, w_off, w_len, w_follows_q, precise, stack_heads):
    nq = t_q // tq
    ncp = kcvc.shape[3]
    nsp = overlap.shape[1]
    qw = GQA_R * HEAD_DIM
    slab = lambda arr_t, col: pl.BlockSpec((arr_t, HEAD_DIM), col)
    kern = functools.partial(_nsa_kernel, tq=tq, q_off=q_off, nc_real=nc_real, ns_real=ns_real, tk=tk,
                             n_kv_static=n_kv_static, w_off=w_off, w_len=w_len, w_follows_q=w_follows_q,
                             precise=precise, stack_heads=stack_heads)
    return pl.pallas_call(
        kern,
        grid=(B, N_KV, nq),
        in_specs=[pl.BlockSpec((tq, qw), lambda b, g, i: (b * nq + i, g)),
                  pl.BlockSpec((None, None, None, ncp, HEAD_DIM), lambda b, g, i: (b, 0, g, 0, 0)),
                  pl.BlockSpec((None, None, None, ncp, HEAD_DIM), lambda b, g, i: (b, 1, g, 0, 0)),
                  slab(t_kv, lambda b, g, i: (b, kv_col0 + 2 * N_KV + g)),
                  slab(t_kv, lambda b, g, i: (b, kv_col0 + 3 * N_KV + g)),
                  slab(t_win, lambda b, g, i: (b, win_col0 + g)),
                  slab(t_win, lambda b, g, i: (b, win_col0 + N_KV + g)),
                  pl.BlockSpec((tq, LANES), lambda b, g, i: (b * nq + i, 0)),
                  pl.BlockSpec((1, LANES), lambda b, g, i: (0, 0)),
                  pl.BlockSpec((ncp, nsp), lambda b, g, i: (0, 0))],
        out_specs=pl.BlockSpec((tq, qw), lambda b, g, i: (b * nq + i, g)),
        out_shape=jax.ShapeDtypeStruct((B * t_q, N_HEADS * HEAD_DIM), F32),
        compiler_params=_cparams("parallel", "parallel", "arbitrary"),
        name="nsa_attention",
    )(q2d, kcvc, kcvc, kv2d, kv2d, win2d, win2d, gates, gate_b, overlap)


def _overlap_matrix(ncp, nsp, nc_real, ns_real):
    n = jnp.arange(ncp)[:, None]
    s = jnp.arange(nsp)[None, :]
    c0 = n * CMP_STRIDE
    s0 = s * SLC_BLOCK
    ov = (c0 < s0 + SLC_BLOCK) & (c0 + CMP_BLOCK > s0) & (n < nc_real) & (s < ns_real)
    return ov.astype(BF16)


def _gather_pages_kernel(pt_ref, page_ref, tail_ref, o_ref):
    j = pl.program_id(1)
    n_pages = pl.num_programs(1) - 1

    @pl.when(j < n_pages)
    def _():
        o_ref[...] = page_ref[...]

    @pl.when(j == n_pages)
    def _():
        o_ref[...] = tail_ref[...]


def gather_pages(cache5, page_table, tail, li):
    B, n_pages = page_table.shape
    page, row_w = cache5.shape[1], cache5.shape[4]
    grid_spec = pltpu.PrefetchScalarGridSpec(
        num_scalar_prefetch=1,
        grid=(B, n_pages + 1),
        in_specs=[pl.BlockSpec((None, page, None, None, row_w),
                               lambda b, j, pt: (pt[b, jnp.minimum(j, n_pages - 1)], 0, li, 0, 0)),
                  pl.BlockSpec((None, page, row_w), lambda b, j, pt: (b, 0, 0))],
        out_specs=pl.BlockSpec((None, page, row_w), lambda b, j, pt: (b, j, 0)),
    )
    return pl.pallas_call(
        _gather_pages_kernel,
        grid_spec=grid_spec,
        out_shape=jax.ShapeDtypeStruct((B, (n_pages + 1) * page, row_w), F32),
        compiler_params=_cparams("parallel", "arbitrary"),
        name="gather_pages",
    )(page_table, cache5, tail)


def attn_params(li, attn_w_in, attn_gate_b, w_cmp_k, w_cmp_v, attn_w_out):
    qkv_w = N_HEADS * HEAD_DIM + 6 * N_KV * HEAD_DIM
    ng = 3 * N_HEADS
    half = lambda w: jnp.concatenate([w[:CMP_STRIDE], w[CMP_STRIDE:]], axis=-1)
    return dict(
        w_main=attn_w_in[li][:, :qkv_w],
        w_gate=jnp.pad(attn_w_in[li][:, qkv_w:], ((0, 0), (0, LANES - ng))),
        gate_b=jnp.pad(attn_gate_b[li], (0, LANES - ng)).reshape(1, LANES),
        wcat=jnp.stack([half(w_cmp_k[li]), half(w_cmp_v[li])]),
        w_out=attn_w_out[li],
    )


def attn_prompt(x2d, B, T, prm):
    p = mm(x2d, prm["w_main"])
    gates = mm(x2d, prm["w_gate"])
    q_blocks = N_HEADS
    n_pieces = T // CMP_STRIDE
    nc_real = n_pieces - CMP_BLOCK // CMP_STRIDE + 1
    ns_real = T // SLC_BLOCK
    ncp = -(-n_pieces // LANES) * LANES
    nsp = -(-ns_real // LANES) * LANES
    kcvc = nsa_compress(p, q_blocks, B, T, prm["wcat"], ncp, False)
    ov = _overlap_matrix(ncp, nsp, nc_real, ns_real)
    tq = min(128, T)
    o = nsa_attention(p, kcvc, p, q_blocks, p, q_blocks + 4 * N_KV, gates, prm["gate_b"], ov,
                      B=B, t_q=T, t_kv=T, t_win=T, tq=tq, q_off=0, nc_real=nc_real, ns_real=ns_real,
                      tk=min(256, T), n_kv_static=None, w_off=0, w_len=min(WINDOW + tq, T), w_follows_q=True,
                      precise=False, stack_heads=False)
    y = mm(o, prm["w_out"])
    kv = p.reshape(B, T, -1)[:, :, N_HEADS * HEAD_DIM:].reshape(B, T, 6, N_KV, HEAD_DIM)
    w_buf = min(WINDOW, T)
    win = jnp.pad(kv[:, :, 4:], ((0, 0), (WINDOW, 0), (0, 0), (0, 0), (0, 0)))[:, -w_buf:]
    return y, kv[:, :, :4], win


def attn_sample(x2d, B, T, prm, cache_nsa_kv, cache_win, page_table, li):
    n_pool, page = cache_nsa_kv.shape[:2]
    n_attn = cache_nsa_kv.shape[2]
    n_pages = page_table.shape[1]
    past_len = n_pages * page
    row_w = 4 * N_KV * HEAD_DIM
    p = mm(x2d, prm["w_main"], True)
    gates = mm(x2d, prm["w_gate"], True)
    kv_new = p[:, N_HEADS * HEAD_DIM:].reshape(B, T, 6, N_KV, HEAD_DIM)
    tail = jnp.pad(kv_new[:, :, :4].reshape(B, T, row_w), ((0, 0), (0, page - T), (0, 0)))
    full = gather_pages(cache_nsa_kv.reshape(n_pool, page, n_attn, 1, row_w), page_table, tail, li)
    t_kv = (n_pages + 1) * page
    full2d = full.reshape(B * t_kv, row_w)
    L = past_len + T
    l_pad = -(-L // SLC_BLOCK) * SLC_BLOCK
    nc_real = l_pad // CMP_STRIDE - CMP_BLOCK // CMP_STRIDE + 1
    ns_real = l_pad // SLC_BLOCK
    n_pieces = t_kv // CMP_STRIDE
    ncp = -(-n_pieces // LANES) * LANES
    nsp = -(-ns_real // LANES) * LANES
    kcvc = nsa_compress(full2d, 0, B, t_kv, prm["wcat"], ncp, True)
    ov = _overlap_matrix(ncp, nsp, nc_real, ns_real)
    w_buf = cache_win.shape[1]
    win = jnp.concatenate([cache_win, kv_new[:, :, 4:]], axis=1)
    t_win = -(-(w_buf + T) // LANES) * LANES
    win2d = jnp.pad(win.reshape(B, w_buf + T, 2 * N_KV * HEAD_DIM), ((0, 0), (0, t_win - (w_buf + T)), (0, 0)))
    win2d = win2d.reshape(B * t_win, 2 * N_KV * HEAD_DIM)
    n_kv = 3
    assert t_kv % (n_kv * LANES) == 0
    o = nsa_attention(p, kcvc, full2d, 0, win2d, 0, gates, prm["gate_b"], ov,
                      B=B, t_q=T, t_kv=t_kv, t_win=t_win, tq=T, q_off=past_len, nc_real=nc_real, ns_real=ns_real,
                      tk=t_kv // n_kv, n_kv_static=n_kv, w_off=past_len - w_buf, w_len=t_win, w_follows_q=False,
                      precise=True, stack_heads=True)
    y = mm(o, prm["w_out"], True)
    return y, kv_new[:, :, :4], win[:, -w_buf:]


MOE_TM = 256


ROUTER_TN = 256


def _first_argmax(vals):
    best, idx = vals[0], jnp.zeros(vals[0].shape, jnp.int32)
    for i in range(1, len(vals)):
        better = vals[i] > best
        best = jnp.where(better, vals[i], best)
        idx = jnp.where(better, i, idx)
    return best, idx


def _pick_by(idx, vals):
    out = vals[-1]
    for i in range(len(vals) - 2, -1, -1):
        out = jnp.where(idx == i, vals[i], out)
    return out


def _router_kernel(x_ref, wt_ref, bias_ref, e_ref, w_ref, rank_ref, cnt_ref, base, *, n_tok):
    i = pl.program_id(0)
    tn = x_ref.shape[0]

    @pl.when(i == 0)
    def _():
        base[...] = jnp.zeros(base.shape, F32)

    xh, xl = _split2(x_ref[...])
    wh, wl = _split2(wt_ref[...])
    logits = _dot_nt(wh, xh) + _dot_nt(wh, xl) + _dot_nt(wl, xh)
    aff = _sigmoid(logits)
    sel = aff + bias_ref[...]
    s = [sel[e:e + 1, :] for e in range(N_EXPERTS)]
    a = [aff[e:e + 1, :] for e in range(N_EXPERTS)]
    G = EXPERTS_PER_GROUP
    gscore = []
    for g in range(N_GROUPS):
        v0, v1, v2, v3 = s[G * g:G * g + G]
        hi1, lo1 = jnp.maximum(v0, v1), jnp.minimum(v0, v1)
        hi2, lo2 = jnp.maximum(v2, v3), jnp.minimum(v2, v3)
        gscore.append(jnp.maximum(hi1, hi2) + jnp.maximum(jnp.minimum(hi1, hi2), jnp.maximum(lo1, lo2)))
    _, gi = _first_argmax(gscore)
    sg = [_pick_by(gi, [s[G * g + j] for g in range(N_GROUPS)]) for j in range(G)]
    ag = [_pick_by(gi, [a[G * g + j] for g in range(N_GROUPS)]) for j in range(G)]
    _, l0 = _first_argmax(sg)
    _, l1 = _first_argmax([jnp.where(l0 == j, -jnp.inf, sg[j]) for j in range(G)])
    w0, w1 = _pick_by(l0, ag), _pick_by(l1, ag)
    wsum = w0 + w1
    e0, e1 = gi * G + l0, gi * G + l1
    e_ref[...] = jnp.concatenate([e0, e1], axis=0)
    w_ref[...] = jnp.concatenate([w0 / wsum, w1 / wsum], axis=0)
    tok = i * tn + lax.broadcasted_iota(jnp.int32, (N_EXPERTS, tn), 1)
    eid = lax.broadcasted_iota(jnp.int32, (N_EXPERTS, tn), 0)
    valid = tok < n_tok
    oh0 = jnp.where((eid == e0) & valid, 1.0, 0.0)
    oh1 = jnp.where((eid == e1) & valid, 1.0, 0.0)
    cnt = oh0 + oh1
    r_i = lax.broadcasted_iota(jnp.int32, (tn, tn), 0)
    c_i = lax.broadcasted_iota(jnp.int32, (tn, tn), 1)
    upper = jnp.where(r_i <= c_i, 1.0, 0.0).astype(BF16)
    excl = _dot(cnt.astype(BF16), upper) - cnt + base[...]
    rank_ref[...] = jnp.concatenate([jnp.sum(oh0 * excl, axis=0, keepdims=True),
                                     jnp.sum(oh1 * excl, axis=0, keepdims=True)], axis=0).astype(jnp.int32)
    base[...] = base[...] + jnp.sum(cnt, axis=1, keepdims=True)
    cnt_ref[...] = jnp.broadcast_to(base[...], cnt_ref.shape).astype(jnp.int32)


def router(x, w_t, bias_col):
    n, d = x.shape
    tn = ROUTER_TN if n >= ROUTER_TN else LANES
    nt = -(-n // tn)
    n_pad = nt * tn
    if n < tn:
        x = jnp.pad(x, ((0, tn - n), (0, 0)))
    two = lambda dt: jax.ShapeDtypeStruct((TOP_K, n_pad), dt)
    out_row = pl.BlockSpec((TOP_K, tn), lambda i: (0, i))
    e, w, rank, cnt = pl.pallas_call(
        functools.partial(_router_kernel, n_tok=n),
        grid=(nt,),
        in_specs=[pl.BlockSpec((tn, d), lambda i: (i, 0)),
                  pl.BlockSpec((N_EXPERTS, d), lambda i: (0, 0)),
                  pl.BlockSpec((N_EXPERTS, 1), lambda i: (0, 0))],
        out_specs=[out_row, out_row, out_row, pl.BlockSpec((N_EXPERTS, LANES), lambda i: (0, 0))],
        out_shape=[two(jnp.int32), two(F32), two(jnp.int32), jax.ShapeDtypeStruct((N_EXPERTS, LANES), jnp.int32)],
        scratch_shapes=[pltpu.VMEM((N_EXPERTS, 1), F32)],
        compiler_params=_cparams("arbitrary"),
        name="router",
    )(x, w_t, bias_col)
    return e[:, :n], w[:, :n], rank[:, :n], cnt[:, 0]


def _expert_kernel(te_ref, x_ref, rw_ref, wg_ref, wu_ref, wd_ref, o_ref):
    xb = x_ref[...].astype(BF16)
    hg = _dot(xb, wg_ref[...])
    hu = _dot(xb, wu_ref[...])
    h = (hg * _sigmoid(hg)) * hu * rw_ref[...]
    o_ref[...] = _dot(h.astype(BF16), wd_ref[...])


def expert_ffn(x_sorted, row_w, tile_expert, wg, wu, wd, layer):
    a_pad, d = x_sorted.shape
    f = wg.shape[3]
    n_tiles = a_pad // MOE_TM
    grid_spec = pltpu.PrefetchScalarGridSpec(
        num_scalar_prefetch=1,
        grid=(n_tiles,),
        in_specs=[pl.BlockSpec((MOE_TM, d), lambda t, te: (t, 0)),
                  pl.BlockSpec((MOE_TM, 1), lambda t, te: (t, 0)),
                  pl.BlockSpec((None, None, d, f), lambda t, te: (layer, te[t], 0, 0)),
                  pl.BlockSpec((None, None, d, f), lambda t, te: (layer, te[t], 0, 0)),
                  pl.BlockSpec((None, None, f, d), lambda t, te: (layer, te[t], 0, 0))],
        out_specs=pl.BlockSpec((MOE_TM, d), lambda t, te: (t, 0)),
    )
    return pl.pallas_call(
        _expert_kernel,
        grid_spec=grid_spec,
        out_shape=jax.ShapeDtypeStruct((a_pad, d), F32),
        compiler_params=_cparams("arbitrary"),
        name="expert_ffn",
    )(tile_expert, x_sorted, row_w, wg, wu, wd)


MOE_DENSE_FT = 512


def _moe_dense_kernel(x_ref, gate_ref, wg_ref, wu_ref, wd_ref, o_ref):
    e = pl.program_id(0)

    @pl.when((e == 0) & (pl.program_id(1) == 0))
    def _():
        o_ref[...] = jnp.zeros(o_ref.shape, F32)

    x = x_ref[...]
    hg = _dot3(x, wg_ref[...])
    hu = _dot3(x, wu_ref[...])
    gate = gate_ref[...]
    lane = lax.broadcasted_iota(jnp.int32, gate.shape, 1)
    gcol = jnp.sum(jnp.where(lane == e, gate, 0.0), axis=-1, keepdims=True)
    o_ref[...] += _dot3((hg * _sigmoid(hg)) * hu * gcol, wd_ref[...])


def moe_dense_precise(x, router_wt, r_bias_col, w_g, w_u, w_d, layer):
    n, d = x.shape
    f = w_g.shape[3]
    ft = MOE_DENSE_FT
    e, w_sel, _, _ = router(x, router_wt, r_bias_col)
    lanes = jnp.arange(LANES, dtype=jnp.int32)[None, :]
    gate = sum(jnp.where(e[k][:, None] == lanes, w_sel[k][:, None], 0.0) for k in range(TOP_K))
    return pl.pallas_call(
        _moe_dense_kernel,
        grid=(N_EXPERTS, f // ft),
        in_specs=[pl.BlockSpec((n, d), lambda e, j: (0, 0)),
                  pl.BlockSpec((n, LANES), lambda e, j: (0, 0)),
                  pl.BlockSpec((None, None, d, ft), lambda e, j: (layer, e, 0, j)),
                  pl.BlockSpec((None, None, d, ft), lambda e, j: (layer, e, 0, j)),
                  pl.BlockSpec((None, None, ft, d), lambda e, j: (layer, e, j, 0))],
        out_specs=pl.BlockSpec((n, d), lambda e, j: (0, 0)),
        out_shape=jax.ShapeDtypeStruct((n, d), F32),
        compiler_params=_cparams("arbitrary", "arbitrary"),
        name="moe_dense_precise",
    )(x, gate, w_g, w_u, w_d)


def moe(x, router_wt, r_bias_col, w_g, w_u, w_d, layer):
    n, d = x.shape
    e, w_sel, rank, counts = router(x, router_wt, r_bias_col)
    n_asg = n * TOP_K
    gsz = (counts + MOE_TM - 1) // MOE_TM * MOE_TM
    p_end = jnp.cumsum(gsz)
    p_start = p_end - gsz
    onehot = (e[:, :, None] == jnp.arange(N_EXPERTS, dtype=jnp.int32)[None, None, :]).astype(jnp.int32)
    dest = jnp.sum(onehot * p_start[None, None, :], axis=2) + rank
    a_pad = -(-(n_asg + N_EXPERTS * (MOE_TM - 1)) // MOE_TM) * MOE_TM
    tok = jnp.broadcast_to(jnp.arange(n, dtype=jnp.int32)[None, :], (TOP_K, n))
    src_tok = jnp.zeros((a_pad,), jnp.int32).at[dest.reshape(n_asg)].set(tok.reshape(n_asg))
    row_w = jnp.zeros((a_pad,), F32).at[dest.reshape(n_asg)].set(w_sel.reshape(n_asg))
    tile_start = jnp.arange(a_pad // MOE_TM, dtype=jnp.int32) * MOE_TM
    tile_expert = jnp.minimum(jnp.sum((p_end[None, :] <= tile_start[:, None]).astype(jnp.int32), axis=1),
                              N_EXPERTS - 1).astype(jnp.int32)
    ys = expert_ffn(x[src_tok], row_w.reshape(a_pad, 1), tile_expert, w_g, w_u, w_d, layer)
    return ys[dest[0]] + ys[dest[1]]


def kernel(x_prompt, x_sample, cache_nsa_kv, cache_win_kv, state_rwkv, state_rwkv_shift, state_lru_h, state_lru_conv,
           page_table, ln1_g, ln1_b, ln2_g, ln2_b, rec_w_in, rec_mu, rwkv_w0, rwkv_w_up, rwkv_a0, rwkv_a_up, rwkv_g_up,
           rwkv_k_k, rwkv_k_a, rwkv_r_k, rwkv_gn_g, rwkv_gn_b, lru_conv_w, lru_conv_b, lru_wa, lru_ba, lru_wx, lru_bx,
           lru_lambda, rec_w_out, attn_w_in, attn_gate_b, w_cmp_k, w_cmp_v, attn_w_out, moe_w_router, moe_router_bias,
           moe_w_gate, moe_w_up, moe_w_down):
    Bp, Tp, D = x_prompt.shape
    Bs, Ts, _ = x_sample.shape
    n_p, n_s = Bp * Tp, Bs * Ts
    rec_args = (rec_w_in, rec_mu, rwkv_w0, rwkv_w_up, rwkv_a0, rwkv_a_up, rwkv_g_up, rwkv_k_k, rwkv_k_a,
                rwkv_r_k, rwkv_gn_g, rwkv_gn_b, lru_conv_w, lru_conv_b, lru_wa, lru_ba, lru_wx, lru_bx, lru_lambda,
                rec_w_out)
    xp, xs = x_prompt.reshape(n_p, D), x_sample.reshape(n_s, D)
    router_w = moe_w_router.T
    moe_router_bias = moe_router_bias.reshape(N_EXPERTS, 1)
    wg_bf, wu_bf, wd_bf = moe_w_gate.astype(BF16), moe_w_up.astype(BF16), moe_w_down.astype(BF16)
    nsa_p, nsa_s, win_p, win_s = [], [], [], []
    rs_p, rs_s, rsh_p, rsh_s, lh_p, lh_s, lc_p, lc_s = [], [], [], [], [], [], [], []
    for layer in range(DEPTH):
        li = layer // 2
        if layer % 2 == 0:
            prm = rec_params(li, *rec_args)
            yp, S, sh, h, cb = rec_mixer(xp, Bp, Tp, jnp.zeros((Bp, RWKV_HEADS, RWKV_HD, RWKV_HD), F32),
                                         jnp.zeros((Bp, RWKV_SHIFT_W), F32), jnp.zeros((Bp, LRU_W), F32),
                                         jnp.zeros((Bp, CONV_W - 1, LRU_W), F32), prm)
            ys, S2, sh2, h2, cb2 = rec_mixer(xs, Bs, Ts, state_rwkv[:, li], state_rwkv_shift[:, li],
                                             state_lru_h[:, li], state_lru_conv[:, li], prm, precise=True)
            rs_p.append(S); rs_s.append(S2); rsh_p.append(sh); rsh_s.append(sh2)
            lh_p.append(h); lh_s.append(h2); lc_p.append(cb); lc_s.append(cb2)
        else:
            prm = attn_params(li, attn_w_in, attn_gate_b, w_cmp_k, w_cmp_v, attn_w_out)
            yp, rows, wb = attn_prompt(xp, Bp, Tp, prm)
            ys, rows2, wb2 = attn_sample(xs, Bs, Ts, prm, cache_nsa_kv, cache_win_kv[:, :, li], page_table, li)
            nsa_p.append(rows); nsa_s.append(rows2); win_p.append(wb); win_s.append(wb2)
        xp = ln_res(xp, yp, ln1_g[layer], ln1_b[layer])
        xs = ln_res(xs, ys, ln1_g[layer], ln1_b[layer])
        xp = ln_res(xp, moe(xp, router_w, moe_router_bias, wg_bf, wu_bf, wd_bf, layer), ln2_g[layer], ln2_b[layer])
        xs = ln_res(xs, moe_dense_precise(xs, router_w, moe_router_bias, moe_w_gate, moe_w_up, moe_w_down, layer),
                    ln2_g[layer], ln2_b[layer])
    return (xp.reshape(Bp, Tp, D), xs.reshape(Bs, Ts, D),
            jnp.stack(nsa_p, axis=2), jnp.stack(nsa_s, axis=2), jnp.stack(win_p, axis=2), jnp.stack(win_s, axis=2),
            jnp.stack(rs_p, axis=1), jnp.stack(rs_s, axis=1), jnp.stack(rsh_p, axis=1), jnp.stack(rsh_s, axis=1),
            jnp.stack(lh_p, axis=1), jnp.stack(lh_s, axis=1), jnp.stack(lc_p, axis=1), jnp.stack(lc_s, axis=1))
```

```python
import functools
import math

import jax
import jax.numpy as jnp
from jax import lax
from jax.experimental import pallas as pl
from jax.experimental.pallas import tpu as pltpu

F32 = jnp.float32
BF16 = jnp.bfloat16

D_MODEL = 2048
DEPTH = 4
RWKV_HEADS = 16
RWKV_HD = 64
RWKV_W = RWKV_HEADS * RWKV_HD
LORA_DECAY = 64
LORA_A = 64
LORA_GATE = 128
RWKV_SHIFT_W = 3 * RWKV_W + LORA_DECAY + LORA_A + LORA_GATE
GN_EPS = 64e-5
LRU_W = D_MODEL - RWKV_W
LRU_BLOCKS = 16
CONV_W = 4
LRU_C = 8.0
N_HEADS = 16
HEAD_DIM = 128
N_KV = 4
GQA_R = N_HEADS // N_KV
CMP_BLOCK = 32
CMP_STRIDE = 16
SLC_BLOCK = 64
TOP_N = 16
WINDOW = 512
SCALE = HEAD_DIM ** -0.5
N_EXPERTS = 16
N_GROUPS = 4
EXPERTS_PER_GROUP = N_EXPERTS // N_GROUPS
TOP_K = 2
D_EXPERT = 1024
LN_EPS = 1e-5
DN_ALPHA = (2 * DEPTH) ** 0.25
NEG = -1e30
BIG = 1e4

LANES = 128
SUBLANES = 8
VMEM_LIMIT_BYTES = 48 * 1024 * 1024

RWKV_PAD_W = 3584
PAIRS = RWKV_HEADS // 2


def _cparams(*sem):
    return pltpu.CompilerParams(dimension_semantics=sem, vmem_limit_bytes=VMEM_LIMIT_BYTES)


def _split2(x):
    hi = x.astype(BF16)
    lo = (x - hi.astype(F32)).astype(BF16)
    return hi, lo


def _dot(a, b):
    return jnp.dot(a, b, preferred_element_type=F32)


def _dot2(x, w_bf):
    hi, lo = _split2(x)
    return _dot(hi, w_bf) + _dot(lo, w_bf)


def _sigmoid(x):
    return 1.0 / (1.0 + jnp.exp(-x))


def _softplus(x):
    return jnp.maximum(x, 0.0) + jnp.log(1.0 + jnp.exp(-jnp.abs(x)))


def _mdot(a, b, precise):
    return _dot3(a, b) if precise else _dot(a.astype(BF16), b.astype(BF16))


def _mdot_nt(a, b, precise):
    return _dot3_nt(a, b) if precise else _dot_nt(a.astype(BF16), b.astype(BF16))


def _mm_kernel(x_ref, w_ref, o_ref, *, precise):
    o_ref[...] = _mdot(x_ref[...], w_ref[...], precise)


def _pick(n, cands):
    for c in cands:
        if n % c == 0:
            return c
    return n


def mm(x, w, precise=False):
    m, k = x.shape
    n = w.shape[1]
    tm = _pick(m, (512, 256, 128, 64, 32, 16, 8))
    tn = _pick(n, (512, 256, 128) if precise else (1024, 768, 512, 256, 128))
    return pl.pallas_call(
        functools.partial(_mm_kernel, precise=precise),
        grid=(n // tn, m // tm),
        in_specs=[pl.BlockSpec((tm, k), lambda j, i: (i, 0)),
                  pl.BlockSpec((k, tn), lambda j, i: (0, j))],
        out_specs=pl.BlockSpec((tm, tn), lambda j, i: (i, j)),
        out_shape=jax.ShapeDtypeStruct((m, n), F32),
        compiler_params=_cparams("parallel", "parallel"),
        name="mm",
    )(x, w)


def _mm2_kernel(x1_ref, x2_ref, w1_ref, w2_ref, o_ref, *, precise):
    o_ref[...] = _mdot(x1_ref[...], w1_ref[...], precise) + _mdot(x2_ref[...], w2_ref[...], precise)


def mm_cat2(x1, x2, w, precise=False):
    m, k1 = x1.shape
    k2 = x2.shape[1]
    assert k1 == k2 and w.shape[0] == k1 + k2
    n = w.shape[1]
    tm = _pick(m, (512, 256, 128, 64, 32, 16, 8))
    tn = _pick(n, (512, 256, 128) if precise else (1024, 512, 256, 128))
    return pl.pallas_call(
        functools.partial(_mm2_kernel, precise=precise),
        grid=(n // tn, m // tm),
        in_specs=[pl.BlockSpec((tm, k1), lambda j, i: (i, 0)),
                  pl.BlockSpec((tm, k2), lambda j, i: (i, 0)),
                  pl.BlockSpec((k1, tn), lambda j, i: (0, j)),
                  pl.BlockSpec((k2, tn), lambda j, i: (1, j))],
        out_specs=pl.BlockSpec((tm, tn), lambda j, i: (i, j)),
        out_shape=jax.ShapeDtypeStruct((m, n), F32),
        compiler_params=_cparams("parallel", "parallel"),
        name="mm_cat2",
    )(x1, x2, w, w)


def _ln_res_kernel(x_ref, y_ref, g_ref, b_ref, o_ref):
    z = DN_ALPHA * x_ref[...] + y_ref[...]
    mu = jnp.mean(z, axis=-1, keepdims=True)
    zc = z - mu
    var = jnp.mean(zc * zc, axis=-1, keepdims=True)
    o_ref[...] = zc * lax.rsqrt(var + LN_EPS) * g_ref[...] + b_ref[...]


def ln_res(x, y, g, b):
    m, d = x.shape
    tm = _pick(m, (256, 192, 128, 64, 32, 16, 8))
    row = pl.BlockSpec((tm, d), lambda i: (i, 0))
    vec = pl.BlockSpec((1, d), lambda i: (0, 0))
    return pl.pallas_call(
        _ln_res_kernel,
        grid=(m // tm,),
        in_specs=[row, row, vec, vec],
        out_specs=row,
        out_shape=jax.ShapeDtypeStruct((m, d), F32),
        compiler_params=_cparams("parallel"),
        name="ln_res",
    )(x, y, g.reshape(1, d), b.reshape(1, d))


def _shifted(carry_ref, x, j):
    ext = jnp.concatenate([carry_ref[...], x], axis=0)
    return pltpu.roll(ext, j, axis=0)[SUBLANES:, :]


def _rwkv_prep_kernel(p_ref, sh0_ref, mu_ref, w0_ref, a0_ref, wlora_ref, alora_ref, gup_ref,
                      kk_ref, ka_ref, rk_ref, bd_ref, *refs, chunked, precise):
    outs, carry = refs[:-1], refs[-1]
    ti = pl.program_id(1)

    @pl.when(ti == 0)
    def _():
        carry[...] = jnp.broadcast_to(sh0_ref[0], carry.shape)

    p = p_ref[...]
    tm = p.shape[0]
    p_prev = _shifted(carry, p, 1)
    carry[...] = p[tm - SUBLANES:, :]
    pm = p + mu_ref[...] * (p_prev - p)
    W = RWKV_W
    r = pm[:, 0:W]
    k = pm[:, W:2 * W]
    v = pm[:, 2 * W:3 * W]
    lo = pm[:, 3 * W:3 * W + LORA_DECAY + LORA_A]
    gd = pm[:, 3 * W + LORA_DECAY + LORA_A:RWKV_SHIFT_W]
    lane = lax.broadcasted_iota(jnp.int32, lo.shape, 1)
    z = jnp.where(lane < LORA_DECAY, jnp.tanh(lo), lo)
    w_l = _mdot(z, wlora_ref[...], precise)
    a_l = _mdot(z, alora_ref[...], precise)
    w = -_softplus(-(w0_ref[...] + w_l)) - 0.5
    log_decay = -jnp.exp(w)
    decay = jnp.exp(log_decay)
    a = _sigmoid(a0_ref[...] + a_l)
    g = _mdot(_sigmoid(gd), gup_ref[...], precise)
    bd = bd_ref[...]
    kk = k * kk_ref[...]
    ss = _dot2(kk * kk, bd)
    kn = kk * lax.rsqrt(jnp.maximum(ss, 1e-24))
    k_eff = k * (1.0 + (a - 1.0) * ka_ref[...])
    bvec = kn * a
    bonus = _dot2(r * k_eff * rk_ref[...], bd) * v
    if chunked:
        vals = (r, log_decay, k_eff, kn, bvec, v, g, bonus)
    else:
        vals = (r, decay, k_eff, kn, bvec, v, decay * r, _dot2(bvec * r, bd), _dot2(k_eff * r, bd), g, bonus)
    for o_ref, val in zip(outs, vals):
        o_ref[...] = val


def rwkv_prep(p_rw, sh0, B, T, prm, chunked, precise):
    n_out = 8 if chunked else 11
    tm = _pick(T, (256, 128, 64, 32, 16, 8))
    nt = T // tm
    W = RWKV_W
    row_in = pl.BlockSpec((tm, RWKV_PAD_W), lambda b, t: (b * nt + t, 0))
    row_out = pl.BlockSpec((tm, W), lambda b, t: (b * nt + t, 0))
    full = lambda a: pl.BlockSpec(a.shape, lambda b, t: (0,) * a.ndim)
    sh0p = sh0.reshape(B, 1, RWKV_PAD_W)
    consts = [prm["mu"], prm["w0"], prm["a0"], prm["wlora"], prm["alora"], prm["gup"],
              prm["k_k"], prm["k_a"], prm["r_k"], prm["bd"]]
    outs = pl.pallas_call(
        functools.partial(_rwkv_prep_kernel, chunked=chunked, precise=precise),
        grid=(B, nt),
        in_specs=[row_in, pl.BlockSpec((1, 1, RWKV_PAD_W), lambda b, t: (b, 0, 0))] + [full(c) for c in consts],
        out_specs=[row_out] * n_out,
        out_shape=[jax.ShapeDtypeStruct((B * T, W), F32)] * n_out,
        scratch_shapes=[pltpu.VMEM((SUBLANES, RWKV_PAD_W), F32)],
        compiler_params=_cparams("arbitrary", "arbitrary"),
        name="rwkv_prep",
    )(p_rw, sh0p, *consts)
    return outs


PAIR_GROUP = 4
RWKV_SCAN_BATCH = 2


def _rwkv_scan_kernel(r_ref, w_ref, k_ref, kn_ref, b_ref, v_ref, wr_ref, br_ref, kr_ref, s0_ref, oseg_ref,
                      y_ref, sfin_ref, s_scr, *, nb, tc):
    ti = pl.program_id(1)

    @pl.when(ti == 0)
    def _():
        s_scr[...] = s0_ref[...]

    row = lax.broadcasted_iota(jnp.int32, (RWKV_HD, LANES), 0)
    lane = lax.broadcasted_iota(jnp.int32, (RWKV_HD, LANES), 1)
    diag = (lane & (RWKV_HD - 1)) == row
    oseg = oseg_ref[...]
    n_pairs = nb * PAIRS
    R = RWKV_HD

    def step8(t8, carry):
        base = pl.multiple_of(t8 * SUBLANES, SUBLANES)
        for g0 in range(0, n_pairs, PAIR_GROUP):
            ids = [(i // PAIRS, i % PAIRS) for i in range(g0, g0 + PAIR_GROUP)]
            n = len(ids)
            tile = lambda ref, b, p: ref[b, pl.ds(base, SUBLANES), p * LANES:(p + 1) * LANES]
            tiles = {nm: [tile(ref, b, p) for (b, p) in ids]
                     for nm, ref in (("kn", kn_ref), ("wr", wr_ref), ("v", v_ref), ("w", w_ref), ("b", b_ref),
                                     ("k", k_ref), ("br", br_ref), ("kr", kr_ref))}
            s_cur = [s_scr[b * PAIRS + p] for (b, p) in ids]
            y_rows = [[] for _ in ids]
            for j in range(SUBLANES):
                row = lambda nm, i: tiles[nm][i][j:j + 1, :]
                blocks = ([s_cur[i] * row("kn", i) for i in range(n)]
                          + [s_cur[i] * row("wr", i) for i in range(n)]
                          + [jnp.where(diag, row("v", i), 0.0) for i in range(n)])
                red = _dot2(jnp.concatenate(blocks, axis=0), oseg)
                for i in range(n):
                    skk = red[i * R:(i + 1) * R]
                    u = red[(n + i) * R:(n + i + 1) * R]
                    vb = red[(2 * n + i) * R:(2 * n + i + 1) * R]
                    s_cur[i] = s_cur[i] * row("w", i) - skk * row("b", i) + vb * row("k", i)
                    y_bc = u - skk * row("br", i) + vb * row("kr", i)
                    y_rows[i].append(jnp.sum(jnp.where(diag, y_bc, 0.0), axis=0, keepdims=True))
            for i, (b, p) in enumerate(ids):
                s_scr[b * PAIRS + p] = s_cur[i]
                y_ref[b, pl.ds(base, SUBLANES), p * LANES:(p + 1) * LANES] = jnp.concatenate(y_rows[i], axis=0)
        return carry

    lax.fori_loop(0, tc // SUBLANES, step8, 0)

    @pl.when(ti == pl.num_programs(1) - 1)
    def _():
        sfin_ref[...] = s_scr[...]


def rwkv_scan(seqs, s0, B, T):
    tc = _pick(T, (128, 64, 32, 16, 8))
    nb = RWKV_SCAN_BATCH
    assert B % nb == 0
    views = [s.reshape(B, T, RWKV_W) for s in seqs]
    s0p = s0.reshape(B, PAIRS, 2, RWKV_HD, RWKV_HD).transpose(0, 1, 3, 2, 4).reshape(B * PAIRS, RWKV_HD, LANES)
    li = jnp.arange(LANES)
    oseg = (li[:, None] // RWKV_HD == li[None, :] // RWKV_HD).astype(BF16)
    seq_spec = pl.BlockSpec((nb, tc, RWKV_W), lambda b, t: (b, t, 0))
    st_spec = pl.BlockSpec((nb * PAIRS, RWKV_HD, LANES), lambda b, t: (b, 0, 0))
    y, sfin = pl.pallas_call(
        functools.partial(_rwkv_scan_kernel, nb=nb, tc=tc),
        grid=(B // nb, T // tc),
        in_specs=[seq_spec] * 9 + [st_spec, pl.BlockSpec((LANES, LANES), lambda b, t: (0, 0))],
        out_specs=[seq_spec, st_spec],
        out_shape=[jax.ShapeDtypeStruct((B, T, RWKV_W), F32),
                   jax.ShapeDtypeStruct((B * PAIRS, RWKV_HD, LANES), F32)],
        scratch_shapes=[pltpu.VMEM((nb * PAIRS, RWKV_HD, LANES), F32)],
        compiler_params=_cparams("arbitrary", "arbitrary"),
        name="rwkv_scan",
    )(*views, s0p, oseg)
    sfin = sfin.reshape(B, PAIRS, RWKV_HD, 2, RWKV_HD).transpose(0, 1, 3, 2, 4).reshape(B, RWKV_HEADS, RWKV_HD, RWKV_HD)
    return y.reshape(B * T, RWKV_W), sfin


RWKV_CHUNK = 64
CHUNK_PAIRS = 8


def _split_bf(x):
    hi = x.astype(BF16)
    return hi, (x - hi.astype(F32)).astype(BF16)


def _dot3(a, b):
    ah, al = _split_bf(a)
    bh, bl = _split_bf(b)
    return _dot(ah, bh) + _dot(al, bh) + _dot(ah, bl)


def _dot3_nt(a, b):
    ah, al = _split_bf(a)
    bh, bl = _split_bf(b)
    return _dot_nt(ah, bh) + _dot_nt(al, bh) + _dot_nt(ah, bl)


def _rwkv_chunk_math(sts, rs, lws, ks, kns, bs, vs):
    C = RWKV_CHUNK
    lane = lax.broadcasted_iota(jnp.int32, (C, LANES), 1)
    row = lax.broadcasted_iota(jnp.int32, (C, LANES), 0)
    head0 = lane < RWKV_HD
    by_head = lambda x: jnp.concatenate([jnp.where(head0, x, 0.0), jnp.where(head0, 0.0, x)], axis=0)
    each = lambda f, *ls: [f(*a) for a in zip(*ls)]

    def cumsum_rows(lw):
        cs, d = lw, 1
        while d < C:
            cs = cs + jnp.where(row >= d, pltpu.roll(cs, d, axis=0), 0.0)
            d *= 2
        return cs

    css = each(cumsum_rows, lws)
    totals = [cs[C - 1:C, :] for cs in css]
    kps = each(lambda kn, cs, lw: kn * jnp.exp(cs - lw), kns, css, lws)
    rps = each(lambda r, cs: r * jnp.exp(cs), rs, css)
    e_negs = [jnp.exp(-cs) for cs in css]
    e_rems = each(lambda t, cs: jnp.exp(t - cs), totals, css)
    grams = each(lambda kp, rp, k, b, e: _dot3_nt(jnp.concatenate([kp, rp], axis=0),
                                                  jnp.concatenate([by_head(k * e), by_head(b * e)], axis=0)),
                 kps, rps, ks, bs, e_negs)
    s_idx = lane & (C - 1)
    strict, incl = s_idx < row, s_idx <= row
    a_ks = [jnp.where(strict, g[:C, :LANES], 0.0) for g in grams]
    a_bs = [jnp.where(strict, g[:C, LANES:], 0.0) for g in grams]
    d_ks = [jnp.where(incl, g[C:, :LANES], 0.0) for g in grams]
    d_bs = [jnp.where(incl, g[C:, LANES:], 0.0) for g in grams]
    r2 = lax.broadcasted_iota(jnp.int32, (2 * C, LANES), 0)
    l2 = lax.broadcasted_iota(jnp.int32, (2 * C, LANES), 1)
    eye = r2 == l2
    n_pows = [-by_head(a) for a in a_bs]
    t_invs = [jnp.where(eye, 1.0, 0.0) + n for n in n_pows]
    p = 2
    while p < C:
        n_pows = each(lambda n: _dot3(n, n), n_pows)
        t_invs = each(lambda t, n: t + _dot3(t, n), t_invs, n_pows)
        p *= 2
    v_bhs = [by_head(v) for v in vs]
    rhss = each(lambda kp, a, st, vb: _dot3(jnp.concatenate([kp, a], axis=1), jnp.concatenate([st, vb], axis=0)),
                kps, a_ks, sts, v_bhs)
    zs = each(lambda t, rhs: _dot3(t[:C] + t[C:], by_head(rhs)), t_invs, rhss)
    ys = each(lambda rp, dk, db, st, vb, z: _dot3(jnp.concatenate([rp, dk, -db], axis=1),
                                                  jnp.concatenate([st, vb, by_head(z)], axis=0)),
              rps, d_ks, d_bs, sts, v_bhs, zs)
    same_head = (r2 < RWKV_HD) == (l2 < RWKV_HD)

    def new_state(st, k, b, e, total, v, z):
        g_total = jnp.where(eye, jnp.broadcast_to(jnp.exp(total), (2 * C, LANES)), 0.0)
        kb_t = jnp.transpose(jnp.concatenate([k * e, b * e], axis=0))
        upd = _dot3(kb_t, jnp.concatenate([v, -z], axis=0))
        return _dot3(g_total, st) + jnp.where(same_head, upd, 0.0)

    return ys, each(new_state, sts, ks, bs, e_rems, totals, vs, zs)


def _rwkv_chunk_kernel(r_ref, lw_ref, k_ref, kn_ref, b_ref, v_ref, s0_ref, y_ref, sfin_ref, s_scr):
    ci = pl.program_id(2)

    @pl.when(ci == 0)
    def _():
        s_scr[...] = s0_ref[...]

    sls = [slice(i * LANES, (i + 1) * LANES) for i in range(CHUNK_PAIRS)]
    cols = lambda ref: [ref[:, sl] for sl in sls]
    ys, sts = _rwkv_chunk_math([s_scr[i] for i in range(CHUNK_PAIRS)], cols(r_ref), cols(lw_ref), cols(k_ref),
                               cols(kn_ref), cols(b_ref), cols(v_ref))
    for i, sl in enumerate(sls):
        y_ref[:, sl] = ys[i]
        s_scr[i] = sts[i]

    @pl.when(ci == pl.num_programs(2) - 1)
    def _():
        sfin_ref[...] = s_scr[...]


def rwkv_scan_chunked(seqs, s0, B, T):
    C, PP = RWKV_CHUNK, CHUNK_PAIRS
    assert T % C == 0 and PAIRS % PP == 0
    npg = PAIRS // PP
    views = [s.reshape(B, T, RWKV_W) for s in seqs]
    s0t = s0.reshape(B, PAIRS, 2, RWKV_HD, RWKV_HD).transpose(0, 1, 2, 4, 3)
    eye2 = jnp.eye(2, dtype=F32)
    s0bd = (s0t[:, :, :, :, None, :] * eye2[None, None, :, None, :, None]).reshape(B * PAIRS, LANES, LANES)
    seq_spec = pl.BlockSpec((None, C, PP * LANES), lambda b, g, c: (b, c, g))
    st_spec = pl.BlockSpec((PP, LANES, LANES), lambda b, g, c: (b * npg + g, 0, 0))
    y, sfin = pl.pallas_call(
        _rwkv_chunk_kernel,
        grid=(B, npg, T // C),
        in_specs=[seq_spec] * 6 + [st_spec],
        out_specs=[seq_spec, st_spec],
        out_shape=[jax.ShapeDtypeStruct((B, T, RWKV_W), F32),
                   jax.ShapeDtypeStruct((B * PAIRS, LANES, LANES), F32)],
        scratch_shapes=[pltpu.VMEM((PP, LANES, LANES), F32)],
        compiler_params=_cparams("parallel", "parallel", "arbitrary"),
        name="rwkv_chunk",
    )(*views, s0bd)
    sf = sfin.reshape(B, PAIRS, 2, RWKV_HD, 2, RWKV_HD)
    sf = jnp.stack([sf[:, :, 0, :, 0, :], sf[:, :, 1, :, 1, :]], axis=2)
    return y.reshape(B * T, RWKV_W), sf.transpose(0, 1, 2, 4, 3).reshape(B, RWKV_HEADS, RWKV_HD, RWKV_HD)


def _rwkv_post_kernel(y_ref, bonus_ref, g_ref, gng_ref, gnb_ref, bd_ref, o_ref):
    y = y_ref[...]
    bd = bd_ref[...]
    inv = 1.0 / RWKV_HD
    mean = _dot2(y, bd) * inv
    yc = y - mean
    var = _dot2(yc * yc, bd) * inv
    yn = yc * lax.rsqrt(var + GN_EPS) * gng_ref[...] + gnb_ref[...]
    o_ref[...] = (yn + bonus_ref[...]) * g_ref[...]


def rwkv_post(y, bonus, g, gn_g, gn_b, bd):
    m, W = y.shape
    tm = _pick(m, (256, 128, 64, 32, 16, 8))
    row = pl.BlockSpec((tm, W), lambda i: (i, 0))
    vec = pl.BlockSpec((1, W), lambda i: (0, 0))
    return pl.pallas_call(
        _rwkv_post_kernel,
        grid=(m // tm,),
        in_specs=[row, row, row, vec, vec, pl.BlockSpec((W, W), lambda i: (0, 0))],
        out_specs=row,
        out_shape=jax.ShapeDtypeStruct((m, W), F32),
        compiler_params=_cparams("parallel"),
        name="rwkv_post",
    )(y, bonus, g, gn_g.reshape(1, W), gn_b.reshape(1, W), bd)


def _gelu_tanh(x):
    return 0.5 * x * (1.0 + jnp.tanh(math.sqrt(2.0 / math.pi) * (x + 0.044715 * (x * x * x))))


def _neg_expm1(x):
    series = -x * (1.0 + x * (1.0 / 2.0) * (1.0 + x * (1.0 / 3.0) * (1.0 + x * (1.0 / 4.0) * (1.0 + x * (1.0 / 5.0)
             * (1.0 + x * (1.0 / 6.0))))))
    return jnp.where(x > -0.05, series, 1.0 - jnp.exp(x))


def _lru_kernel(px_ref, pg_ref, c0_ref, h0_ref, cw_ref, cb_ref, wa_ref, ba_ref, wx_ref, bx_ref, lam_ref,
                y_ref, hlast_ref, carry, hcar, *, precise):
    ti = pl.program_id(1)

    @pl.when(ti == 0)
    def _():
        carry[...] = c0_ref[0]
        hcar[...] = h0_ref[0]

    x = px_ref[...]
    tm = x.shape[0]
    cw = cw_ref[...]
    xc = cb_ref[...] + cw[CONV_W - 1:CONV_W, :] * x
    for j in range(1, CONV_W):
        xc = xc + cw[CONV_W - 1 - j:CONV_W - j, :] * _shifted(carry, x, j)
    carry[...] = x[tm - SUBLANES:, :]
    xcm = xc if precise else xc.astype(BF16)
    r = _sigmoid(_mdot(xcm, wa_ref[...], precise) + ba_ref[...])
    i = _sigmoid(_mdot(xcm, wx_ref[...], precise) + bx_ref[...])
    log_a = (-LRU_C * _softplus(-lam_ref[...])) * r
    a = jnp.exp(log_a)
    b = jnp.sqrt(_neg_expm1(2.0 * log_a)) * (i * xc)
    rows = lax.broadcasted_iota(jnp.int32, a.shape, 0)
    d = 1
    while d < tm:
        a_sh = pltpu.roll(a, d, axis=0)
        b_sh = pltpu.roll(b, d, axis=0)
        keep = rows >= d
        b = jnp.where(keep, a * b_sh + b, b)
        a = jnp.where(keep, a * a_sh, a)
        d *= 2
    h = a * hcar[...] + b
    hcar[...] = h[tm - 1:tm, :]
    hlast_ref[0] = h[tm - 1:tm, :]
    y_ref[...] = h * _gelu_tanh(pg_ref[...])


def lru_mix(p_l, c0, h0, B, T, prm, precise):
    W = LRU_W
    tm = _pick(T, (256, 128, 64, 32, 16, 8))
    nt = T // tm
    c0p = jnp.concatenate([jnp.zeros((B, SUBLANES - (CONV_W - 1), W), F32), c0], axis=1)
    h0p = h0.reshape(B, 1, W)
    full = lambda a: pl.BlockSpec(a.shape, lambda b, t: (0,) * a.ndim)
    consts = [prm["conv_w"], prm["conv_b"], prm["wa"], prm["ba"], prm["wx"], prm["bx"], prm["lam"]]
    y, hlast = pl.pallas_call(
        functools.partial(_lru_kernel, precise=precise),
        grid=(B, nt),
        in_specs=[pl.BlockSpec((tm, W), lambda b, t: (b * nt + t, 0)),
                  pl.BlockSpec((tm, W), lambda b, t: (b * nt + t, 1)),
                  pl.BlockSpec((1, SUBLANES, W), lambda b, t: (b, 0, 0)),
                  pl.BlockSpec((1, 1, W), lambda b, t: (b, 0, 0))] + [full(c) for c in consts],
        out_specs=[pl.BlockSpec((tm, W), lambda b, t: (b * nt + t, 0)),
                   pl.BlockSpec((1, 1, W), lambda b, t: (b, 0, 0))],
        out_shape=[jax.ShapeDtypeStruct((B * T, W), F32), jax.ShapeDtypeStruct((B, 1, W), F32)],
        scratch_shapes=[pltpu.VMEM((SUBLANES, W), F32), pltpu.VMEM((1, W), F32)],
        compiler_params=_cparams("arbitrary", "arbitrary"),
        name="lru_mix",
    )(p_l, p_l, c0p, h0p, *consts)
    return y, hlast.reshape(B, W)


def _block_diag(w):
    n, d, e = w.shape
    eye = jnp.eye(n, dtype=w.dtype)
    return (eye[:, None, :, None] * w[:, :, None, :]).reshape(n * d, n * e)


def rec_params(li, rec_w_in, rec_mu, rwkv_w0, rwkv_w_up, rwkv_a0, rwkv_a_up, rwkv_g_up, rwkv_k_k, rwkv_k_a,
               rwkv_r_k, rwkv_gn_g, rwkv_gn_b, lru_conv_w, lru_conv_b, lru_wa, lru_ba, lru_wx, lru_bx, lru_lambda,
               rec_w_out):
    W = RWKV_W
    pad = RWKV_PAD_W - RWKV_SHIFT_W
    hi = jnp.arange(W) // RWKV_HD
    zl = jnp.zeros((LORA_DECAY, W), F32)
    return dict(
        w_rw=jnp.pad(rec_w_in[li][:, :RWKV_SHIFT_W], ((0, 0), (0, pad))),
        w_l=rec_w_in[li][:, RWKV_SHIFT_W:],
        mu=jnp.pad(rec_mu[li], (0, pad)).reshape(1, RWKV_PAD_W),
        w0=rwkv_w0[li].reshape(1, W), a0=rwkv_a0[li].reshape(1, W),
        wlora=jnp.concatenate([rwkv_w_up[li], zl], axis=0),
        alora=jnp.concatenate([zl, rwkv_a_up[li]], axis=0),
        gup=rwkv_g_up[li],
        k_k=rwkv_k_k[li].reshape(1, W), k_a=rwkv_k_a[li].reshape(1, W), r_k=rwkv_r_k[li].reshape(1, W),
        bd=(hi[:, None] == hi[None, :]).astype(BF16),
        gn_g=rwkv_gn_g[li], gn_b=rwkv_gn_b[li],
        conv_w=lru_conv_w[li], conv_b=lru_conv_b[li].reshape(1, LRU_W),
        wa=_block_diag(lru_wa[li]), ba=lru_ba[li].reshape(1, LRU_W),
        wx=_block_diag(lru_wx[li]), bx=lru_bx[li].reshape(1, LRU_W),
        lam=lru_lambda[li].reshape(1, LRU_W),
        w_out=rec_w_out[li],
    )


def rec_mixer(x2d, B, T, S0, sh0, h0, c0, prm, precise=False):
    p_rw = mm(x2d, prm["w_rw"], precise)
    p_l = mm(x2d, prm["w_l"], precise)
    sh0p = jnp.pad(sh0, ((0, 0), (0, RWKV_PAD_W - RWKV_SHIFT_W)))
    chunked = T % RWKV_CHUNK == 0
    outs = rwkv_prep(p_rw, sh0p, B, T, prm, chunked, precise)
    g, bonus = outs[-2:]
    if chunked:
        y_raw, s_fin = rwkv_scan_chunked(outs[:6], S0, B, T)
    else:
        y_raw, s_fin = rwkv_scan(outs[:9], S0, B, T)
    y_rw = rwkv_post(y_raw, bonus, g, prm["gn_g"], prm["gn_b"], prm["bd"])
    y_lru, h_last = lru_mix(p_l, c0, h0, B, T, prm, precise)
    y = mm_cat2(y_rw, y_lru, prm["w_out"], precise)
    sh = p_rw.reshape(B, T, RWKV_PAD_W)[:, -1, :RWKV_SHIFT_W]
    px = p_l.reshape(B, T, 2 * LRU_W)[:, :, :LRU_W]
    cbuf = jnp.concatenate([c0, px], axis=1)[:, -(CONV_W - 1):] if T < CONV_W - 1 else px[:, -(CONV_W - 1):]
    return y, s_fin, sh, h_last, cbuf


def _dot_nt(a, b):
    return lax.dot_general(a, b, (((1,), (1,)), ((), ())), preferred_element_type=F32)


def _compress_kernel(slab_ref, w_ref, o_ref, *, n_pieces, precise):
    acc = None
    for l in range(CMP_STRIDE):
        d = _mdot(slab_ref[pl.ds(l, n_pieces, stride=CMP_STRIDE), :], w_ref[l], precise)
        acc = d if acc is None else acc + d
    out = acc[:, :HEAD_DIM] + pltpu.roll(acc[:, HEAD_DIM:], n_pieces - 1, axis=0)
    o_ref[0:n_pieces, :] = out
    pad = o_ref.shape[0] - n_pieces
    if pad:
        o_ref[n_pieces:, :] = jnp.zeros((pad, HEAD_DIM), F32)


def nsa_compress(kv2d, col0, B, t_kv, wcat, ncp, precise):
    n_pieces = t_kv // CMP_STRIDE
    return pl.pallas_call(
        functools.partial(_compress_kernel, n_pieces=n_pieces, precise=precise),
        grid=(B, 2, N_KV),
        in_specs=[pl.BlockSpec((t_kv, HEAD_DIM), lambda b, c, g: (b, col0 + c * N_KV + g)),
                  pl.BlockSpec((None, CMP_STRIDE, HEAD_DIM, 2 * HEAD_DIM), lambda b, c, g: (c, 0, 0, 0))],
        out_specs=pl.BlockSpec((None, None, None, ncp, HEAD_DIM), lambda b, c, g: (b, c, g, 0, 0)),
        out_shape=jax.ShapeDtypeStruct((B, 2, N_KV, ncp, HEAD_DIM), F32),
        compiler_params=_cparams("parallel", "parallel", "parallel"),
        name="nsa_compress",
    )(kv2d, wcat)


MASKED_DIST = 1e30
EXP2_SCALE = SCALE * math.log2(math.e)


def _parts(a, precise):
    return _split_bf(a) if precise else (a.astype(BF16),)


def _pdot(a, b, nt=False):
    dot = _dot_nt if nt else _dot
    out = dot(a[0], b[0])
    if len(a) > 1:
        out = out + dot(a[1], b[0]) + dot(a[0], b[1])
    return out


def _online_softmax_step(q, k, v, cdist, m, l, acc, precise):
    u = _pdot(q, k, nt=True) - cdist
    m_new = jnp.maximum(m, jnp.max(u, axis=-1, keepdims=True))
    alpha = jnp.exp2((m - m_new) * EXP2_SCALE)
    p = jnp.exp2((u - m_new) * EXP2_SCALE)
    l = alpha * l + jnp.sum(p, axis=-1, keepdims=True)
    acc = alpha * acc + _pdot(_parts(p, precise), v)
    return m_new, l, acc


def _nsa_kernel(q_ref, kc_ref, vc_ref, ks_ref, vs_ref, kw_ref, vw_ref, gate_ref, gb_ref, ov_ref, o_ref, *,
                tq, q_off, nc_real, ns_real, tk, n_kv_static, w_off, w_len, w_follows_q, precise, stack_heads):
    g = pl.program_id(1)
    qi = pl.program_id(2)
    q0 = qi * tq
    ncp = kc_ref.shape[0]
    nsp = ov_ref.shape[1]
    qpos_col = q_off + q0 + lax.broadcasted_iota(jnp.int32, (tq, 1), 0)
    q_all = q_ref[...]
    q_bf = [_parts(q_all[:, r * HEAD_DIM:(r + 1) * HEAD_DIM], precise) for r in range(GQA_R)]
    head_f = (g * GQA_R).astype(F32)
    slopes = [jnp.exp2(-0.5 * (jnp.full((1, 1), 1.0 + r, F32) + head_f)) for r in range(GQA_R)]

    n_idx = lax.broadcasted_iota(jnp.int32, (tq, ncp), 1)
    dist_c = qpos_col - (n_idx * CMP_STRIDE + (CMP_BLOCK - 1))
    ok_c = (dist_c >= 0) & (n_idx < nc_real)
    dist_cf = dist_c.astype(F32)
    kc_bf = _parts(kc_ref[...], precise)
    vc_bf = _parts(vc_ref[...], precise)
    o_c = []
    p_sum = None
    for r in range(GQA_R):
        s = _pdot(q_bf[r], kc_bf, nt=True) * SCALE - slopes[r] * dist_cf
        s = jnp.where(ok_c, s, NEG)
        m = jnp.max(s, axis=-1, keepdims=True)
        e = jnp.where(ok_c, jnp.exp(s - m), 0.0)
        p = e / jnp.maximum(jnp.sum(e, axis=-1, keepdims=True), 1e-30)
        o_c.append(_pdot(_parts(p, precise), vc_bf))
        p_sum = p if p_sum is None else p_sum + p
    imp = _dot2(p_sum, ov_ref[...])
    s_idx = lax.broadcasted_iota(jnp.int32, (tq, nsp), 1)
    cur = lax.shift_right_logical(qpos_col, int(math.log2(SLC_BLOCK)))
    forced = (s_idx == 0) | (s_idx == cur) | (s_idx == cur - 1)
    causal_blk = (s_idx * SLC_BLOCK <= qpos_col) & (s_idx < ns_real)
    score = jnp.where(causal_blk, imp + jnp.where(forced, BIG, 0.0), NEG)
    n_acc = 4
    ranks = [jnp.zeros((tq, nsp), jnp.int32) for _ in range(n_acc)]
    for j in range(ns_real):
        col = score[:, j:j + 1]
        ahead = (col > score) | ((col == score) & (s_idx > j))
        ranks[j % n_acc] = ranks[j % n_acc] + jnp.where(ahead, 1, 0)
    rank = (ranks[0] + ranks[1]) + (ranks[2] + ranks[3])
    sel = (rank < min(TOP_N, ns_real)) & (score > 0.5 * NEG)
    sel_bf = jnp.where(sel, 1.0, 0.0).astype(BF16)

    cslopes = [sl * (1.0 / SCALE) for sl in slopes]
    if stack_heads:
        qg = [_parts(jnp.concatenate([q_all[:, r * HEAD_DIM:(r + 1) * HEAD_DIM] for r in range(GQA_R)], axis=0),
                     precise)]
        csg = [jnp.concatenate([jnp.broadcast_to(c, (tq, 1)) for c in cslopes], axis=0)]
        rep = lambda x: jnp.concatenate([x] * GQA_R, axis=0)
        rows = GQA_R * tq
        head_rows = lambda st, r: tuple(a[r * tq:(r + 1) * tq] for a in st[0])
    else:
        qg, csg, rep, rows = q_bf, cslopes, (lambda x: x), tq
        head_rows = lambda st, r: st[r]

    def init():
        return tuple((jnp.full((rows, 1), NEG, F32), jnp.zeros((rows, 1), F32), jnp.zeros((rows, HEAD_DIM), F32))
                     for _ in qg)

    blk_per_tile = tk // SLC_BLOCK
    ex_row = lax.broadcasted_iota(jnp.int32, (nsp, tk), 0)
    ex_col = lax.shift_right_logical(lax.broadcasted_iota(jnp.int32, (nsp, tk), 1), int(math.log2(SLC_BLOCK)))
    key_i = lax.broadcasted_iota(jnp.int32, (tq, tk), 1)

    def slc_body(kt, carry):
        base = pl.multiple_of(kt * tk, tk)
        expand = jnp.where(ex_row == ex_col + kt * blk_per_tile, 1.0, 0.0).astype(BF16)
        selexp = _dot(sel_bf, expand)
        dist = qpos_col - (key_i + base)
        distm = rep(jnp.where(selexp > 0.5, jnp.where(dist >= 0, dist.astype(F32), MASKED_DIST), MASKED_DIST))
        k_p = _parts(ks_ref[pl.ds(base, tk), :], precise)
        v_p = _parts(vs_ref[pl.ds(base, tk), :], precise)
        return tuple(_online_softmax_step(qg[i], k_p, v_p, csg[i] * distm, *carry[i], precise)
                     for i in range(len(qg)))

    if n_kv_static is None:
        n_kv = (q0 + tq + tk - 1) // tk
    else:
        n_kv = n_kv_static
    st_s = lax.fori_loop(0, n_kv, slc_body, init())

    if w_follows_q:
        w_base = pl.multiple_of(jnp.maximum(q0 + tq - w_len, 0), SUBLANES)
    else:
        w_base = 0
    dist_w = qpos_col - (w_off + w_base + lax.broadcasted_iota(jnp.int32, (tq, w_len), 1))
    ok_w = jnp.where(dist_w >= 0, dist_w, WINDOW) < WINDOW
    dist_wm = rep(jnp.where(ok_w, dist_w.astype(F32), MASKED_DIST))
    kw_p = _parts(kw_ref[pl.ds(w_base, w_len), :], precise)
    vw_p = _parts(vw_ref[pl.ds(w_base, w_len), :], precise)
    st_w = tuple(_online_softmax_step(qg[i], kw_p, vw_p, csg[i] * dist_wm, *init()[i], precise)
                 for i in range(len(qg)))

    gates = _sigmoid(gate_ref[...] + gb_ref[...])
    lane = lax.broadcasted_iota(jnp.int32, gates.shape, 1)
    for r in range(GQA_R):
        head = g * GQA_R + r
        gcol = lambda br: jnp.sum(jnp.where(lane == br * N_HEADS + head, gates, 0.0), axis=-1, keepdims=True)
        _, l_s, a_s = head_rows(st_s, r)
        _, l_w, a_w = head_rows(st_w, r)
        o_s = a_s / jnp.maximum(l_s, 1e-30)
        o_w = a_w / jnp.maximum(l_w, 1e-30)
        o_ref[:, r * HEAD_DIM:(r + 1) * HEAD_DIM] = gcol(0) * o_c[r] + gcol(1) * o_s + gcol(2) * o_w


def nsa_attention(q2d, kcvc, kv2d, kv_col0, win2d, win_col0, gates, gate_b, overlap, *, B, t_q, t_kv, t_win, tq,
                  q_off, nc_real, ns_real, tk, n_kv_static, w_off, w_len, w_follows_q, precise, stack_heads):
    nq = t_q // tq
    ncp = kcvc.shape[3]
    nsp = overlap.shape[1]
    qw = GQA_R * HEAD_DIM
    slab = lambda arr_t, col: pl.BlockSpec((arr_t, HEAD_DIM), col)
    kern = functools.partial(_nsa_kernel, tq=tq, q_off=q_off, nc_real=nc_real, ns_real=ns_real, tk=tk,
                             n_kv_static=n_kv_static, w_off=w_off, w_len=w_len, w_follows_q=w_follows_q,
                             precise=precise, stack_heads=stack_heads)
    return pl.pallas_call(
        kern,
        grid=(B, N_KV, nq),
        in_specs=[pl.BlockSpec((tq, qw), lambda b, g, i: (b * nq + i, g)),
                  pl.BlockSpec((None, None, None, ncp, HEAD_DIM), lambda b, g, i: (b, 0, g, 0, 0)),
                  pl.BlockSpec((None, None, None, ncp, HEAD_DIM), lambda b, g, i: (b, 1, g, 0, 0)),
                  slab(t_kv, lambda b, g, i: (b, kv_col0 + 2 * N_KV + g)),
                  slab(t_kv, lambda b, g, i: (b, kv_col0 + 3 * N_KV + g)),
                  slab(t_win, lambda b, g, i: (b, win_col0 + g)),
                  slab(t_win, lambda b, g, i: (b, win_col0 + N_KV + g)),
                  pl.BlockSpec((tq, LANES), lambda b, g, i: (b * nq + i, 0)),
                  pl.BlockSpec((1, LANES), lambda b, g, i: (0, 0)),
                  pl.BlockSpec((ncp, nsp), lambda b, g, i: (0, 0))],
        out_specs=pl.BlockSpec((tq, qw), lambda b, g, i: (b * nq + i, g)),
        out_shape=jax.ShapeDtypeStruct((B * t_q, N_HEADS * HEAD_DIM), F32),
        compiler_params=_cparams("parallel", "parallel", "arbitrary"),
        name="nsa_attention",
    )(q2d, kcvc, kcvc, kv2d, kv2d, win2d, win2d, gates, gate_b, overlap)


def _overlap_matrix(ncp, nsp, nc_real, ns_real):
    n = jnp.arange(ncp)[:, None]
    s = jnp.arange(nsp)[None, :]
    c0 = n * CMP_STRIDE
    s0 = s * SLC_BLOCK
    ov = (c0 < s0 + SLC_BLOCK) & (c0 + CMP_BLOCK > s0) & (n < nc_real) & (s < ns_real)
    return ov.astype(BF16)


def _gather_pages_kernel(pt_ref, page_ref, tail_ref, o_ref):
    j = pl.program_id(1)
    n_pages = pl.num_programs(1) - 1

    @pl.when(j < n_pages)
    def _():
        o_ref[...] = page_ref[...]

    @pl.when(j == n_pages)
    def _():
        o_ref[...] = tail_ref[...]


def gather_pages(cache5, page_table, tail, li):
    B, n_pages = page_table.shape
    page, row_w = cache5.shape[1], cache5.shape[4]
    grid_spec = pltpu.PrefetchScalarGridSpec(
        num_scalar_prefetch=1,
        grid=(B, n_pages + 1),
        in_specs=[pl.BlockSpec((None, page, None, None, row_w),
                               lambda b, j, pt: (pt[b, jnp.minimum(j, n_pages - 1)], 0, li, 0, 0)),
                  pl.BlockSpec((None, page, row_w), lambda b, j, pt: (b, 0, 0))],
        out_specs=pl.BlockSpec((None, page, row_w), lambda b, j, pt: (b, j, 0)),
    )
    return pl.pallas_call(
        _gather_pages_kernel,
        grid_spec=grid_spec,
        out_shape=jax.ShapeDtypeStruct((B, (n_pages + 1) * page, row_w), F32),
        compiler_params=_cparams("parallel", "arbitrary"),
        name="gather_pages",
    )(page_table, cache5, tail)


def attn_params(li, attn_w_in, attn_gate_b, w_cmp_k, w_cmp_v, attn_w_out):
    qkv_w = N_HEADS * HEAD_DIM + 6 * N_KV * HEAD_DIM
    ng = 3 * N_HEADS
    half = lambda w: jnp.concatenate([w[:CMP_STRIDE], w[CMP_STRIDE:]], axis=-1)
    return dict(
        w_main=attn_w_in[li][:, :qkv_w],
        w_gate=jnp.pad(attn_w_in[li][:, qkv_w:], ((0, 0), (0, LANES - ng))),
        gate_b=jnp.pad(attn_gate_b[li], (0, LANES - ng)).reshape(1, LANES),
        wcat=jnp.stack([half(w_cmp_k[li]), half(w_cmp_v[li])]),
        w_out=attn_w_out[li],
    )


def attn_prompt(x2d, B, T, prm):
    p = mm(x2d, prm["w_main"])
    gates = mm(x2d, prm["w_gate"])
    q_blocks = N_HEADS
    n_pieces = T // CMP_STRIDE
    nc_real = n_pieces - CMP_BLOCK // CMP_STRIDE + 1
    ns_real = T // SLC_BLOCK
    ncp = -(-n_pieces // LANES) * LANES
    nsp = -(-ns_real // LANES) * LANES
    kcvc = nsa_compress(p, q_blocks, B, T, prm["wcat"], ncp, False)
    ov = _overlap_matrix(ncp, nsp, nc_real, ns_real)
    tq = min(128, T)
    o = nsa_attention(p, kcvc, p, q_blocks, p, q_blocks + 4 * N_KV, gates, prm["gate_b"], ov,
                      B=B, t_q=T, t_kv=T, t_win=T, tq=tq, q_off=0, nc_real=nc_real, ns_real=ns_real,
                      tk=min(256, T), n_kv_static=None, w_off=0, w_len=min(WINDOW + tq, T), w_follows_q=True,
                      precise=False, stack_heads=False)
    y = mm(o, prm["w_out"])
    kv = p.reshape(B, T, -1)[:, :, N_HEADS * HEAD_DIM:].reshape(B, T, 6, N_KV, HEAD_DIM)
    w_buf = min(WINDOW, T)
    win = jnp.pad(kv[:, :, 4:], ((0, 0), (WINDOW, 0), (0, 0), (0, 0), (0, 0)))[:, -w_buf:]
    return y, kv[:, :, :4], win


def attn_sample(x2d, B, T, prm, cache_nsa_kv, cache_win, page_table, li):
    n_pool, page = cache_nsa_kv.shape[:2]
    n_attn = cache_nsa_kv.shape[2]
    n_pages = page_table.shape[1]
    past_len = n_pages * page
    row_w = 4 * N_KV * HEAD_DIM
    p = mm(x2d, prm["w_main"], True)
    gates = mm(x2d, prm["w_gate"], True)
    kv_new = p[:, N_HEADS * HEAD_DIM:].reshape(B, T, 6, N_KV, HEAD_DIM)
    tail = jnp.pad(kv_new[:, :, :4].reshape(B, T, row_w), ((0, 0), (0, page - T), (0, 0)))
    full = gather_pages(cache_nsa_kv.reshape(n_pool, page, n_attn, 1, row_w), page_table, tail, li)
    t_kv = (n_pages + 1) * page
    full2d = full.reshape(B * t_kv, row_w)
    L = past_len + T
    l_pad = -(-L // SLC_BLOCK) * SLC_BLOCK
    nc_real = l_pad // CMP_STRIDE - CMP_BLOCK // CMP_STRIDE + 1
    ns_real = l_pad // SLC_BLOCK
    n_pieces = t_kv // CMP_STRIDE
    ncp = -(-n_pieces // LANES) * LANES
    nsp = -(-ns_real // LANES) * LANES
    kcvc = nsa_compress(full2d, 0, B, t_kv, prm["wcat"], ncp, True)
    ov = _overlap_matrix(ncp, nsp, nc_real, ns_real)
    w_buf = cache_win.shape[1]
    win = jnp.concatenate([cache_win, kv_new[:, :, 4:]], axis=1)
    t_win = -(-(w_buf + T) // LANES) * LANES
    win2d = jnp.pad(win.reshape(B, w_buf + T, 2 * N_KV * HEAD_DIM), ((0, 0), (0, t_win - (w_buf + T)), (0, 0)))
    win2d = win2d.reshape(B * t_win, 2 * N_KV * HEAD_DIM)
    n_kv = 3
    assert t_kv % (n_kv * LANES) == 0
    o = nsa_attention(p, kcvc, full2d, 0, win2d, 0, gates, prm["gate_b"], ov,
                      B=B, t_q=T, t_kv=t_kv, t_win=t_win, tq=T, q_off=past_len, nc_real=nc_real, ns_real=ns_real,
                      tk=t_kv // n_kv, n_kv_static=n_kv, w_off=past_len - w_buf, w_len=t_win, w_follows_q=False,
                      precise=True, stack_heads=True)
    y = mm(o, prm["w_out"], True)
    return y, kv_new[:, :, :4], win[:, -w_buf:]


MOE_TM = 256


ROUTER_TN = 256


def _first_argmax(vals):
    best, idx = vals[0], jnp.zeros(vals[0].shape, jnp.int32)
    for i in range(1, len(vals)):
        better = vals[i] > best
        best = jnp.where(better, vals[i], best)
        idx = jnp.where(better, i, idx)
    return best, idx


def _pick_by(idx, vals):
    out = vals[-1]
    for i in range(len(vals) - 2, -1, -1):
        out = jnp.where(idx == i, vals[i], out)
    return out


def _router_kernel(x_ref, wt_ref, bias_ref, e_ref, w_ref, rank_ref, cnt_ref, base, *, n_tok):
    i = pl.program_id(0)
    tn = x_ref.shape[0]

    @pl.when(i == 0)
    def _():
        base[...] = jnp.zeros(base.shape, F32)

    xh, xl = _split2(x_ref[...])
    wh, wl = _split2(wt_ref[...])
    logits = _dot_nt(wh, xh) + _dot_nt(wh, xl) + _dot_nt(wl, xh)
    aff = _sigmoid(logits)
    sel = aff + bias_ref[...]
    s = [sel[e:e + 1, :] for e in range(N_EXPERTS)]
    a = [aff[e:e + 1, :] for e in range(N_EXPERTS)]
    G = EXPERTS_PER_GROUP
    gscore = []
    for g in range(N_GROUPS):
        v0, v1, v2, v3 = s[G * g:G * g + G]
        hi1, lo1 = jnp.maximum(v0, v1), jnp.minimum(v0, v1)
        hi2, lo2 = jnp.maximum(v2, v3), jnp.minimum(v2, v3)
        gscore.append(jnp.maximum(hi1, hi2) + jnp.maximum(jnp.minimum(hi1, hi2), jnp.maximum(lo1, lo2)))
    _, gi = _first_argmax(gscore)
    sg = [_pick_by(gi, [s[G * g + j] for g in range(N_GROUPS)]) for j in range(G)]
    ag = [_pick_by(gi, [a[G * g + j] for g in range(N_GROUPS)]) for j in range(G)]
    _, l0 = _first_argmax(sg)
    _, l1 = _first_argmax([jnp.where(l0 == j, -jnp.inf, sg[j]) for j in range(G)])
    w0, w1 = _pick_by(l0, ag), _pick_by(l1, ag)
    wsum = w0 + w1
    e0, e1 = gi * G + l0, gi * G + l1
    e_ref[...] = jnp.concatenate([e0, e1], axis=0)
    w_ref[...] = jnp.concatenate([w0 / wsum, w1 / wsum], axis=0)
    tok = i * tn + lax.broadcasted_iota(jnp.int32, (N_EXPERTS, tn), 1)
    eid = lax.broadcasted_iota(jnp.int32, (N_EXPERTS, tn), 0)
    valid = tok < n_tok
    oh0 = jnp.where((eid == e0) & valid, 1.0, 0.0)
    oh1 = jnp.where((eid == e1) & valid, 1.0, 0.0)
    cnt = oh0 + oh1
    r_i = lax.broadcasted_iota(jnp.int32, (tn, tn), 0)
    c_i = lax.broadcasted_iota(jnp.int32, (tn, tn), 1)
    upper = jnp.where(r_i <= c_i, 1.0, 0.0).astype(BF16)
    excl = _dot(cnt.astype(BF16), upper) - cnt + base[...]
    rank_ref[...] = jnp.concatenate([jnp.sum(oh0 * excl, axis=0, keepdims=True),
                                     jnp.sum(oh1 * excl, axis=0, keepdims=True)], axis=0).astype(jnp.int32)
    base[...] = base[...] + jnp.sum(cnt, axis=1, keepdims=True)
    cnt_ref[...] = jnp.broadcast_to(base[...], cnt_ref.shape).astype(jnp.int32)


def router(x, w_t, bias_col):
    n, d = x.shape
    tn = ROUTER_TN if n >= ROUTER_TN else LANES
    nt = -(-n // tn)
    n_pad = nt * tn
    if n < tn:
        x = jnp.pad(x, ((0, tn - n), (0, 0)))
    two = lambda dt: jax.ShapeDtypeStruct((TOP_K, n_pad), dt)
    out_row = pl.BlockSpec((TOP_K, tn), lambda i: (0, i))
    e, w, rank, cnt = pl.pallas_call(
        functools.partial(_router_kernel, n_tok=n),
        grid=(nt,),
        in_specs=[pl.BlockSpec((tn, d), lambda i: (i, 0)),
                  pl.BlockSpec((N_EXPERTS, d), lambda i: (0, 0)),
                  pl.BlockSpec((N_EXPERTS, 1), lambda i: (0, 0))],
        out_specs=[out_row, out_row, out_row, pl.BlockSpec((N_EXPERTS, LANES), lambda i: (0, 0))],
        out_shape=[two(jnp.int32), two(F32), two(jnp.int32), jax.ShapeDtypeStruct((N_EXPERTS, LANES), jnp.int32)],
        scratch_shapes=[pltpu.VMEM((N_EXPERTS, 1), F32)],
        compiler_params=_cparams("arbitrary"),
        name="router",
    )(x, w_t, bias_col)
    return e[:, :n], w[:, :n], rank[:, :n], cnt[:, 0]


def _expert_kernel(te_ref, x_ref, rw_ref, wg_ref, wu_ref, wd_ref, o_ref):
    xb = x_ref[...].astype(BF16)
    hg = _dot(xb, wg_ref[...])
    hu = _dot(xb, wu_ref[...])
    h = (hg * _sigmoid(hg)) * hu * rw_ref[...]
    o_ref[...] = _dot(h.astype(BF16), wd_ref[...])


def expert_ffn(x_sorted, row_w, tile_expert, wg, wu, wd, layer):
    a_pad, d = x_sorted.shape
    f = wg.shape[3]
    n_tiles = a_pad // MOE_TM
    grid_spec = pltpu.PrefetchScalarGridSpec(
        num_scalar_prefetch=1,
        grid=(n_tiles,),
        in_specs=[pl.BlockSpec((MOE_TM, d), lambda t, te: (t, 0)),
                  pl.BlockSpec((MOE_TM, 1), lambda t, te: (t, 0)),
                  pl.BlockSpec((None, None, d, f), lambda t, te: (layer, te[t], 0, 0)),
                  pl.BlockSpec((None, None, d, f), lambda t, te: (layer, te[t], 0, 0)),
                  pl.BlockSpec((None, None, f, d), lambda t, te: (layer, te[t], 0, 0))],
        out_specs=pl.BlockSpec((MOE_TM, d), lambda t, te: (t, 0)),
    )
    return pl.pallas_call(
        _expert_kernel,
        grid_spec=grid_spec,
        out_shape=jax.ShapeDtypeStruct((a_pad, d), F32),
        compiler_params=_cparams("arbitrary"),
        name="expert_ffn",
    )(tile_expert, x_sorted, row_w, wg, wu, wd)


MOE_DENSE_FT = 512


def _moe_dense_kernel(x_ref, gate_ref, wg_ref, wu_ref, wd_ref, o_ref):
    e = pl.program_id(0)

    @pl.when((e == 0) & (pl.program_id(1) == 0))
    def _():
        o_ref[...] = jnp.zeros(o_ref.shape, F32)

    x = x_ref[...]
    hg = _dot3(x, wg_ref[...])
    hu = _dot3(x, wu_ref[...])
    gate = gate_ref[...]
    lane = lax.broadcasted_iota(jnp.int32, gate.shape, 1)
    gcol = jnp.sum(jnp.where(lane == e, gate, 0.0), axis=-1, keepdims=True)
    o_ref[...] += _dot3((hg * _sigmoid(hg)) * hu * gcol, wd_ref[...])


def moe_dense_precise(x, router_wt, r_bias_col, w_g, w_u, w_d, layer):
    n, d = x.shape
    f = w_g.shape[3]
    ft = MOE_DENSE_FT
    e, w_sel, _, _ = router(x, router_wt, r_bias_col)
    lanes = jnp.arange(LANES, dtype=jnp.int32)[None, :]
    gate = sum(jnp.where(e[k][:, None] == lanes, w_sel[k][:, None], 0.0) for k in range(TOP_K))
    return pl.pallas_call(
        _moe_dense_kernel,
        grid=(N_EXPERTS, f // ft),
        in_specs=[pl.BlockSpec((n, d), lambda e, j: (0, 0)),
                  pl.BlockSpec((n, LANES), lambda e, j: (0, 0)),
                  pl.BlockSpec((None, None, d, ft), lambda e, j: (layer, e, 0, j)),
                  pl.BlockSpec((None, None, d, ft), lambda e, j: (layer, e, 0, j)),
                  pl.BlockSpec((None, None, ft, d), lambda e, j: (layer, e, j, 0))],
        out_specs=pl.BlockSpec((n, d), lambda e, j: (0, 0)),
        out_shape=jax.ShapeDtypeStruct((n, d), F32),
        compiler_params=_cparams("arbitrary", "arbitrary"),
        name="moe_dense_precise",
    )(x, gate, w_g, w_u, w_d)


def moe(x, router_wt, r_bias_col, w_g, w_u, w_d, layer):
    n, d = x.shape
    e, w_sel, rank, counts = router(x, router_wt, r_bias_col)
    n_asg = n * TOP_K
    gsz = (counts + MOE_TM - 1) // MOE_TM * MOE_TM
    p_end = jnp.cumsum(gsz)
    p_start = p_end - gsz
    onehot = (e[:, :, None] == jnp.arange(N_EXPERTS, dtype=jnp.int32)[None, None, :]).astype(jnp.int32)
    dest = jnp.sum(onehot * p_start[None, None, :], axis=2) + rank
    a_pad = -(-(n_asg + N_EXPERTS * (MOE_TM - 1)) // MOE_TM) * MOE_TM
    tok = jnp.broadcast_to(jnp.arange(n, dtype=jnp.int32)[None, :], (TOP_K, n))
    src_tok = jnp.zeros((a_pad,), jnp.int32).at[dest.reshape(n_asg)].set(tok.reshape(n_asg))
    row_w = jnp.zeros((a_pad,), F32).at[dest.reshape(n_asg)].set(w_sel.reshape(n_asg))
    tile_start = jnp.arange(a_pad // MOE_TM, dtype=jnp.int32) * MOE_TM
    tile_expert = jnp.minimum(jnp.sum((p_end[None, :] <= tile_start[:, None]).astype(jnp.int32), axis=1),
                              N_EXPERTS - 1).astype(jnp.int32)
    ys = expert_ffn(x.astype(BF16)[src_tok], row_w.reshape(a_pad, 1), tile_expert, w_g, w_u, w_d, layer)
    return ys[dest[0]] + ys[dest[1]]


def kernel(x_prompt, x_sample, cache_nsa_kv, cache_win_kv, state_rwkv, state_rwkv_shift, state_lru_h, state_lru_conv,
           page_table, ln1_g, ln1_b, ln2_g, ln2_b, rec_w_in, rec_mu, rwkv_w0, rwkv_w_up, rwkv_a0, rwkv_a_up, rwkv_g_up,
           rwkv_k_k, rwkv_k_a, rwkv_r_k, rwkv_gn_g, rwkv_gn_b, lru_conv_w, lru_conv_b, lru_wa, lru_ba, lru_wx, lru_bx,
           lru_lambda, rec_w_out, attn_w_in, attn_gate_b, w_cmp_k, w_cmp_v, attn_w_out, moe_w_router, moe_router_bias,
           moe_w_gate, moe_w_up, moe_w_down):
    Bp, Tp, D = x_prompt.shape
    Bs, Ts, _ = x_sample.shape
    n_p, n_s = Bp * Tp, Bs * Ts
    rec_args = (rec_w_in, rec_mu, rwkv_w0, rwkv_w_up, rwkv_a0, rwkv_a_up, rwkv_g_up, rwkv_k_k, rwkv_k_a,
                rwkv_r_k, rwkv_gn_g, rwkv_gn_b, lru_conv_w, lru_conv_b, lru_wa, lru_ba, lru_wx, lru_bx, lru_lambda,
                rec_w_out)
    xp, xs = x_prompt.reshape(n_p, D), x_sample.reshape(n_s, D)
    router_w = moe_w_router.T
    moe_router_bias = moe_router_bias.reshape(N_EXPERTS, 1)
    wg_bf, wu_bf, wd_bf = moe_w_gate.astype(BF16), moe_w_up.astype(BF16), moe_w_down.astype(BF16)
    nsa_p, nsa_s, win_p, win_s = [], [], [], []
    rs_p, rs_s, rsh_p, rsh_s, lh_p, lh_s, lc_p, lc_s = [], [], [], [], [], [], [], []
    for layer in range(DEPTH):
        li = layer // 2
        if layer % 2 == 0:
            prm = rec_params(li, *rec_args)
            yp, S, sh, h, cb = rec_mixer(xp, Bp, Tp, jnp.zeros((Bp, RWKV_HEADS, RWKV_HD, RWKV_HD), F32),
                                         jnp.zeros((Bp, RWKV_SHIFT_W), F32), jnp.zeros((Bp, LRU_W), F32),
                                         jnp.zeros((Bp, CONV_W - 1, LRU_W), F32), prm)
            ys, S2, sh2, h2, cb2 = rec_mixer(xs, Bs, Ts, state_rwkv[:, li], state_rwkv_shift[:, li],
                                             state_lru_h[:, li], state_lru_conv[:, li], prm, precise=True)
            rs_p.append(S); rs_s.append(S2); rsh_p.append(sh); rsh_s.append(sh2)
            lh_p.append(h); lh_s.append(h2); lc_p.append(cb); lc_s.append(cb2)
        else:
            prm = attn_params(li, attn_w_in, attn_gate_b, w_cmp_k, w_cmp_v, attn_w_out)
            yp, rows, wb = attn_prompt(xp, Bp, Tp, prm)
            ys, rows2, wb2 = attn_sample(xs, Bs, Ts, prm, cache_nsa_kv, cache_win_kv[:, :, li], page_table, li)
            nsa_p.append(rows); nsa_s.append(rows2); win_p.append(wb); win_s.append(wb2)
        xp = ln_res(xp, yp, ln1_g[layer], ln1_b[layer])
        xs = ln_res(xs, ys, ln1_g[layer], ln1_b[layer])
        xp = ln_res(xp, moe(xp, router_w, moe_router_bias, wg_bf, wu_bf, wd_bf, layer), ln2_g[layer], ln2_b[layer])
        xs = ln_res(xs, moe_dense_precise(xs, router_w, moe_router_bias, moe_w_gate, moe_w_up, moe_w_down, layer),
                    ln2_g[layer], ln2_b[layer])
    return (xp.reshape(Bp, Tp, D), xs.reshape(Bs, Ts, D),
            jnp.stack(nsa_p, axis=2), jnp.stack(nsa_s, axis=2), jnp.stack(win_p, axis=2), jnp.stack(win_s, axis=2),
            jnp.stack(rs_p, axis=1), jnp.stack(rs_s, axis=1), jnp.stack(rsh_p, axis=1), jnp.stack(rsh_s, axis=1),
            jnp.stack(lh_p, axis=1), jnp.stack(lh_s, axis=1), jnp.stack(lc_p, axis=1), jnp.stack(lc_s, axis=1))
```

```python
import functools
import math

import jax
import jax.numpy as jnp
from jax import lax
from jax.experimental import pallas as pl
from jax.experimental.pallas import tpu as pltpu

F32 = jnp.float32
BF16 = jnp.bfloat16

D_MODEL = 2048
DEPTH = 4
RWKV_HEADS = 16
RWKV_HD = 64
RWKV_W = RWKV_HEADS * RWKV_HD
LORA_DECAY = 64
LORA_A = 64
LORA_GATE = 128
RWKV_SHIFT_W = 3 * RWKV_W + LORA_DECAY + LORA_A + LORA_GATE
GN_EPS = 64e-5
LRU_W = D_MODEL - RWKV_W
LRU_BLOCKS = 16
CONV_W = 4
LRU_C = 8.0
N_HEADS = 16
HEAD_DIM = 128
N_KV = 4
GQA_R = N_HEADS // N_KV
CMP_BLOCK = 32
CMP_STRIDE = 16
SLC_BLOCK = 64
TOP_N = 16
WINDOW = 512
SCALE = HEAD_DIM ** -0.5
N_EXPERTS = 16
N_GROUPS = 4
EXPERTS_PER_GROUP = N_EXPERTS // N_GROUPS
TOP_K = 2
D_EXPERT = 1024
LN_EPS = 1e-5
DN_ALPHA = (2 * DEPTH) ** 0.25
NEG = -1e30
BIG = 1e4

LANES = 128
SUBLANES = 8
VMEM_LIMIT_BYTES = 48 * 1024 * 1024

RWKV_PAD_W = 3584
PAIRS = RWKV_HEADS // 2


def _cparams(*sem):
    return pltpu.CompilerParams(dimension_semantics=sem, vmem_limit_bytes=VMEM_LIMIT_BYTES)


def _split2(x):
    hi = x.astype(BF16)
    lo = (x - hi.astype(F32)).astype(BF16)
    return hi, lo


def _dot(a, b):
    return jnp.dot(a, b, preferred_element_type=F32)


def _dot2(x, w_bf):
    hi, lo = _split2(x)
    return _dot(hi, w_bf) + _dot(lo, w_bf)


def _sigmoid(x):
    return 1.0 / (1.0 + jnp.exp(-x))


def _softplus(x):
    return jnp.maximum(x, 0.0) + jnp.log(1.0 + jnp.exp(-jnp.abs(x)))


def _mdot(a, b, precise):
    return _dot3(a, b) if precise else _dot(a.astype(BF16), b.astype(BF16))


def _mdot_nt(a, b, precise):
    return _dot3_nt(a, b) if precise else _dot_nt(a.astype(BF16), b.astype(BF16))


def _mm_kernel(x_ref, w_ref, o_ref, *, precise):
    o_ref[...] = _mdot(x_ref[...], w_ref[...], precise)


def _pick(n, cands):
    for c in cands:
        if n % c == 0:
            return c
    return n


def mm(x, w, precise=False):
    m, k = x.shape
    n = w.shape[1]
    tm = _pick(m, (512, 256, 128, 64, 32, 16, 8))
    tn = _pick(n, (512, 256, 128) if precise else (1024, 768, 512, 256, 128))
    return pl.pallas_call(
        functools.partial(_mm_kernel, precise=precise),
        grid=(n // tn, m // tm),
        in_specs=[pl.BlockSpec((tm, k), lambda j, i: (i, 0)),
                  pl.BlockSpec((k, tn), lambda j, i: (0, j))],
        out_specs=pl.BlockSpec((tm, tn), lambda j, i: (i, j)),
        out_shape=jax.ShapeDtypeStruct((m, n), F32),
        compiler_params=_cparams("parallel", "parallel"),
        name="mm",
    )(x, w)


def _mm2_kernel(x1_ref, x2_ref, w1_ref, w2_ref, o_ref, *, precise):
    o_ref[...] = _mdot(x1_ref[...], w1_ref[...], precise) + _mdot(x2_ref[...], w2_ref[...], precise)


def mm_cat2(x1, x2, w, precise=False):
    m, k1 = x1.shape
    k2 = x2.shape[1]
    assert k1 == k2 and w.shape[0] == k1 + k2
    n = w.shape[1]
    tm = _pick(m, (512, 256, 128, 64, 32, 16, 8))
    tn = _pick(n, (512, 256, 128) if precise else (1024, 512, 256, 128))
    return pl.pallas_call(
        functools.partial(_mm2_kernel, precise=precise),
        grid=(n // tn, m // tm),
        in_specs=[pl.BlockSpec((tm, k1), lambda j, i: (i, 0)),
                  pl.BlockSpec((tm, k2), lambda j, i: (i, 0)),
                  pl.BlockSpec((k1, tn), lambda j, i: (0, j)),
                  pl.BlockSpec((k2, tn), lambda j, i: (1, j))],
        out_specs=pl.BlockSpec((tm, tn), lambda j, i: (i, j)),
        out_shape=jax.ShapeDtypeStruct((m, n), F32),
        compiler_params=_cparams("parallel", "parallel"),
        name="mm_cat2",
    )(x1, x2, w, w)


def _ln_res_kernel(x_ref, y_ref, g_ref, b_ref, o_ref):
    z = DN_ALPHA * x_ref[...] + y_ref[...]
    mu = jnp.mean(z, axis=-1, keepdims=True)
    zc = z - mu
    var = jnp.mean(zc * zc, axis=-1, keepdims=True)
    o_ref[...] = zc * lax.rsqrt(var + LN_EPS) * g_ref[...] + b_ref[...]


def ln_res(x, y, g, b):
    m, d = x.shape
    tm = _pick(m, (256, 192, 128, 64, 32, 16, 8))
    row = pl.BlockSpec((tm, d), lambda i: (i, 0))
    vec = pl.BlockSpec((1, d), lambda i: (0, 0))
    return pl.pallas_call(
        _ln_res_kernel,
        grid=(m // tm,),
        in_specs=[row, row, vec, vec],
        out_specs=row,
        out_shape=jax.ShapeDtypeStruct((m, d), F32),
        compiler_params=_cparams("parallel"),
        name="ln_res",
    )(x, y, g.reshape(1, d), b.reshape(1, d))


def _shifted(carry_ref, x, j):
    ext = jnp.concatenate([carry_ref[...], x], axis=0)
    return pltpu.roll(ext, j, axis=0)[SUBLANES:, :]


def _rwkv_prep_kernel(p_ref, sh0_ref, mu_ref, w0_ref, a0_ref, wlora_ref, alora_ref, gup_ref,
                      kk_ref, ka_ref, rk_ref, bd_ref, *refs, chunked, precise):
    outs, carry = refs[:-1], refs[-1]
    ti = pl.program_id(1)

    @pl.when(ti == 0)
    def _():
        carry[...] = jnp.broadcast_to(sh0_ref[0], carry.shape)

    p = p_ref[...]
    tm = p.shape[0]
    p_prev = _shifted(carry, p, 1)
    carry[...] = p[tm - SUBLANES:, :]
    pm = p + mu_ref[...] * (p_prev - p)
    W = RWKV_W
    r = pm[:, 0:W]
    k = pm[:, W:2 * W]
    v = pm[:, 2 * W:3 * W]
    lo = pm[:, 3 * W:3 * W + LORA_DECAY + LORA_A]
    gd = pm[:, 3 * W + LORA_DECAY + LORA_A:RWKV_SHIFT_W]
    lane = lax.broadcasted_iota(jnp.int32, lo.shape, 1)
    z = jnp.where(lane < LORA_DECAY, jnp.tanh(lo), lo)
    w_l = _mdot(z, wlora_ref[...], precise)
    a_l = _mdot(z, alora_ref[...], precise)
    w = -_softplus(-(w0_ref[...] + w_l)) - 0.5
    log_decay = -jnp.exp(w)
    decay = jnp.exp(log_decay)
    a = _sigmoid(a0_ref[...] + a_l)
    g = _mdot(_sigmoid(gd), gup_ref[...], precise)
    bd = bd_ref[...]
    kk = k * kk_ref[...]
    ss = _dot2(kk * kk, bd)
    kn = kk * lax.rsqrt(jnp.maximum(ss, 1e-24))
    k_eff = k * (1.0 + (a - 1.0) * ka_ref[...])
    bvec = kn * a
    bonus = _dot2(r * k_eff * rk_ref[...], bd) * v
    if chunked:
        vals = (r, log_decay, k_eff, kn, bvec, v, g, bonus)
    else:
        vals = (r, decay, k_eff, kn, bvec, v, decay * r, _dot2(bvec * r, bd), _dot2(k_eff * r, bd), g, bonus)
    for o_ref, val in zip(outs, vals):
        o_ref[...] = val


def rwkv_prep(p_rw, sh0, B, T, prm, chunked, precise):
    n_out = 8 if chunked else 11
    tm = _pick(T, (256, 128, 64, 32, 16, 8))
    nt = T // tm
    W = RWKV_W
    row_in = pl.BlockSpec((tm, RWKV_PAD_W), lambda b, t: (b * nt + t, 0))
    row_out = pl.BlockSpec((tm, W), lambda b, t: (b * nt + t, 0))
    full = lambda a: pl.BlockSpec(a.shape, lambda b, t: (0,) * a.ndim)
    sh0p = sh0.reshape(B, 1, RWKV_PAD_W)
    consts = [prm["mu"], prm["w0"], prm["a0"], prm["wlora"], prm["alora"], prm["gup"],
              prm["k_k"], prm["k_a"], prm["r_k"], prm["bd"]]
    outs = pl.pallas_call(
        functools.partial(_rwkv_prep_kernel, chunked=chunked, precise=precise),
        grid=(B, nt),
        in_specs=[row_in, pl.BlockSpec((1, 1, RWKV_PAD_W), lambda b, t: (b, 0, 0))] + [full(c) for c in consts],
        out_specs=[row_out] * n_out,
        out_shape=[jax.ShapeDtypeStruct((B * T, W), F32)] * n_out,
        scratch_shapes=[pltpu.VMEM((SUBLANES, RWKV_PAD_W), F32)],
        compiler_params=_cparams("arbitrary", "arbitrary"),
        name="rwkv_prep",
    )(p_rw, sh0p, *consts)
    return outs


PAIR_GROUP = 4
RWKV_SCAN_BATCH = 2


def _rwkv_scan_kernel(r_ref, w_ref, k_ref, kn_ref, b_ref, v_ref, wr_ref, br_ref, kr_ref, s0_ref, oseg_ref,
                      y_ref, sfin_ref, s_scr, *, nb, tc):
    ti = pl.program_id(1)

    @pl.when(ti == 0)
    def _():
        s_scr[...] = s0_ref[...]

    row = lax.broadcasted_iota(jnp.int32, (RWKV_HD, LANES), 0)
    lane = lax.broadcasted_iota(jnp.int32, (RWKV_HD, LANES), 1)
    diag = (lane & (RWKV_HD - 1)) == row
    oseg = oseg_ref[...]
    n_pairs = nb * PAIRS
    R = RWKV_HD

    def step8(t8, carry):
        base = pl.multiple_of(t8 * SUBLANES, SUBLANES)
        for g0 in range(0, n_pairs, PAIR_GROUP):
            ids = [(i // PAIRS, i % PAIRS) for i in range(g0, g0 + PAIR_GROUP)]
            n = len(ids)
            tile = lambda ref, b, p: ref[b, pl.ds(base, SUBLANES), p * LANES:(p + 1) * LANES]
            tiles = {nm: [tile(ref, b, p) for (b, p) in ids]
                     for nm, ref in (("kn", kn_ref), ("wr", wr_ref), ("v", v_ref), ("w", w_ref), ("b", b_ref),
                                     ("k", k_ref), ("br", br_ref), ("kr", kr_ref))}
            s_cur = [s_scr[b * PAIRS + p] for (b, p) in ids]
            y_rows = [[] for _ in ids]
            for j in range(SUBLANES):
                row = lambda nm, i: tiles[nm][i][j:j + 1, :]
                blocks = ([s_cur[i] * row("kn", i) for i in range(n)]
                          + [s_cur[i] * row("wr", i) for i in range(n)]
                          + [jnp.where(diag, row("v", i), 0.0) for i in range(n)])
                red = _dot2(jnp.concatenate(blocks, axis=0), oseg)
                for i in range(n):
                    skk = red[i * R:(i + 1) * R]
                    u = red[(n + i) * R:(n + i + 1) * R]
                    vb = red[(2 * n + i) * R:(2 * n + i + 1) * R]
                    s_cur[i] = s_cur[i] * row("w", i) - skk * row("b", i) + vb * row("k", i)
                    y_bc = u - skk * row("br", i) + vb * row("kr", i)
                    y_rows[i].append(jnp.sum(jnp.where(diag, y_bc, 0.0), axis=0, keepdims=True))
            for i, (b, p) in enumerate(ids):
                s_scr[b * PAIRS + p] = s_cur[i]
                y_ref[b, pl.ds(base, SUBLANES), p * LANES:(p + 1) * LANES] = jnp.concatenate(y_rows[i], axis=0)
        return carry

    lax.fori_loop(0, tc // SUBLANES, step8, 0)

    @pl.when(ti == pl.num_programs(1) - 1)
    def _():
        sfin_ref[...] = s_scr[...]


def rwkv_scan(seqs, s0, B, T):
    tc = _pick(T, (128, 64, 32, 16, 8))
    nb = RWKV_SCAN_BATCH
    assert B % nb == 0
    views = [s.reshape(B, T, RWKV_W) for s in seqs]
    s0p = s0.reshape(B, PAIRS, 2, RWKV_HD, RWKV_HD).transpose(0, 1, 3, 2, 4).reshape(B * PAIRS, RWKV_HD, LANES)
    li = jnp.arange(LANES)
    oseg = (li[:, None] // RWKV_HD == li[None, :] // RWKV_HD).astype(BF16)
    seq_spec = pl.BlockSpec((nb, tc, RWKV_W), lambda b, t: (b, t, 0))
    st_spec = pl.BlockSpec((nb * PAIRS, RWKV_HD, LANES), lambda b, t: (b, 0, 0))
    y, sfin = pl.pallas_call(
        functools.partial(_rwkv_scan_kernel, nb=nb, tc=tc),
        grid=(B // nb, T // tc),
        in_specs=[seq_spec] * 9 + [st_spec, pl.BlockSpec((LANES, LANES), lambda b, t: (0, 0))],
        out_specs=[seq_spec, st_spec],
        out_shape=[jax.ShapeDtypeStruct((B, T, RWKV_W), F32),
                   jax.ShapeDtypeStruct((B * PAIRS, RWKV_HD, LANES), F32)],
        scratch_shapes=[pltpu.VMEM((nb * PAIRS, RWKV_HD, LANES), F32)],
        compiler_params=_cparams("arbitrary", "arbitrary"),
        name="rwkv_scan",
    )(*views, s0p, oseg)
    sfin = sfin.reshape(B, PAIRS, RWKV_HD, 2, RWKV_HD).transpose(0, 1, 3, 2, 4).reshape(B, RWKV_HEADS, RWKV_HD, RWKV_HD)
    return y.reshape(B * T, RWKV_W), sfin


RWKV_CHUNK = 64
CHUNK_PAIRS = 8


def _split_bf(x):
    hi = x.astype(BF16)
    return hi, (x - hi.astype(F32)).astype(BF16)


def _dot3(a, b):
    ah, al = _split_bf(a)
    bh, bl = _split_bf(b)
    return _dot(ah, bh) + _dot(al, bh) + _dot(ah, bl)


def _dot3_nt(a, b):
    ah, al = _split_bf(a)
    bh, bl = _split_bf(b)
    return _dot_nt(ah, bh) + _dot_nt(al, bh) + _dot_nt(ah, bl)


def _rwkv_chunk_math(sts, rs, lws, ks, kns, bs, vs):
    C = RWKV_CHUNK
    lane = lax.broadcasted_iota(jnp.int32, (C, LANES), 1)
    row = lax.broadcasted_iota(jnp.int32, (C, LANES), 0)
    head0 = lane < RWKV_HD
    by_head = lambda x: jnp.concatenate([jnp.where(head0, x, 0.0), jnp.where(head0, 0.0, x)], axis=0)
    each = lambda f, *ls: [f(*a) for a in zip(*ls)]

    def cumsum_rows(lw):
        cs, d = lw, 1
        while d < C:
            cs = cs + jnp.where(row >= d, pltpu.roll(cs, d, axis=0), 0.0)
            d *= 2
        return cs

    css = each(cumsum_rows, lws)
    totals = [cs[C - 1:C, :] for cs in css]
    kps = each(lambda kn, cs, lw: kn * jnp.exp(cs - lw), kns, css, lws)
    rps = each(lambda r, cs: r * jnp.exp(cs), rs, css)
    e_negs = [jnp.exp(-cs) for cs in css]
    e_rems = each(lambda t, cs: jnp.exp(t - cs), totals, css)
    grams = each(lambda kp, rp, k, b, e: _dot3_nt(jnp.concatenate([kp, rp], axis=0),
                                                  jnp.concatenate([by_head(k * e), by_head(b * e)], axis=0)),
                 kps, rps, ks, bs, e_negs)
    s_idx = lane & (C - 1)
    strict, incl = s_idx < row, s_idx <= row
    a_ks = [jnp.where(strict, g[:C, :LANES], 0.0) for g in grams]
    a_bs = [jnp.where(strict, g[:C, LANES:], 0.0) for g in grams]
    d_ks = [jnp.where(incl, g[C:, :LANES], 0.0) for g in grams]
    d_bs = [jnp.where(incl, g[C:, LANES:], 0.0) for g in grams]
    r2 = lax.broadcasted_iota(jnp.int32, (2 * C, LANES), 0)
    l2 = lax.broadcasted_iota(jnp.int32, (2 * C, LANES), 1)
    eye = r2 == l2
    n_pows = [-by_head(a) for a in a_bs]
    t_invs = [jnp.where(eye, 1.0, 0.0) + n for n in n_pows]
    p = 2
    while p < C:
        n_pows = each(lambda n: _dot3(n, n), n_pows)
        t_invs = each(lambda t, n: t + _dot3(t, n), t_invs, n_pows)
        p *= 2
    v_bhs = [by_head(v) for v in vs]
    rhss = each(lambda kp, a, st, vb: _dot3(jnp.concatenate([kp, a], axis=1), jnp.concatenate([st, vb], axis=0)),
                kps, a_ks, sts, v_bhs)
    zs = each(lambda t, rhs: _dot3(t[:C] + t[C:], by_head(rhs)), t_invs, rhss)
    ys = each(lambda rp, dk, db, st, vb, z: _dot3(jnp.concatenate([rp, dk, -db], axis=1),
                                                  jnp.concatenate([st, vb, by_head(z)], axis=0)),
              rps, d_ks, d_bs, sts, v_bhs, zs)
    same_head = (r2 < RWKV_HD) == (l2 < RWKV_HD)

    def new_state(st, k, b, e, total, v, z):
        g_total = jnp.where(eye, jnp.broadcast_to(jnp.exp(total), (2 * C, LANES)), 0.0)
        kb_t = jnp.transpose(jnp.concatenate([k * e, b * e], axis=0))
        upd = _dot3(kb_t, jnp.concatenate([v, -z], axis=0))
        return _dot3(g_total, st) + jnp.where(same_head, upd, 0.0)

    return ys, each(new_state, sts, ks, bs, e_rems, totals, vs, zs)


def _rwkv_chunk_kernel(r_ref, lw_ref, k_ref, kn_ref, b_ref, v_ref, s0_ref, y_ref, sfin_ref, s_scr):
    ci = pl.program_id(2)

    @pl.when(ci == 0)
    def _():
        s_scr[...] = s0_ref[...]

    sls = [slice(i * LANES, (i + 1) * LANES) for i in range(CHUNK_PAIRS)]
    cols = lambda ref: [ref[:, sl] for sl in sls]
    ys, sts = _rwkv_chunk_math([s_scr[i] for i in range(CHUNK_PAIRS)], cols(r_ref), cols(lw_ref), cols(k_ref),
                               cols(kn_ref), cols(b_ref), cols(v_ref))
    for i, sl in enumerate(sls):
        y_ref[:, sl] = ys[i]
        s_scr[i] = sts[i]

    @pl.when(ci == pl.num_programs(2) - 1)
    def _():
        sfin_ref[...] = s_scr[...]


def rwkv_scan_chunked(seqs, s0, B, T):
    C, PP = RWKV_CHUNK, CHUNK_PAIRS
    assert T % C == 0 and PAIRS % PP == 0
    npg = PAIRS // PP
    views = [s.reshape(B, T, RWKV_W) for s in seqs]
    s0t = s0.reshape(B, PAIRS, 2, RWKV_HD, RWKV_HD).transpose(0, 1, 2, 4, 3)
    eye2 = jnp.eye(2, dtype=F32)
    s0bd = (s0t[:, :, :, :, None, :] * eye2[None, None, :, None, :, None]).reshape(B * PAIRS, LANES, LANES)
    seq_spec = pl.BlockSpec((None, C, PP * LANES), lambda b, g, c: (b, c, g))
    st_spec = pl.BlockSpec((PP, LANES, LANES), lambda b, g, c: (b * npg + g, 0, 0))
    y, sfin = pl.pallas_call(
        _rwkv_chunk_kernel,
        grid=(B, npg, T // C),
        in_specs=[seq_spec] * 6 + [st_spec],
        out_specs=[seq_spec, st_spec],
        out_shape=[jax.ShapeDtypeStruct((B, T, RWKV_W), F32),
                   jax.ShapeDtypeStruct((B * PAIRS, LANES, LANES), F32)],
        scratch_shapes=[pltpu.VMEM((PP, LANES, LANES), F32)],
        compiler_params=_cparams("parallel", "parallel", "arbitrary"),
        name="rwkv_chunk",
    )(*views, s0bd)
    sf = sfin.reshape(B, PAIRS, 2, RWKV_HD, 2, RWKV_HD)
    sf = jnp.stack([sf[:, :, 0, :, 0, :], sf[:, :, 1, :, 1, :]], axis=2)
    return y.reshape(B * T, RWKV_W), sf.transpose(0, 1, 2, 4, 3).reshape(B, RWKV_HEADS, RWKV_HD, RWKV_HD)


def _rwkv_post_kernel(y_ref, bonus_ref, g_ref, gng_ref, gnb_ref, bd_ref, o_ref):
    y = y_ref[...]
    bd = bd_ref[...]
    inv = 1.0 / RWKV_HD
    mean = _dot2(y, bd) * inv
    yc = y - mean
    var = _dot2(yc * yc, bd) * inv
    yn = yc * lax.rsqrt(var + GN_EPS) * gng_ref[...] + gnb_ref[...]
    o_ref[...] = (yn + bonus_ref[...]) * g_ref[...]


def rwkv_post(y, bonus, g, gn_g, gn_b, bd):
    m, W = y.shape
    tm = _pick(m, (256, 128, 64, 32, 16, 8))
    row = pl.BlockSpec((tm, W), lambda i: (i, 0))
    vec = pl.BlockSpec((1, W), lambda i: (0, 0))
    return pl.pallas_call(
        _rwkv_post_kernel,
        grid=(m // tm,),
        in_specs=[row, row, row, vec, vec, pl.BlockSpec((W, W), lambda i: (0, 0))],
        out_specs=row,
        out_shape=jax.ShapeDtypeStruct((m, W), F32),
        compiler_params=_cparams("parallel"),
        name="rwkv_post",
    )(y, bonus, g, gn_g.reshape(1, W), gn_b.reshape(1, W), bd)


def _gelu_tanh(x):
    return 0.5 * x * (1.0 + jnp.tanh(math.sqrt(2.0 / math.pi) * (x + 0.044715 * (x * x * x))))


def _neg_expm1(x):
    series = -x * (1.0 + x * (1.0 / 2.0) * (1.0 + x * (1.0 / 3.0) * (1.0 + x * (1.0 / 4.0) * (1.0 + x * (1.0 / 5.0)
             * (1.0 + x * (1.0 / 6.0))))))
    return jnp.where(x > -0.05, series, 1.0 - jnp.exp(x))


def _lru_kernel(px_ref, pg_ref, c0_ref, h0_ref, cw_ref, cb_ref, wa_ref, ba_ref, wx_ref, bx_ref, lam_ref,
                y_ref, hlast_ref, carry, hcar, *, precise):
    ti = pl.program_id(1)

    @pl.when(ti == 0)
    def _():
        carry[...] = c0_ref[0]
        hcar[...] = h0_ref[0]

    x = px_ref[...]
    tm = x.shape[0]
    cw = cw_ref[...]
    xc = cb_ref[...] + cw[CONV_W - 1:CONV_W, :] * x
    for j in range(1, CONV_W):
        xc = xc + cw[CONV_W - 1 - j:CONV_W - j, :] * _shifted(carry, x, j)
    carry[...] = x[tm - SUBLANES:, :]
    xcm = xc if precise else xc.astype(BF16)
    r = _sigmoid(_mdot(xcm, wa_ref[...], precise) + ba_ref[...])
    i = _sigmoid(_mdot(xcm, wx_ref[...], precise) + bx_ref[...])
    log_a = (-LRU_C * _softplus(-lam_ref[...])) * r
    a = jnp.exp(log_a)
    b = jnp.sqrt(_neg_expm1(2.0 * log_a)) * (i * xc)
    rows = lax.broadcasted_iota(jnp.int32, a.shape, 0)
    d = 1
    while d < tm:
        a_sh = pltpu.roll(a, d, axis=0)
        b_sh = pltpu.roll(b, d, axis=0)
        keep = rows >= d
        b = jnp.where(keep, a * b_sh + b, b)
        a = jnp.where(keep, a * a_sh, a)
        d *= 2
    h = a * hcar[...] + b
    hcar[...] = h[tm - 1:tm, :]
    hlast_ref[0] = h[tm - 1:tm, :]
    y_ref[...] = h * _gelu_tanh(pg_ref[...])


def lru_mix(p_l, c0, h0, B, T, prm, precise):
    W = LRU_W
    tm = _pick(T, (256, 128, 64, 32, 16, 8))
    nt = T // tm
    c0p = jnp.concatenate([jnp.zeros((B, SUBLANES - (CONV_W - 1), W), F32), c0], axis=1)
    h0p = h0.reshape(B, 1, W)
    full = lambda a: pl.BlockSpec(a.shape, lambda b, t: (0,) * a.ndim)
    consts = [prm["conv_w"], prm["conv_b"], prm["wa"], prm["ba"], prm["wx"], prm["bx"], prm["lam"]]
    y, hlast = pl.pallas_call(
        functools.partial(_lru_kernel, precise=precise),
        grid=(B, nt),
        in_specs=[pl.BlockSpec((tm, W), lambda b, t: (b * nt + t, 0)),
                  pl.BlockSpec((tm, W), lambda b, t: (b * nt + t, 1)),
                  pl.BlockSpec((1, SUBLANES, W), lambda b, t: (b, 0, 0)),
                  pl.BlockSpec((1, 1, W), lambda b, t: (b, 0, 0))] + [full(c) for c in consts],
        out_specs=[pl.BlockSpec((tm, W), lambda b, t: (b * nt + t, 0)),
                   pl.BlockSpec((1, 1, W), lambda b, t: (b, 0, 0))],
        out_shape=[jax.ShapeDtypeStruct((B * T, W), F32), jax.ShapeDtypeStruct((B, 1, W), F32)],
        scratch_shapes=[pltpu.VMEM((SUBLANES, W), F32), pltpu.VMEM((1, W), F32)],
        compiler_params=_cparams("arbitrary", "arbitrary"),
        name="lru_mix",
    )(p_l, p_l, c0p, h0p, *consts)
    return y, hlast.reshape(B, W)


def _block_diag(w):
    n, d, e = w.shape
    eye = jnp.eye(n, dtype=w.dtype)
    return (eye[:, None, :, None] * w[:, :, None, :]).reshape(n * d, n * e)


def rec_params(li, rec_w_in, rec_mu, rwkv_w0, rwkv_w_up, rwkv_a0, rwkv_a_up, rwkv_g_up, rwkv_k_k, rwkv_k_a,
               rwkv_r_k, rwkv_gn_g, rwkv_gn_b, lru_conv_w, lru_conv_b, lru_wa, lru_ba, lru_wx, lru_bx, lru_lambda,
               rec_w_out):
    W = RWKV_W
    pad = RWKV_PAD_W - RWKV_SHIFT_W
    hi = jnp.arange(W) // RWKV_HD
    zl = jnp.zeros((LORA_DECAY, W), F32)
    return dict(
        w_rw=jnp.pad(rec_w_in[li][:, :RWKV_SHIFT_W], ((0, 0), (0, pad))),
        w_l=rec_w_in[li][:, RWKV_SHIFT_W:],
        mu=jnp.pad(rec_mu[li], (0, pad)).reshape(1, RWKV_PAD_W),
        w0=rwkv_w0[li].reshape(1, W), a0=rwkv_a0[li].reshape(1, W),
        wlora=jnp.concatenate([rwkv_w_up[li], zl], axis=0),
        alora=jnp.concatenate([zl, rwkv_a_up[li]], axis=0),
        gup=rwkv_g_up[li],
        k_k=rwkv_k_k[li].reshape(1, W), k_a=rwkv_k_a[li].reshape(1, W), r_k=rwkv_r_k[li].reshape(1, W),
        bd=(hi[:, None] == hi[None, :]).astype(BF16),
        gn_g=rwkv_gn_g[li], gn_b=rwkv_gn_b[li],
        conv_w=lru_conv_w[li], conv_b=lru_conv_b[li].reshape(1, LRU_W),
        wa=_block_diag(lru_wa[li]), ba=lru_ba[li].reshape(1, LRU_W),
        wx=_block_diag(lru_wx[li]), bx=lru_bx[li].reshape(1, LRU_W),
        lam=lru_lambda[li].reshape(1, LRU_W),
        w_out=rec_w_out[li],
    )


def rec_mixer(x2d, B, T, S0, sh0, h0, c0, prm, precise=False):
    p_rw = mm(x2d, prm["w_rw"], precise)
    p_l = mm(x2d, prm["w_l"], precise)
    sh0p = jnp.pad(sh0, ((0, 0), (0, RWKV_PAD_W - RWKV_SHIFT_W)))
    chunked = T % RWKV_CHUNK == 0
    outs = rwkv_prep(p_rw, sh0p, B, T, prm, chunked, precise)
    g, bonus = outs[-2:]
    if chunked:
        y_raw, s_fin = rwkv_scan_chunked(outs[:6], S0, B, T)
    else:
        y_raw, s_fin = rwkv_scan(outs[:9], S0, B, T)
    y_rw = rwkv_post(y_raw, bonus, g, prm["gn_g"], prm["gn_b"], prm["bd"])
    y_lru, h_last = lru_mix(p_l, c0, h0, B, T, prm, precise)
    y = mm_cat2(y_rw, y_lru, prm["w_out"], precise)
    sh = p_rw.reshape(B, T, RWKV_PAD_W)[:, -1, :RWKV_SHIFT_W]
    px = p_l.reshape(B, T, 2 * LRU_W)[:, :, :LRU_W]
    cbuf = jnp.concatenate([c0, px], axis=1)[:, -(CONV_W - 1):] if T < CONV_W - 1 else px[:, -(CONV_W - 1):]
    return y, s_fin, sh, h_last, cbuf


def _dot_nt(a, b):
    return lax.dot_general(a, b, (((1,), (1,)), ((), ())), preferred_element_type=F32)


def _compress_kernel(slab_ref, w_ref, o_ref, *, n_pieces, precise):
    acc = None
    for l in range(CMP_STRIDE):
        d = _mdot(slab_ref[pl.ds(l, n_pieces, stride=CMP_STRIDE), :], w_ref[l], precise)
        acc = d if acc is None else acc + d
    out = acc[:, :HEAD_DIM] + pltpu.roll(acc[:, HEAD_DIM:], n_pieces - 1, axis=0)
    o_ref[0:n_pieces, :] = out
    pad = o_ref.shape[0] - n_pieces
    if pad:
        o_ref[n_pieces:, :] = jnp.zeros((pad, HEAD_DIM), F32)


def nsa_compress(kv2d, col0, B, t_kv, wcat, ncp, precise):
    n_pieces = t_kv // CMP_STRIDE
    return pl.pallas_call(
        functools.partial(_compress_kernel, n_pieces=n_pieces, precise=precise),
        grid=(B, 2, N_KV),
        in_specs=[pl.BlockSpec((t_kv, HEAD_DIM), lambda b, c, g: (b, col0 + c * N_KV + g)),
                  pl.BlockSpec((None, CMP_STRIDE, HEAD_DIM, 2 * HEAD_DIM), lambda b, c, g: (c, 0, 0, 0))],
        out_specs=pl.BlockSpec((None, None, None, ncp, HEAD_DIM), lambda b, c, g: (b, c, g, 0, 0)),
        out_shape=jax.ShapeDtypeStruct((B, 2, N_KV, ncp, HEAD_DIM), F32),
        compiler_params=_cparams("parallel", "parallel", "parallel"),
        name="nsa_compress",
    )(kv2d, wcat)


MASKED_DIST = 1e30
EXP2_SCALE = SCALE * math.log2(math.e)


def _parts(a, precise):
    return _split_bf(a) if precise else (a.astype(BF16),)


def _pdot(a, b, nt=False):
    dot = _dot_nt if nt else _dot
    out = dot(a[0], b[0])
    if len(a) > 1:
        out = out + dot(a[1], b[0]) + dot(a[0], b[1])
    return out


def _online_softmax_step(q, k, v, cdist, m, l, acc, precise):
    u = _pdot(q, k, nt=True) - cdist
    m_new = jnp.maximum(m, jnp.max(u, axis=-1, keepdims=True))
    alpha = jnp.exp2((m - m_new) * EXP2_SCALE)
    p = jnp.exp2((u - m_new) * EXP2_SCALE)
    l = alpha * l + jnp.sum(p, axis=-1, keepdims=True)
    acc = alpha * acc + _pdot(_parts(p, precise), v)
    return m_new, l, acc


def _nsa_kernel(q_ref, kc_ref, vc_ref, ks_ref, vs_ref, kw_ref, vw_ref, gate_ref, gb_ref, ov_ref, o_ref, *,
                tq, q_off, nc_real, ns_real, tk, n_kv_static, w_off, w_len, w_follows_q, precise, stack_heads):
    g = pl.program_id(1)
    qi = pl.program_id(2)
    q0 = qi * tq
    ncp = kc_ref.shape[0]
    nsp = ov_ref.shape[1]
    qpos_col = q_off + q0 + lax.broadcasted_iota(jnp.int32, (tq, 1), 0)
    q_all = q_ref[...]
    q_bf = [_parts(q_all[:, r * HEAD_DIM:(r + 1) * HEAD_DIM], precise) for r in range(GQA_R)]
    head_f = (g * GQA_R).astype(F32)
    slopes = [jnp.exp2(-0.5 * (jnp.full((1, 1), 1.0 + r, F32) + head_f)) for r in range(GQA_R)]

    n_idx = lax.broadcasted_iota(jnp.int32, (tq, ncp), 1)
    dist_c = qpos_col - (n_idx * CMP_STRIDE + (CMP_BLOCK - 1))
    ok_c = (dist_c >= 0) & (n_idx < nc_real)
    dist_cf = dist_c.astype(F32)
    kc_bf = _parts(kc_ref[...], precise)
    vc_bf = _parts(vc_ref[...], precise)
    o_c = []
    p_sum = None
    for r in range(GQA_R):
        s = _pdot(q_bf[r], kc_bf, nt=True) * SCALE - slopes[r] * dist_cf
        s = jnp.where(ok_c, s, NEG)
        m = jnp.max(s, axis=-1, keepdims=True)
        e = jnp.where(ok_c, jnp.exp(s - m), 0.0)
        p = e / jnp.maximum(jnp.sum(e, axis=-1, keepdims=True), 1e-30)
        o_c.append(_pdot(_parts(p, precise), vc_bf))
        p_sum = p if p_sum is None else p_sum + p
    imp = _dot2(p_sum, ov_ref[...])
    s_idx = lax.broadcasted_iota(jnp.int32, (tq, nsp), 1)
    cur = lax.shift_right_logical(qpos_col, int(math.log2(SLC_BLOCK)))
    forced = (s_idx == 0) | (s_idx == cur) | (s_idx == cur - 1)
    causal_blk = (s_idx * SLC_BLOCK <= qpos_col) & (s_idx < ns_real)
    score = jnp.where(causal_blk, imp + jnp.where(forced, BIG, 0.0), NEG)
    n_acc = 4
    ranks = [jnp.zeros((tq, nsp), jnp.int32) for _ in range(n_acc)]
    for j in range(ns_real):
        col = score[:, j:j + 1]
        ahead = (col > score) | ((col == score) & (s_idx > j))
        ranks[j % n_acc] = ranks[j % n_acc] + jnp.where(ahead, 1, 0)
    rank = (ranks[0] + ranks[1]) + (ranks[2] + ranks[3])
    sel = (rank < min(TOP_N, ns_real)) & (score > 0.5 * NEG)
    sel_bf = jnp.where(sel, 1.0, 0.0).astype(BF16)

    cslopes = [sl * (1.0 / SCALE) for sl in slopes]
    if stack_heads:
        qg = [_parts(jnp.concatenate([q_all[:, r * HEAD_DIM:(r + 1) * HEAD_DIM] for r in range(GQA_R)], axis=0),
                     precise)]
        csg = [jnp.concatenate([jnp.broadcast_to(c, (tq, 1)) for c in cslopes], axis=0)]
        rep = lambda x: jnp.concatenate([x] * GQA_R, axis=0)
        rows = GQA_R * tq
        head_rows = lambda st, r: tuple(a[r * tq:(r + 1) * tq] for a in st[0])
    else:
        qg, csg, rep, rows = q_bf, cslopes, (lambda x: x), tq
        head_rows = lambda st, r: st[r]

    def init():
        return tuple((jnp.full((rows, 1), NEG, F32), jnp.zeros((rows, 1), F32), jnp.zeros((rows, HEAD_DIM), F32))
                     for _ in qg)

    blk_per_tile = tk // SLC_BLOCK
    ex_row = lax.broadcasted_iota(jnp.int32, (nsp, tk), 0)
    ex_col = lax.shift_right_logical(lax.broadcasted_iota(jnp.int32, (nsp, tk), 1), int(math.log2(SLC_BLOCK)))
    key_i = lax.broadcasted_iota(jnp.int32, (tq, tk), 1)

    def slc_body(kt, carry):
        base = pl.multiple_of(kt * tk, tk)
        expand = jnp.where(ex_row == ex_col + kt * blk_per_tile, 1.0, 0.0).astype(BF16)
        selexp = _dot(sel_bf, expand)
        dist = qpos_col - (key_i + base)
        distm = rep(jnp.where(selexp > 0.5, jnp.where(dist >= 0, dist.astype(F32), MASKED_DIST), MASKED_DIST))
        k_p = _parts(ks_ref[pl.ds(base, tk), :], precise)
        v_p = _parts(vs_ref[pl.ds(base, tk), :], precise)
        return tuple(_online_softmax_step(qg[i], k_p, v_p, csg[i] * distm, *carry[i], precise)
                     for i in range(len(qg)))

    if n_kv_static is None:
        n_kv = (q0 + tq + tk - 1) // tk
    else:
        n_kv = n_kv_static
    st_s = lax.fori_loop(0, n_kv, slc_body, init())

    if w_follows_q:
        w_base = pl.multiple_of(jnp.maximum(q0 + tq - w_len, 0), SUBLANES)
    else:
        w_base = 0
    dist_w = qpos_col - (w_off + w_base + lax.broadcasted_iota(jnp.int32, (tq, w_len), 1))
    ok_w = jnp.where(dist_w >= 0, dist_w, WINDOW) < WINDOW
    dist_wm = rep(jnp.where(ok_w, dist_w.astype(F32), MASKED_DIST))
    kw_p = _parts(kw_ref[pl.ds(w_base, w_len), :], precise)
    vw_p = _parts(vw_ref[pl.ds(w_base, w_len), :], precise)
    st_w = tuple(_online_softmax_step(qg[i], kw_p, vw_p, csg[i] * dist_wm, *init()[i], precise)
                 for i in range(len(qg)))

    gates = _sigmoid(gate_ref[...] + gb_ref[...])
    lane = lax.broadcasted_iota(jnp.int32, gates.shape, 1)
    for r in range(GQA_R):
        head = g * GQA_R + r
        gcol = lambda br: jnp.sum(jnp.where(lane == br * N_HEADS + head, gates, 0.0), axis=-1, keepdims=True)
        _, l_s, a_s = head_rows(st_s, r)
        _, l_w, a_w = head_rows(st_w, r)
        o_s = a_s / jnp.maximum(l_s, 1e-30)
        o_w = a_w / jnp.maximum(l_w, 1e-30)
        o_ref[:, r * HEAD_DIM:(r + 1) * HEAD_DIM] = gcol(0) * o_c[r] + gcol(1) * o_s + gcol(2) * o_w


def nsa_attention(q2d, kcvc, kv2d, kv_col0, win2d, win_col0, gates, gate_b, overlap, *, B, t_q, t_kv, t_win, tq,
                  q_off, nc_real, ns_real, tk, n_kv_static, w_off, w_len, w_follows_q, precise, stack_heads):
    nq = t_q // tq
    ncp = kcvc.shape[3]
    nsp = overlap.shape[1]
    qw = GQA_R * HEAD_DIM
    slab = lambda arr_t, col: pl.BlockSpec((arr_t, HEAD_DIM), col)
    kern = functools.partial(_nsa_kernel, tq=tq, q_off=q_off, nc_real=nc_real, ns_real=ns_real, tk=tk,
                             n_kv_static=n_kv_static, w_off=w_off, w_len=w_len, w_follows_q=w_follows_q,
                             precise=precise, stack_heads=stack_heads)
    return pl.pallas_call(
        kern,
        grid=(B, N_KV, nq),
        in_specs=[pl.BlockSpec((tq, qw), lambda b, g, i: (b * nq + i, g)),
                  pl.BlockSpec((None, None, None, ncp, HEAD_DIM), lambda b, g, i: (b, 0, g, 0, 0)),
                  pl.BlockSpec((None, None, None, ncp, HEAD_DIM), lambda b, g, i: (b, 1, g, 0, 0)),
                  slab(t_kv, lambda b, g, i: (b, kv_col0 + 2 * N_KV + g)),
                  slab(t_kv, lambda b, g, i: (b, kv_col0 + 3 * N_KV + g)),
                  slab(t_win, lambda b, g, i: (b, win_col0 + g)),
                  slab(t_win, lambda b, g, i: (b, win_col0 + N_KV + g)),
                  pl.BlockSpec((tq, LANES), lambda b, g, i: (b * nq + i, 0)),
                  pl.BlockSpec((1, LANES), lambda b, g, i: (0, 0)),
                  pl.BlockSpec((ncp, nsp), lambda b, g, i: (0, 0))],
        out_specs=pl.BlockSpec((tq, qw), lambda b, g, i: (b * nq + i, g)),
        out_shape=jax.ShapeDtypeStruct((B * t_q, N_HEADS * HEAD_DIM), F32),
        compiler_params=_cparams("parallel", "parallel", "arbitrary"),
        name="nsa_attention",
    )(q2d, kcvc, kcvc, kv2d, kv2d, win2d, win2d, gates, gate_b, overlap)


def _overlap_matrix(ncp, nsp, nc_real, ns_real):
    n = jnp.arange(ncp)[:, None]
    s = jnp.arange(nsp)[None, :]
    c0 = n * CMP_STRIDE
    s0 = s * SLC_BLOCK
    ov = (c0 < s0 + SLC_BLOCK) & (c0 + CMP_BLOCK > s0) & (n < nc_real) & (s < ns_real)
    return ov.astype(BF16)


PAGES_PER_STEP = 3


def _gather_pages_kernel(pt_ref, *refs, n_pages):
    page_refs, tail_ref, o_ref = refs[:PAGES_PER_STEP], refs[PAGES_PER_STEP], refs[PAGES_PER_STEP + 1]
    j = pl.program_id(1)
    page = tail_ref.shape[0]
    for i, page_ref in enumerate(page_refs):
        slot = j * PAGES_PER_STEP + i

        @pl.when(slot < n_pages)
        def _():
            o_ref[i * page:(i + 1) * page, :] = page_ref[...]

        @pl.when(slot == n_pages)
        def _():
            o_ref[i * page:(i + 1) * page, :] = tail_ref[...]


def gather_pages(cache5, page_table, tail, li):
    B, n_pages = page_table.shape
    page, row_w = cache5.shape[1], cache5.shape[4]
    pps = PAGES_PER_STEP
    assert (n_pages + 1) % pps == 0
    page_spec = lambda i: pl.BlockSpec(
        (None, page, None, None, row_w),
        lambda b, j, pt: (pt[b, jnp.minimum(j * pps + i, n_pages - 1)], 0, li, 0, 0))
    grid_spec = pltpu.PrefetchScalarGridSpec(
        num_scalar_prefetch=1,
        grid=(B, (n_pages + 1) // pps),
        in_specs=[page_spec(i) for i in range(pps)] + [pl.BlockSpec((None, page, row_w), lambda b, j, pt: (b, 0, 0))],
        out_specs=pl.BlockSpec((None, pps * page, row_w), lambda b, j, pt: (b, j, 0)),
    )
    return pl.pallas_call(
        functools.partial(_gather_pages_kernel, n_pages=n_pages),
        grid_spec=grid_spec,
        out_shape=jax.ShapeDtypeStruct((B, (n_pages + 1) * page, row_w), F32),
        compiler_params=_cparams("parallel", "arbitrary"),
        name="gather_pages",
    )(page_table, *([cache5] * pps), tail)


def attn_params(li, attn_w_in, attn_gate_b, w_cmp_k, w_cmp_v, attn_w_out):
    qkv_w = N_HEADS * HEAD_DIM + 6 * N_KV * HEAD_DIM
    ng = 3 * N_HEADS
    half = lambda w: jnp.concatenate([w[:CMP_STRIDE], w[CMP_STRIDE:]], axis=-1)
    return dict(
        w_main=attn_w_in[li][:, :qkv_w],
        w_gate=jnp.pad(attn_w_in[li][:, qkv_w:], ((0, 0), (0, LANES - ng))),
        gate_b=jnp.pad(attn_gate_b[li], (0, LANES - ng)).reshape(1, LANES),
        wcat=jnp.stack([half(w_cmp_k[li]), half(w_cmp_v[li])]),
        w_out=attn_w_out[li],
    )


def attn_prompt(x2d, B, T, prm):
    p = mm(x2d, prm["w_main"])
    gates = mm(x2d, prm["w_gate"])
    q_blocks = N_HEADS
    n_pieces = T // CMP_STRIDE
    nc_real = n_pieces - CMP_BLOCK // CMP_STRIDE + 1
    ns_real = T // SLC_BLOCK
    ncp = -(-n_pieces // LANES) * LANES
    nsp = -(-ns_real // LANES) * LANES
    kcvc = nsa_compress(p, q_blocks, B, T, prm["wcat"], ncp, False)
    ov = _overlap_matrix(ncp, nsp, nc_real, ns_real)
    tq = min(128, T)
    o = nsa_attention(p, kcvc, p, q_blocks, p, q_blocks + 4 * N_KV, gates, prm["gate_b"], ov,
                      B=B, t_q=T, t_kv=T, t_win=T, tq=tq, q_off=0, nc_real=nc_real, ns_real=ns_real,
                      tk=min(256, T), n_kv_static=None, w_off=0, w_len=min(WINDOW + tq, T), w_follows_q=True,
                      precise=False, stack_heads=False)
    y = mm(o, prm["w_out"])
    kv = p.reshape(B, T, -1)[:, :, N_HEADS * HEAD_DIM:].reshape(B, T, 6, N_KV, HEAD_DIM)
    w_buf = min(WINDOW, T)
    win = jnp.pad(kv[:, :, 4:], ((0, 0), (WINDOW, 0), (0, 0), (0, 0), (0, 0)))[:, -w_buf:]
    return y, kv[:, :, :4], win


def attn_sample(x2d, B, T, prm, cache_nsa_kv, cache_win, page_table, li):
    n_pool, page = cache_nsa_kv.shape[:2]
    n_attn = cache_nsa_kv.shape[2]
    n_pages = page_table.shape[1]
    past_len = n_pages * page
    row_w = 4 * N_KV * HEAD_DIM
    p = mm(x2d, prm["w_main"], True)
    gates = mm(x2d, prm["w_gate"], True)
    kv_new = p[:, N_HEADS * HEAD_DIM:].reshape(B, T, 6, N_KV, HEAD_DIM)
    tail = jnp.pad(kv_new[:, :, :4].reshape(B, T, row_w), ((0, 0), (0, page - T), (0, 0)))
    full = gather_pages(cache_nsa_kv.reshape(n_pool, page, n_attn, 1, row_w), page_table, tail, li)
    t_kv = (n_pages + 1) * page
    full2d = full.reshape(B * t_kv, row_w)
    L = past_len + T
    l_pad = -(-L // SLC_BLOCK) * SLC_BLOCK
    nc_real = l_pad // CMP_STRIDE - CMP_BLOCK // CMP_STRIDE + 1
    ns_real = l_pad // SLC_BLOCK
    n_pieces = t_kv // CMP_STRIDE
    ncp = -(-n_pieces // LANES) * LANES
    nsp = -(-ns_real // LANES) * LANES
    kcvc = nsa_compress(full2d, 0, B, t_kv, prm["wcat"], ncp, True)
    ov = _overlap_matrix(ncp, nsp, nc_real, ns_real)
    w_buf = cache_win.shape[1]
    win = jnp.concatenate([cache_win, kv_new[:, :, 4:]], axis=1)
    t_win = -(-(w_buf + T) // LANES) * LANES
    win2d = jnp.pad(win.reshape(B, w_buf + T, 2 * N_KV * HEAD_DIM), ((0, 0), (0, t_win - (w_buf + T)), (0, 0)))
    win2d = win2d.reshape(B * t_win, 2 * N_KV * HEAD_DIM)
    n_kv = 3
    assert t_kv % (n_kv * LANES) == 0
    o = nsa_attention(p, kcvc, full2d, 0, win2d, 0, gates, prm["gate_b"], ov,
                      B=B, t_q=T, t_kv=t_kv, t_win=t_win, tq=T, q_off=past_len, nc_real=nc_real, ns_real=ns_real,
                      tk=t_kv // n_kv, n_kv_static=n_kv, w_off=past_len - w_buf, w_len=t_win, w_follows_q=False,
                      precise=True, stack_heads=True)
    y = mm(o, prm["w_out"], True)
    return y, kv_new[:, :, :4], win[:, -w_buf:]


MOE_TM = 256


ROUTER_TN = 256


def _first_argmax(vals):
    best, idx = vals[0], jnp.zeros(vals[0].shape, jnp.int32)
    for i in range(1, len(vals)):
        better = vals[i] > best
        best = jnp.where(better, vals[i], best)
        idx = jnp.where(better, i, idx)
    return best, idx


def _pick_by(idx, vals):
    out = vals[-1]
    for i in range(len(vals) - 2, -1, -1):
        out = jnp.where(idx == i, vals[i], out)
    return out


def _router_kernel(x_ref, wt_ref, bias_ref, e_ref, w_ref, rank_ref, cnt_ref, base, *, n_tok):
    i = pl.program_id(0)
    tn = x_ref.shape[0]

    @pl.when(i == 0)
    def _():
        base[...] = jnp.zeros(base.shape, F32)

    xh, xl = _split2(x_ref[...])
    wh, wl = _split2(wt_ref[...])
    logits = _dot_nt(wh, xh) + _dot_nt(wh, xl) + _dot_nt(wl, xh)
    aff = _sigmoid(logits)
    sel = aff + bias_ref[...]
    s = [sel[e:e + 1, :] for e in range(N_EXPERTS)]
    a = [aff[e:e + 1, :] for e in range(N_EXPERTS)]
    G = EXPERTS_PER_GROUP
    gscore = []
    for g in range(N_GROUPS):
        v0, v1, v2, v3 = s[G * g:G * g + G]
        hi1, lo1 = jnp.maximum(v0, v1), jnp.minimum(v0, v1)
        hi2, lo2 = jnp.maximum(v2, v3), jnp.minimum(v2, v3)
        gscore.append(jnp.maximum(hi1, hi2) + jnp.maximum(jnp.minimum(hi1, hi2), jnp.maximum(lo1, lo2)))
    _, gi = _first_argmax(gscore)
    sg = [_pick_by(gi, [s[G * g + j] for g in range(N_GROUPS)]) for j in range(G)]
    ag = [_pick_by(gi, [a[G * g + j] for g in range(N_GROUPS)]) for j in range(G)]
    _, l0 = _first_argmax(sg)
    _, l1 = _first_argmax([jnp.where(l0 == j, -jnp.inf, sg[j]) for j in range(G)])
    w0, w1 = _pick_by(l0, ag), _pick_by(l1, ag)
    wsum = w0 + w1
    e0, e1 = gi * G + l0, gi * G + l1
    e_ref[...] = jnp.concatenate([e0, e1], axis=0)
    w_ref[...] = jnp.concatenate([w0 / wsum, w1 / wsum], axis=0)
    tok = i * tn + lax.broadcasted_iota(jnp.int32, (N_EXPERTS, tn), 1)
    eid = lax.broadcasted_iota(jnp.int32, (N_EXPERTS, tn), 0)
    valid = tok < n_tok
    oh0 = jnp.where((eid == e0) & valid, 1.0, 0.0)
    oh1 = jnp.where((eid == e1) & valid, 1.0, 0.0)
    cnt = oh0 + oh1
    r_i = lax.broadcasted_iota(jnp.int32, (tn, tn), 0)
    c_i = lax.broadcasted_iota(jnp.int32, (tn, tn), 1)
    upper = jnp.where(r_i <= c_i, 1.0, 0.0).astype(BF16)
    excl = _dot(cnt.astype(BF16), upper) - cnt + base[...]
    rank_ref[...] = jnp.concatenate([jnp.sum(oh0 * excl, axis=0, keepdims=True),
                                     jnp.sum(oh1 * excl, axis=0, keepdims=True)], axis=0).astype(jnp.int32)
    base[...] = base[...] + jnp.sum(cnt, axis=1, keepdims=True)
    cnt_ref[...] = jnp.broadcast_to(base[...], cnt_ref.shape).astype(jnp.int32)


def router(x, w_t, bias_col):
    n, d = x.shape
    tn = ROUTER_TN if n >= ROUTER_TN else LANES
    nt = -(-n // tn)
    n_pad = nt * tn
    if n < tn:
        x = jnp.pad(x, ((0, tn - n), (0, 0)))
    two = lambda dt: jax.ShapeDtypeStruct((TOP_K, n_pad), dt)
    out_row = pl.BlockSpec((TOP_K, tn), lambda i: (0, i))
    e, w, rank, cnt = pl.pallas_call(
        functools.partial(_router_kernel, n_tok=n),
        grid=(nt,),
        in_specs=[pl.BlockSpec((tn, d), lambda i: (i, 0)),
                  pl.BlockSpec((N_EXPERTS, d), lambda i: (0, 0)),
                  pl.BlockSpec((N_EXPERTS, 1), lambda i: (0, 0))],
        out_specs=[out_row, out_row, out_row, pl.BlockSpec((N_EXPERTS, LANES), lambda i: (0, 0))],
        out_shape=[two(jnp.int32), two(F32), two(jnp.int32), jax.ShapeDtypeStruct((N_EXPERTS, LANES), jnp.int32)],
        scratch_shapes=[pltpu.VMEM((N_EXPERTS, 1), F32)],
        compiler_params=_cparams("arbitrary"),
        name="router",
    )(x, w_t, bias_col)
    return e[:, :n], w[:, :n], rank[:, :n], cnt[:, 0]


def _expert_kernel(te_ref, x_ref, rw_ref, wg_ref, wu_ref, wd_ref, o_ref):
    xb = x_ref[...].astype(BF16)
    hg = _dot(xb, wg_ref[...])
    hu = _dot(xb, wu_ref[...])
    h = (hg * _sigmoid(hg)) * hu * rw_ref[...]
    o_ref[...] = _dot(h.astype(BF16), wd_ref[...])


def expert_ffn(x_sorted, row_w, tile_expert, wg, wu, wd, layer):
    a_pad, d = x_sorted.shape
    f = wg.shape[3]
    n_tiles = a_pad // MOE_TM
    grid_spec = pltpu.PrefetchScalarGridSpec(
        num_scalar_prefetch=1,
        grid=(n_tiles,),
        in_specs=[pl.BlockSpec((MOE_TM, d), lambda t, te: (t, 0)),
                  pl.BlockSpec((MOE_TM, 1), lambda t, te: (t, 0)),
                  pl.BlockSpec((None, None, d, f), lambda t, te: (layer, te[t], 0, 0)),
                  pl.BlockSpec((None, None, d, f), lambda t, te: (layer, te[t], 0, 0)),
                  pl.BlockSpec((None, None, f, d), lambda t, te: (layer, te[t], 0, 0))],
        out_specs=pl.BlockSpec((MOE_TM, d), lambda t, te: (t, 0)),
    )
    return pl.pallas_call(
        _expert_kernel,
        grid_spec=grid_spec,
        out_shape=jax.ShapeDtypeStruct((a_pad, d), F32),
        compiler_params=_cparams("arbitrary"),
        name="expert_ffn",
    )(tile_expert, x_sorted, row_w, wg, wu, wd)


MOE_DENSE_FT = 512


def _moe_dense_kernel(x_ref, gate_ref, wg_ref, wu_ref, wd_ref, o_ref):
    e = pl.program_id(0)

    @pl.when((e == 0) & (pl.program_id(1) == 0))
    def _():
        o_ref[...] = jnp.zeros(o_ref.shape, F32)

    x = x_ref[...]
    hg = _dot3(x, wg_ref[...])
    hu = _dot3(x, wu_ref[...])
    gate = gate_ref[...]
    lane = lax.broadcasted_iota(jnp.int32, gate.shape, 1)
    gcol = jnp.sum(jnp.where(lane == e, gate, 0.0), axis=-1, keepdims=True)
    o_ref[...] += _dot3((hg * _sigmoid(hg)) * hu * gcol, wd_ref[...])


def moe_dense_precise(x, router_wt, r_bias_col, w_g, w_u, w_d, layer):
    n, d = x.shape
    f = w_g.shape[3]
    ft = MOE_DENSE_FT
    e, w_sel, _, _ = router(x, router_wt, r_bias_col)
    lanes = jnp.arange(LANES, dtype=jnp.int32)[None, :]
    gate = sum(jnp.where(e[k][:, None] == lanes, w_sel[k][:, None], 0.0) for k in range(TOP_K))
    return pl.pallas_call(
        _moe_dense_kernel,
        grid=(N_EXPERTS, f // ft),
        in_specs=[pl.BlockSpec((n, d), lambda e, j: (0, 0)),
                  pl.BlockSpec((n, LANES), lambda e, j: (0, 0)),
                  pl.BlockSpec((None, None, d, ft), lambda e, j: (layer, e, 0, j)),
                  pl.BlockSpec((None, None, d, ft), lambda e, j: (layer, e, 0, j)),
                  pl.BlockSpec((None, None, ft, d), lambda e, j: (layer, e, j, 0))],
        out_specs=pl.BlockSpec((n, d), lambda e, j: (0, 0)),
        out_shape=jax.ShapeDtypeStruct((n, d), F32),
        compiler_params=_cparams("arbitrary", "arbitrary"),
        name="moe_dense_precise",
    )(x, gate, w_g, w_u, w_d)


def moe(x, router_wt, r_bias_col, w_g, w_u, w_d, layer):
    n, d = x.shape
    e, w_sel, rank, counts = router(x, router_wt, r_bias_col)
    n_asg = n * TOP_K
    gsz = (counts + MOE_TM - 1) // MOE_TM * MOE_TM
    p_end = jnp.cumsum(gsz)
    p_start = p_end - gsz
    onehot = (e[:, :, None] == jnp.arange(N_EXPERTS, dtype=jnp.int32)[None, None, :]).astype(jnp.int32)
    dest = jnp.sum(onehot * p_start[None, None, :], axis=2) + rank
    a_pad = -(-(n_asg + N_EXPERTS * (MOE_TM - 1)) // MOE_TM) * MOE_TM
    tok = jnp.broadcast_to(jnp.arange(n, dtype=jnp.int32)[None, :], (TOP_K, n))
    src_tok = jnp.zeros((a_pad,), jnp.int32).at[dest.reshape(n_asg)].set(tok.reshape(n_asg))
    row_w = jnp.zeros((a_pad,), F32).at[dest.reshape(n_asg)].set(w_sel.reshape(n_asg))
    tile_start = jnp.arange(a_pad // MOE_TM, dtype=jnp.int32) * MOE_TM
    tile_expert = jnp.minimum(jnp.sum((p_end[None, :] <= tile_start[:, None]).astype(jnp.int32), axis=1),
                              N_EXPERTS - 1).astype(jnp.int32)
    ys = expert_ffn(x[src_tok], row_w.reshape(a_pad, 1), tile_expert, w_g, w_u, w_d, layer)
    return ys[dest[0]] + ys[dest[1]]


def kernel(x_prompt, x_sample, cache_nsa_kv, cache_win_kv, state_rwkv, state_rwkv_shift, state_lru_h, state_lru_conv,
           page_table, ln1_g, ln1_b, ln2_g, ln2_b, rec_w_in, rec_mu, rwkv_w0, rwkv_w_up, rwkv_a0, rwkv_a_up, rwkv_g_up,
           rwkv_k_k, rwkv_k_a, rwkv_r_k, rwkv_gn_g, rwkv_gn_b, lru_conv_w, lru_conv_b, lru_wa, lru_ba, lru_wx, lru_bx,
           lru_lambda, rec_w_out, attn_w_in, attn_gate_b, w_cmp_k, w_cmp_v, attn_w_out, moe_w_router, moe_router_bias,
           moe_w_gate, moe_w_up, moe_w_down):
    Bp, Tp, D = x_prompt.shape
    Bs, Ts, _ = x_sample.shape
    n_p, n_s = Bp * Tp, Bs * Ts
    rec_args = (rec_w_in, rec_mu, rwkv_w0, rwkv_w_up, rwkv_a0, rwkv_a_up, rwkv_g_up, rwkv_k_k, rwkv_k_a,
                rwkv_r_k, rwkv_gn_g, rwkv_gn_b, lru_conv_w, lru_conv_b, lru_wa, lru_ba, lru_wx, lru_bx, lru_lambda,
                rec_w_out)
    xp, xs = x_prompt.reshape(n_p, D), x_sample.reshape(n_s, D)
    router_w = moe_w_router.T
    moe_router_bias = moe_router_bias.reshape(N_EXPERTS, 1)
    wg_bf, wu_bf, wd_bf = moe_w_gate.astype(BF16), moe_w_up.astype(BF16), moe_w_down.astype(BF16)
    nsa_p, nsa_s, win_p, win_s = [], [], [], []
    rs_p, rs_s, rsh_p, rsh_s, lh_p, lh_s, lc_p, lc_s = [], [], [], [], [], [], [], []
    for layer in range(DEPTH):
        li = layer // 2
        if layer % 2 == 0:
            prm = rec_params(li, *rec_args)
            yp, S, sh, h, cb = rec_mixer(xp, Bp, Tp, jnp.zeros((Bp, RWKV_HEADS, RWKV_HD, RWKV_HD), F32),
                                         jnp.zeros((Bp, RWKV_SHIFT_W), F32), jnp.zeros((Bp, LRU_W), F32),
                                         jnp.zeros((Bp, CONV_W - 1, LRU_W), F32), prm)
            ys, S2, sh2, h2, cb2 = rec_mixer(xs, Bs, Ts, state_rwkv[:, li], state_rwkv_shift[:, li],
                                             state_lru_h[:, li], state_lru_conv[:, li], prm, precise=True)
            rs_p.append(S); rs_s.append(S2); rsh_p.append(sh); rsh_s.append(sh2)
            lh_p.append(h); lh_s.append(h2); lc_p.append(cb); lc_s.append(cb2)
        else:
            prm = attn_params(li, attn_w_in, attn_gate_b, w_cmp_k, w_cmp_v, attn_w_out)
            yp, rows, wb = attn_prompt(xp, Bp, Tp, prm)
            ys, rows2, wb2 = attn_sample(xs, Bs, Ts, prm, cache_nsa_kv, cache_win_kv[:, :, li], page_table, li)
            nsa_p.append(rows); nsa_s.append(rows2); win_p.append(wb); win_s.append(wb2)
        xp = ln_res(xp, yp, ln1_g[layer], ln1_b[layer])
        xs = ln_res(xs, ys, ln1_g[layer], ln1_b[layer])
        xp = ln_res(xp, moe(xp, router_w, moe_router_bias, wg_bf, wu_bf, wd_bf, layer), ln2_g[layer], ln2_b[layer])
        xs = ln_res(xs, moe_dense_precise(xs, router_w, moe_router_bias, moe_w_gate, moe_w_up, moe_w_down, layer),
                    ln2_g[layer], ln2_b[layer])
    return (xp.reshape(Bp, Tp, D), xs.reshape(Bs, Ts, D),
            jnp.stack(nsa_p, axis=2), jnp.stack(nsa_s, axis=2), jnp.stack(win_p, axis=2), jnp.stack(win_s, axis=2),
            jnp.stack(rs_p, axis=1), jnp.stack(rs_s, axis=1), jnp.stack(rsh_p, axis=1), jnp.stack(rsh_s, axis=1),
            jnp.stack(lh_p, axis=1), jnp.stack(lh_s, axis=1), jnp.stack(lc_p, axis=1), jnp.stack(lc_s, axis=1))
```
